```python
import math
import jax, jax.numpy as jnp
from jax import lax
import numpy as np

D_MODEL = 2048
BATCH = 4
SEQ = 2048
DEPTH = 1
DEC_BATCH = 128
DEC_SEQ = 4
PAST_LEN = 16384
PAGE_SIZE = 128

PLE_DIM = 256
HG_HEADS = 8
HG_K = 128
HG_V = 128
HG_WIDTH = HG_HEADS * HG_V
GLA_HEADS = 4
GLA_K = 128
GLA_V = 256
GLA_KW = GLA_HEADS * GLA_K
GLA_VW = GLA_HEADS * GLA_V
GLA_GATE_RANK = 16
GLA_GATE_NORM = 16.0
BRANCH_W = 1024
N_BRANCH = 2
D_FF = -(-8 * D_MODEL // (3 * 256)) * 256
CHUNK = 32
EPS = 1e-6
IN_SIZES = (HG_HEADS * HG_K, HG_HEADS * HG_K, HG_WIDTH, HG_WIDTH,
            GLA_KW, GLA_KW, GLA_VW, GLA_VW, GLA_GATE_RANK,
            N_BRANCH * D_MODEL)
IN_COLS = sum(IN_SIZES)

kernel_name = "hgrn2_gla_parallel_decoder_step"


def rmsnorm(x, g):
    xf = x.astype(jnp.float32)
    y = xf * lax.rsqrt(jnp.mean(xf * xf, axis=-1, keepdims=True) + EPS)
    return (y * g.astype(jnp.float32)).astype(x.dtype)


def rms_head(o, g):
    return o * lax.rsqrt(jnp.mean(o * o, axis=-1, keepdims=True) + EPS) * g.astype(jnp.float32)


def gated_linear_scan(q, k, v, logf, s0):
    b_, t_, h_, _ = q.shape
    dv = v.shape[-1]
    c = math.gcd(t_, CHUNK)
    n = t_ // c
    rs = lambda a: a.reshape(b_, n, c, h_, a.shape[-1])
    q, k, v, logf = rs(q), rs(k), rs(v), rs(logf)
    cum = jnp.cumsum(logf, axis=2)
    last = cum[:, :, -1:]
    ref = cum[:, :, c // 2:c // 2 + 1]
    qe = q * jnp.exp(cum - ref)
    ke = k * jnp.exp(ref - cum)
    att = jnp.einsum('bnthk,bnshk->bnhts', qe, ke)
    mask = jnp.tril(jnp.ones((c, c), dtype=bool))
    att = jnp.where(mask, att, 0.0)
    o_intra = jnp.einsum('bnhts,bnshv->bnthv', att, v)
    q_in = q * jnp.exp(cum)
    k_out = k * jnp.exp(last - cum)
    decay = jnp.exp(last[:, :, 0])

    def step(s, inp):
        qi, ki, vi, di = inp
        o = jnp.einsum('bthk,bhkv->bthv', qi, s)
        s = s * di[..., None] + jnp.einsum('bthk,bthv->bhkv', ki, vi)
        return s, o

    xs = (jnp.moveaxis(q_in, 1, 0), jnp.moveaxis(k_out, 1, 0),
          jnp.moveaxis(v, 1, 0), jnp.moveaxis(decay, 1, 0))
    s_fin, o_inter = lax.scan(step, s0, xs)
    o = o_intra + jnp.moveaxis(o_inter, 0, 1)
    return o.reshape(b_, t_, h_, dv), s_fin


def hgrn2_branch(q_raw, f_raw, i_raw, g_raw, lb, gain, s0):
    bsz, t_, _ = q_raw.shape
    f32 = jnp.float32
    q = jax.nn.silu(q_raw.astype(f32).reshape(bsz, t_, HG_HEADS, HG_K)) * (HG_K ** -0.5)
    fr = f_raw.astype(f32).reshape(bsz, t_, HG_HEADS, HG_K)
    lbh = lb.reshape(HG_HEADS, HG_K)
    logf = jnp.log(lbh + (1.0 - lbh) * jax.nn.sigmoid(fr))
    k = (1.0 - lbh) * jax.nn.sigmoid(-fr)
    v = i_raw.astype(f32).reshape(bsz, t_, HG_HEADS, HG_V)
    o, s = gated_linear_scan(q, k, v, logf, s0.astype(f32))
    g = g_raw.astype(f32).reshape(bsz, t_, HG_HEADS, HG_V)
    o = rms_head(o, gain) * jax.nn.silu(g)
    return o.reshape(bsz, t_, HG_WIDTH), s


def gla_branch(q_raw, k_raw, v_raw, g_raw, gk_lr, w_gk, b_gk, gain, s0):
    bsz, t_, _ = q_raw.shape
    f32 = jnp.float32
    q = q_raw.astype(f32).reshape(bsz, t_, GLA_HEADS, GLA_K) * (GLA_K ** -0.5)
    k = k_raw.astype(f32).reshape(bsz, t_, GLA_HEADS, GLA_K)
    v = v_raw.astype(f32).reshape(bsz, t_, GLA_HEADS, GLA_V)
    gk = gk_lr.astype(f32) @ w_gk.astype(f32) + b_gk.astype(f32)
    logf = (jax.nn.log_sigmoid(gk) / GLA_GATE_NORM).reshape(bsz, t_, GLA_HEADS, GLA_K)
    o, s = gated_linear_scan(q, k, v, logf, s0.astype(f32))
    g = g_raw.astype(f32).reshape(bsz, t_, GLA_HEADS, GLA_V)
    o = rms_head(o, gain) * jax.nn.silu(g)
    return o.reshape(bsz, t_, GLA_VW), s


def decoder_layer(x, p, s_hg, s_gla, lb, ln1, w_in, hg_norm, gla_w_gk, gla_b_gk, gla_norm,
                  w_branch, w_out, ln2, w_gu, w_down, ln3, w_ple, w_pg):
    bsz, t_, _ = x.shape
    h = rmsnorm(x, ln1)
    z = h @ w_in
    splits = [int(c) for c in np.cumsum(IN_SIZES)[:-1]]
    hq, hf, hi, hgt, gq, gk, gv, gg, glr, mg = jnp.split(z, splits, axis=-1)
    a, s_hg = hgrn2_branch(hq, hf, hi, hgt, lb, hg_norm, s_hg)
    b, s_gla = gla_branch(gq, gk, gv, gg, glr, gla_w_gk, gla_b_gk, gla_norm, s_gla)
    br = jnp.stack([a, b], axis=0).astype(x.dtype)
    up = jnp.einsum('nbtw,nwd->nbtd', br, w_branch)
    gates = jax.nn.sigmoid(mg.astype(jnp.float32).reshape(bsz, t_, N_BRANCH, D_MODEL))
    merged = jnp.einsum('btnd,nbtd->btd', gates, up.astype(jnp.float32)).astype(x.dtype)
    x = x + merged @ w_out
    h2 = rmsnorm(x, ln2)
    gate, upv = jnp.split(h2 @ w_gu, 2, axis=-1)
    x = x + (jax.nn.silu(gate) * upv) @ w_down
    h3 = rmsnorm(x, ln3)
    x = x + jax.nn.sigmoid(h3 @ w_pg) * (p.astype(x.dtype) @ w_ple)
    return x, s_hg, s_gla


def setup_inputs(seed: int = 0) -> dict:
    key = jax.random.key(seed)
    ks = jax.random.split(key, 24)
    f32 = jnp.float32
    nrm = lambda k, shape, s: jax.random.normal(k, shape, f32) * s
    gain = lambda k, shape: 1.0 + 0.02 * jax.random.normal(k, shape, f32)
    return {
        "x_prompt": nrm(ks[0], (BATCH, SEQ, D_MODEL), 1.0),
        "x_sample": nrm(ks[1], (DEC_BATCH, DEC_SEQ, D_MODEL), 1.0),
        "state_hgrn": nrm(ks[2], (DEPTH, DEC_BATCH, HG_HEADS, HG_K, HG_V), 0.5),
        "state_gla": nrm(ks[3], (DEPTH, DEC_BATCH, GLA_HEADS, GLA_K, GLA_V), 0.5),
        "p_prompt": nrm(ks[4], (DEPTH, BATCH, SEQ, PLE_DIM), 1.0),
        "p_sample": nrm(ks[5], (DEPTH, DEC_BATCH, DEC_SEQ, PLE_DIM), 1.0),
        "hg_lb": nrm(ks[6], (DEPTH + 1, HG_HEADS * HG_K), 0.1),
        "ln1": gain(ks[7], (DEPTH, D_MODEL)),
        "w_in": nrm(ks[8], (DEPTH, D_MODEL, IN_COLS), D_MODEL ** -0.5),
        "hg_norm": gain(ks[9], (DEPTH, HG_V)),
        "gla_w_gk": nrm(ks[10], (DEPTH, GLA_GATE_RANK, GLA_KW), GLA_GATE_RANK ** -0.5),
        "gla_b_gk": nrm(ks[11], (DEPTH, GLA_KW), 0.1),
        "gla_norm": gain(ks[12], (DEPTH, GLA_V)),
        "w_branch": nrm(ks[13], (DEPTH, N_BRANCH, BRANCH_W, D_MODEL), BRANCH_W ** -0.5),
        "w_out": nrm(ks[14], (DEPTH, D_MODEL, D_MODEL), D_MODEL ** -0.5),
        "ln2": gain(ks[15], (DEPTH, D_MODEL)),
        "w_gu": nrm(ks[16], (DEPTH, D_MODEL, 2 * D_FF), D_MODEL ** -0.5),
        "w_down": nrm(ks[17], (DEPTH, D_FF, D_MODEL), D_FF ** -0.5),
        "ln3": gain(ks[18], (DEPTH, D_MODEL)),
        "w_ple": nrm(ks[19], (DEPTH, PLE_DIM, D_MODEL), PLE_DIM ** -0.5),
        "w_pg": nrm(ks[20], (DEPTH, D_MODEL, D_MODEL), D_MODEL ** -0.5),
        "ln_f": gain(ks[21], (D_MODEL,)),
    }


def reference(x_prompt, x_sample, state_hgrn, state_gla, p_prompt, p_sample, hg_lb, ln1, w_in,
              hg_norm, gla_w_gk, gla_b_gk, gla_norm, w_branch, w_out, ln2, w_gu, w_down, ln3,
              w_ple, w_pg, ln_f):
    lb_all = jnp.cumsum(jax.nn.softmax(hg_lb.astype(jnp.float32), axis=0), axis=0)
    xp, xs = x_prompt, x_sample
    hp_list, gp_list, hs_list, gs_list = [], [], [], []
    for i in range(DEPTH):
        w = (lb_all[i], ln1[i], w_in[i], hg_norm[i], gla_w_gk[i], gla_b_gk[i], gla_norm[i],
             w_branch[i], w_out[i], ln2[i], w_gu[i], w_down[i], ln3[i], w_ple[i], w_pg[i])
        s_hg0 = jnp.zeros((BATCH, HG_HEADS, HG_K, HG_V), jnp.float32)
        s_gl0 = jnp.zeros((BATCH, GLA_HEADS, GLA_K, GLA_V), jnp.float32)
        xp, hp, gp = decoder_layer(xp, p_prompt[i], s_hg0, s_gl0, *w)
        xs, hs, gs = decoder_layer(xs, p_sample[i], state_hgrn[i], state_gla[i], *w)
        hp_list.append(hp.astype(state_hgrn.dtype))
        gp_list.append(gp.astype(state_gla.dtype))
        hs_list.append(hs.astype(state_hgrn.dtype))
        gs_list.append(gs.astype(state_gla.dtype))
    y_prompt = rmsnorm(xp, ln_f)
    y_sample = rmsnorm(xs, ln_f)
    new_hgrn_prompt = jnp.stack(hp_list, axis=0)
    new_gla_prompt = jnp.stack(gp_list, axis=0)
    new_hgrn_sample = jnp.stack(hs_list, axis=0)
    new_gla_sample = jnp.stack(gs_list, axis=0)
    return (y_prompt, y_sample, new_hgrn_prompt, new_gla_prompt, new_hgrn_sample, new_gla_sample)
```

```python
import functools

import jax
import jax.numpy as jnp
from jax import lax
from jax.experimental import pallas as pl
from jax.experimental.pallas import tpu as pltpu

F32 = jnp.float32
BF16 = jnp.bfloat16

EPS = 1e-6
LANE = 128
HG_HEADS, HG_K, HG_V = 8, 128, 128
GLA_HEADS, GLA_K, GLA_V = 4, 128, 256
GLA_GATE_RANK = 16
GLA_GATE_NORM = 16.0
CHUNK = 32
MAIN_COLS = 7168
VMEM_LIMIT = 56 * 1024 * 1024

NT = (((1,), (1,)), ((), ()))
TN = (((0,), (0,)), ((), ()))


def _params(n_axes, vmem=VMEM_LIMIT):
    return pltpu.CompilerParams(dimension_semantics=("arbitrary",) * n_axes,
                                vmem_limit_bytes=vmem)


def _dot(a, b):
    return jnp.dot(a, b, preferred_element_type=F32)


def _rms(x, g):
    return x * lax.rsqrt(jnp.mean(x * x, axis=-1, keepdims=True) + EPS) * g


def _prep_kernel(xp_ref, xs_ref, ln_ref, wglr_ref, h_ref, glr_ref, *, n_prompt_tiles):
    i = pl.program_id(0)

    def body(x):
        h = _rms(x, ln_ref[...]).astype(BF16)
        h_ref[...] = h
        glr_ref[...] = _dot(h, wglr_ref[...])

    @pl.when(i < n_prompt_tiles)
    def _():
        body(xp_ref[...])

    @pl.when(i >= n_prompt_tiles)
    def _():
        body(xs_ref[...])


def _prep(xp, xs, ln1, wglr, tm):
    npr, d = xp.shape
    ns = xs.shape[0]
    npt, nst = npr // tm, ns // tm
    m = npr + ns
    return pl.pallas_call(
        functools.partial(_prep_kernel, n_prompt_tiles=npt),
        grid=(npt + nst,),
        in_specs=[
            pl.BlockSpec((tm, d), lambda i: (jnp.minimum(i, npt - 1), 0)),
            pl.BlockSpec((tm, d), lambda i: (jnp.maximum(i - npt, 0), 0)),
            pl.BlockSpec((1, d), lambda i: (0, 0)),
            pl.BlockSpec((d, LANE), lambda i: (0, 0)),
        ],
        out_specs=[
            pl.BlockSpec((tm, d), lambda i: (i, 0)),
            pl.BlockSpec((tm, LANE), lambda i: (i, 0)),
        ],
        out_shape=[jax.ShapeDtypeStruct((m, d), BF16),
                   jax.ShapeDtypeStruct((m, LANE), F32)],
        compiler_params=_params(1),
        name="prep",
    )(xp, xs, ln1, wglr)


def _in_proj_kernel(h_ref, w_ref, z_ref, wbf_ref):
    @pl.when(pl.program_id(1) == 0)
    def _():
        wbf_ref[...] = w_ref[...].astype(BF16)

    z_ref[...] = _dot(h_ref[...], wbf_ref[...]).astype(z_ref.dtype)


def _in_proj(h1, w_in, tm, tn):
    m, d = h1.shape
    return pl.pallas_call(
        _in_proj_kernel,
        grid=(MAIN_COLS // tn, m // tm),
        in_specs=[
            pl.BlockSpec((tm, d), lambda j, i: (i, 0)),
            pl.BlockSpec((d, tn), lambda j, i: (0, j)),
        ],
        out_specs=pl.BlockSpec((tm, tn), lambda j, i: (i, j)),
        out_shape=jax.ShapeDtypeStruct((m, MAIN_COLS), BF16),
        scratch_shapes=[pltpu.VMEM((d, tn), BF16)],
        compiler_params=_params(2),
        name="in_proj",
    )(h1, w_in)


def _cumsum_rows(x, length):
    row = lax.broadcasted_iota(jnp.int32, x.shape, 0) % length
    s = 1
    while s < length:
        x = x + jnp.where(row >= s, pltpu.roll(x, s, axis=0), 0.0)
        s *= 2
    return x


def _forget_lower_bound(lb_rows):
    mx = jnp.max(lb_rows, axis=0, keepdims=True)
    e = jnp.exp(lb_rows - mx)
    return e[0:1, :] / jnp.sum(e, axis=0, keepdims=True)


def _col_broadcast(row):
    return jnp.broadcast_to(row, (LANE, LANE)).T


def _chunk_step(q, k, v_b, logf, s, tril, mid):
    c = q.shape[0]
    cum = _cumsum_rows(logf, c)
    ref = cum[mid:mid + 1, :]
    last = cum[c - 1:c, :]
    qe = q * jnp.exp(cum - ref)
    ke = k * jnp.exp(ref - cum)
    q_in = (qe * jnp.exp(ref)).astype(BF16)
    k_out = (ke * jnp.exp(last - ref)).astype(BF16)
    att = lax.dot_general(qe.astype(BF16), ke.astype(BF16), NT, preferred_element_type=F32)
    att = jnp.where(tril, att, 0.0).astype(BF16)
    o = _dot(att, v_b) + _dot(q_in, s.astype(BF16))
    kv = lax.dot_general(k_out, v_b, TN, preferred_element_type=F32)
    dcol = _col_broadcast(jnp.exp(last))
    if s.shape[1] != LANE:
        dcol = jnp.concatenate([dcol] * (s.shape[1] // LANE), axis=1)
    return o, s * dcol + kv


def _gated_out(o, g, gain):
    return (_rms(o, gain) * jax.nn.silu(g)).astype(BF16)


def _hgrn_prompt_kernel(lb_ref, gain_ref, q_ref, f_ref, i_ref, g_ref, o_ref, s_ref, *, hpb):
    t = q_ref.shape[0]
    lb = _forget_lower_bound(lb_ref[...])
    gain = gain_ref[...]
    tril = (lax.broadcasted_iota(jnp.int32, (CHUNK, CHUNK), 0)
            >= lax.broadcasted_iota(jnp.int32, (CHUNK, CHUNK), 1))
    s_ref[...] = jnp.zeros_like(s_ref)

    def chunk(n, carry):
        rows = pl.ds(pl.multiple_of(n * CHUNK, CHUNK), CHUNK)
        for h in range(hpb):
            cs = slice(h * HG_K, (h + 1) * HG_K)
            lbh = lb[:, cs]
            q = jax.nn.silu(q_ref[rows, cs].astype(F32)) * (HG_K ** -0.5)
            fr = f_ref[rows, cs].astype(F32)
            logf = jnp.log(lbh + (1.0 - lbh) * jax.nn.sigmoid(fr))
            k = (1.0 - lbh) * jax.nn.sigmoid(-fr)
            o, s_new = _chunk_step(q, k, i_ref[rows, cs], logf, s_ref[0, h], tril, CHUNK // 2)
            s_ref[0, h] = s_new
            o_ref[rows, cs] = _gated_out(o, g_ref[rows, cs].astype(F32), gain)
        return carry

    lax.fori_loop(0, t // CHUNK, chunk, 0)


def _hgrn_prompt(z, hg_lb, gain, batch, seq, hpb):
    w = hpb * HG_K
    sec = (HG_HEADS * HG_K) // w
    zspec = lambda s: pl.BlockSpec((seq, w), lambda b, g, s=s: (b, s * sec + g))
    return pl.pallas_call(
        functools.partial(_hgrn_prompt_kernel, hpb=hpb),
        grid=(batch, HG_HEADS // hpb),
        in_specs=[
            pl.BlockSpec((hg_lb.shape[0], w), lambda b, g: (0, g)),
            pl.BlockSpec((1, HG_V), lambda b, g: (0, 0)),
            zspec(0), zspec(1), zspec(2), zspec(3),
        ],
        out_specs=[
            pl.BlockSpec((seq, w), lambda b, g: (b, g)),
            pl.BlockSpec((1, hpb, HG_K, HG_V), lambda b, g: (b, g, 0, 0)),
        ],
        out_shape=[jax.ShapeDtypeStruct((z.shape[0], HG_HEADS * HG_V), BF16),
                   jax.ShapeDtypeStruct((batch, HG_HEADS, HG_K, HG_V), F32)],
        compiler_params=_params(2),
        name="hgrn_prompt",
    )(hg_lb, gain, z, z, z, z)


def _gla_prompt_kernel(gain_ref, wgk_ref, bgk_ref, glr_ref, q_ref, k_ref, v_ref, g_ref,
                       o_ref, s_ref, logf_ref, *, hpb):
    t = q_ref.shape[0]
    gain = gain_ref[...]
    tril = (lax.broadcasted_iota(jnp.int32, (CHUNK, CHUNK), 0)
            >= lax.broadcasted_iota(jnp.int32, (CHUNK, CHUNK), 1))
    s_ref[...] = jnp.zeros_like(s_ref)
    gk = _dot(glr_ref[...].astype(BF16), wgk_ref[...].astype(BF16)) + bgk_ref[...]
    logf_ref[...] = jax.nn.log_sigmoid(gk) / GLA_GATE_NORM

    def chunk(n, carry):
        rows = pl.ds(pl.multiple_of(n * CHUNK, CHUNK), CHUNK)
        for h in range(hpb):
            ks = slice(h * GLA_K, (h + 1) * GLA_K)
            vs = slice(h * GLA_V, (h + 1) * GLA_V)
            q = q_ref[rows, ks].astype(F32) * (GLA_K ** -0.5)
            k = k_ref[rows, ks].astype(F32)
            o, s_new = _chunk_step(q, k, v_ref[rows, vs], logf_ref[rows, ks], s_ref[0, h], tril,
                                   CHUNK // 2)
            s_ref[0, h] = s_new
            o_ref[rows, vs] = _gated_out(o, g_ref[rows, vs].astype(F32), gain)
        return carry

    lax.fori_loop(0, t // CHUNK, chunk, 0)


def _gla_prompt(z, glr, wgk, bgk, gain, batch, seq, hpb):
    kw, vw = hpb * GLA_K, hpb * GLA_V
    q0 = (4 * HG_HEADS * HG_K) // kw
    k0 = q0 + (GLA_HEADS * GLA_K) // kw
    v0 = (4 * HG_HEADS * HG_K + 2 * GLA_HEADS * GLA_K) // vw
    g0 = v0 + (GLA_HEADS * GLA_V) // vw
    return pl.pallas_call(
        functools.partial(_gla_prompt_kernel, hpb=hpb),
        grid=(batch, GLA_HEADS // hpb),
        in_specs=[
            pl.BlockSpec((1, GLA_V), lambda b, g: (0, 0)),
            pl.BlockSpec((LANE, kw), lambda b, g: (0, g)),
            pl.BlockSpec((1, kw), lambda b, g: (0, g)),
            pl.BlockSpec((seq, LANE), lambda b, g: (b, 0)),
            pl.BlockSpec((seq, kw), lambda b, g: (b, q0 + g)),
            pl.BlockSpec((seq, kw), lambda b, g: (b, k0 + g)),
            pl.BlockSpec((seq, vw), lambda b, g: (b, v0 + g)),
            pl.BlockSpec((seq, vw), lambda b, g: (b, g0 + g)),
        ],
        out_specs=[
            pl.BlockSpec((seq, vw), lambda b, g: (b, g)),
            pl.BlockSpec((1, hpb, GLA_K, GLA_V), lambda b, g: (b, g, 0, 0)),
        ],
        out_shape=[jax.ShapeDtypeStruct((z.shape[0], GLA_HEADS * GLA_V), BF16),
                   jax.ShapeDtypeStruct((batch, GLA_HEADS, GLA_K, GLA_V), F32)],
        scratch_shapes=[pltpu.VMEM((seq, kw), F32)],
        compiler_params=_params(2),
        name="gla_prompt",
    )(gain, wgk, bgk, glr, z, z, z, z)


def _seg_pick(x, seg, idx):
    n = x.shape[0]
    rowmod = lax.broadcasted_iota(jnp.int32, x.shape, 0) % seg
    out = x
    for m in range(seg):
        if m != idx:
            out = jnp.where(rowmod == m, pltpu.roll(x, (m - idx) % n, axis=0), out)
    return out


def _sample_branch(q, k, v_b, logf, g, gain, s_ref, ns_ref, o_ref, o_scr, *, heads, kdim, vdim,
                   seq, pairs):
    tile = 2 * seq
    cum = _cumsum_rows(logf, seq)
    ref = _seg_pick(cum, seq, seq // 2)
    last = _seg_pick(cum, seq, seq - 1)
    qe = (q * jnp.exp(cum - ref)).astype(BF16)
    ke = (k * jnp.exp(ref - cum)).astype(BF16)
    q_in = (q * jnp.exp(cum)).astype(BF16)
    k_out = k * jnp.exp(last - cum)
    dec = jnp.exp(last)
    r_i = lax.broadcasted_iota(jnp.int32, (tile, tile), 0)
    c_i = lax.broadcasted_iota(jnp.int32, (tile, tile), 1)
    amask = (r_i // seq == c_i // seq) & (c_i <= r_i)
    half = lax.broadcasted_iota(jnp.int32, (tile, 1), 0) // seq
    for p in range(pairs):
        rs = slice(p * tile, (p + 1) * tile)
        for h in range(heads):
            ks = slice(h * kdim, (h + 1) * kdim)
            vs = slice(h * vdim, (h + 1) * vdim)
            v_t = v_b[rs, vs]
            att = lax.dot_general(qe[rs, ks], ke[rs, ks], NT, preferred_element_type=F32)
            o = _dot(jnp.where(amask, att, 0.0).astype(BF16), v_t)
            for bi in range(2):
                b = 2 * p + bi
                s0 = s_ref[0, b, h]
                o = o + jnp.where(half == bi, _dot(q_in[rs, ks], s0.astype(BF16)), 0.0)
                km = jnp.where(half == bi, k_out[rs, ks], 0.0).astype(BF16)
                kv = lax.dot_general(km, v_t, TN, preferred_element_type=F32)
                r0 = p * tile + bi * seq
                dcol = _col_broadcast(dec[r0:r0 + 1, ks])
                if vdim != LANE:
                    dcol = jnp.concatenate([dcol] * (vdim // LANE), axis=1)
                ns_ref[0, b, h] = s0 * dcol + kv
            o_scr[rs, vs] = o
    for h in range(heads):
        vs = slice(h * vdim, (h + 1) * vdim)
        o_ref[:, vs] = _gated_out(o_scr[:, vs], g[:, vs].astype(F32), gain)


def _sample_kernel(lb_ref, hgain_ref, ggain_ref, wgk_ref, bgk_ref,
                   hq_ref, hf_ref, hi_ref, hg_ref, gq_ref, gk_ref, gv_ref, gg_ref, glr_ref,
                   sh_ref, sg_ref, a_alias, b_alias,
                   a_ref, b_ref, nsh_ref, nsg_ref, oh_scr, og_scr, *, seq, pairs):
    del a_alias, b_alias
    lb = _forget_lower_bound(lb_ref[...])
    fr = hf_ref[...].astype(F32)
    _sample_branch(
        jax.nn.silu(hq_ref[...].astype(F32)) * (HG_K ** -0.5),
        (1.0 - lb) * jax.nn.sigmoid(-fr),
        hi_ref[...],
        jnp.log(lb + (1.0 - lb) * jax.nn.sigmoid(fr)),
        hg_ref[...], hgain_ref[...], sh_ref, nsh_ref, a_ref, oh_scr,
        heads=HG_HEADS, kdim=HG_K, vdim=HG_V, seq=seq, pairs=pairs)
    gk = _dot(glr_ref[...].astype(BF16), wgk_ref[...].astype(BF16)) + bgk_ref[...]
    _sample_branch(
        gq_ref[...].astype(F32) * (GLA_K ** -0.5),
        gk_ref[...].astype(F32),
        gv_ref[...],
        jax.nn.log_sigmoid(gk) / GLA_GATE_NORM,
        gg_ref[...], ggain_ref[...], sg_ref, nsg_ref, b_ref, og_scr,
        heads=GLA_HEADS, kdim=GLA_K, vdim=GLA_V, seq=seq, pairs=pairs)


def _sample_scan(z, glr, a, b, state_hgrn, state_gla, hg_lb, hgain, ggain, wgk, bgk,
                 row0, nb, seq, bb):
    rows = bb * seq
    rb0 = row0 // rows
    hw, kw, vw = HG_HEADS * HG_K, GLA_HEADS * GLA_K, GLA_HEADS * GLA_V
    gq0 = (4 * hw) // kw
    gv0 = (4 * hw + 2 * kw) // vw
    zs = lambda w, c: pl.BlockSpec((rows, w), lambda i, c=c: (rb0 + i, c))
    const = lambda shape: pl.BlockSpec(shape, lambda i: (0,) * len(shape))
    st = lambda h, k, v: pl.BlockSpec((1, bb, h, k, v), lambda i: (0, i, 0, 0, 0))
    any_spec = pl.BlockSpec(memory_space=pl.ANY)
    return pl.pallas_call(
        functools.partial(_sample_kernel, seq=seq, pairs=bb // 2),
        grid=(nb // bb,),
        in_specs=[
            const(hg_lb.shape), const((1, HG_V)), const((1, GLA_V)), const((LANE, kw)),
            const((1, kw)),
            zs(hw, 0), zs(hw, 1), zs(hw, 2), zs(hw, 3),
            zs(kw, gq0), zs(kw, gq0 + 1), zs(vw, gv0), zs(vw, gv0 + 1),
            pl.BlockSpec((rows, LANE), lambda i: (rb0 + i, 0)),
            st(HG_HEADS, HG_K, HG_V), st(GLA_HEADS, GLA_K, GLA_V),
            any_spec, any_spec,
        ],
        out_specs=[
            pl.BlockSpec((rows, hw), lambda i: (rb0 + i, 0)),
            pl.BlockSpec((rows, vw), lambda i: (rb0 + i, 0)),
            st(HG_HEADS, HG_K, HG_V), st(GLA_HEADS, GLA_K, GLA_V),
        ],
        out_shape=[jax.ShapeDtypeStruct(a.shape, a.dtype), jax.ShapeDtypeStruct(b.shape, b.dtype),
                   jax.ShapeDtypeStruct(state_hgrn.shape, state_hgrn.dtype),
                   jax.ShapeDtypeStruct(state_gla.shape, state_gla.dtype)],
        scratch_shapes=[pltpu.VMEM((rows, hw), F32), pltpu.VMEM((rows, vw), F32)],
        input_output_aliases={16: 0, 17: 1},
        compiler_params=_params(1),
        name="sample_scan",
    )(hg_lb, hgain, ggain, wgk, bgk, z, z, z, z, z, z, z, z, glr, state_hgrn, state_gla, a, b)


def _merge_kernel(h_ref, a_ref, b_ref, wg0_ref, wg1_ref, wb0_ref, wb1_ref, o_ref, wb_scr):
    @pl.when(pl.program_id(1) == 0)
    def _():
        wb_scr[0] = wb0_ref[0].astype(BF16)
        wb_scr[1] = wb1_ref[0].astype(BF16)

    h = h_ref[...]
    up0 = _dot(a_ref[...], wb_scr[0])
    m = jax.nn.sigmoid(_dot(h, wg0_ref[...])) * up0
    up1 = _dot(b_ref[...], wb_scr[1])
    m = m + jax.nn.sigmoid(_dot(h, wg1_ref[...])) * up1
    o_ref[...] = m.astype(o_ref.dtype)


def _merge(h1, a, b, w_mg, w_branch, tm, tn):
    m, d = h1.shape
    bw = a.shape[1]
    nj = d // tn
    return pl.pallas_call(
        _merge_kernel,
        grid=(nj, m // tm),
        in_specs=[
            pl.BlockSpec((tm, d), lambda j, i: (i, 0)),
            pl.BlockSpec((tm, bw), lambda j, i: (i, 0)),
            pl.BlockSpec((tm, bw), lambda j, i: (i, 0)),
            pl.BlockSpec((d, tn), lambda j, i: (0, j)),
            pl.BlockSpec((d, tn), lambda j, i: (0, nj + j)),
            pl.BlockSpec((1, bw, tn), lambda j, i: (0, 0, j)),
            pl.BlockSpec((1, bw, tn), lambda j, i: (1, 0, j)),
        ],
        out_specs=pl.BlockSpec((tm, tn), lambda j, i: (i, j)),
        out_shape=jax.ShapeDtypeStruct((m, d), BF16),
        scratch_shapes=[pltpu.VMEM((2, bw, tn), BF16)],
        compiler_params=_params(2),
        name="merge",
    )(h1, a, b, w_mg, w_mg, w_branch, w_branch)


def _out_proj_kernel(m_ref, xp_ref, xs_ref, w_ref, ln_ref, x1_ref, h2_ref, *, n_prompt_tiles):
    i = pl.program_id(0)
    y = _dot(m_ref[...], w_ref[...])

    def body(x):
        x1 = x + y
        x1_ref[...] = x1
        h2_ref[...] = _rms(x1, ln_ref[...]).astype(BF16)

    @pl.when(i < n_prompt_tiles)
    def _():
        body(xp_ref[...])

    @pl.when(i >= n_prompt_tiles)
    def _():
        body(xs_ref[...])


def _out_proj(merged, xp, xs, w_out, ln2, tm):
    m, d = merged.shape
    npt = xp.shape[0] // tm
    return pl.pallas_call(
        functools.partial(_out_proj_kernel, n_prompt_tiles=npt),
        grid=(m // tm,),
        in_specs=[
            pl.BlockSpec((tm, d), lambda i: (i, 0)),
            pl.BlockSpec((tm, d), lambda i: (jnp.minimum(i, npt - 1), 0)),
            pl.BlockSpec((tm, d), lambda i: (jnp.maximum(i - npt, 0), 0)),
            pl.BlockSpec((d, d), lambda i: (0, 0), pipeline_mode=pl.Buffered(1)),
            pl.BlockSpec((1, d), lambda i: (0, 0)),
        ],
        out_specs=[pl.BlockSpec((tm, d), lambda i: (i, 0)),
                   pl.BlockSpec((tm, d), lambda i: (i, 0))],
        out_shape=[jax.ShapeDtypeStruct((m, d), F32), jax.ShapeDtypeStruct((m, d), BF16)],
        compiler_params=_params(1),
        name="out_proj",
    )(merged, xp, xs, w_out, ln2)


def _ffn_up_kernel(h_ref, wg_ref, wu_ref, o_ref, w_scr):
    @pl.when(pl.program_id(1) == 0)
    def _():
        w_scr[0] = wg_ref[...].astype(BF16)
        w_scr[1] = wu_ref[...].astype(BF16)

    h = h_ref[...]
    gate = _dot(h, w_scr[0])
    o_ref[...] = (jax.nn.silu(gate) * _dot(h, w_scr[1])).astype(o_ref.dtype)


def _ffn_up(h2, w_gu, tm, tf):
    m, d = h2.shape
    dff = w_gu.shape[1] // 2
    nj = dff // tf
    return pl.pallas_call(
        _ffn_up_kernel,
        grid=(nj, m // tm),
        in_specs=[
            pl.BlockSpec((tm, d), lambda j, i: (i, 0)),
            pl.BlockSpec((d, tf), lambda j, i: (0, j)),
            pl.BlockSpec((d, tf), lambda j, i: (0, nj + j)),
        ],
        out_specs=pl.BlockSpec((tm, tf), lambda j, i: (i, j)),
        out_shape=jax.ShapeDtypeStruct((m, dff), BF16),
        scratch_shapes=[pltpu.VMEM((2, d, tf), BF16)],
        compiler_params=_params(2),
        name="ffn_up",
    )(h2, w_gu, w_gu)


def _ffn_down_kernel(a_ref, w_ref, x_ref, o_ref, w_scr):
    @pl.when(pl.program_id(1) == 0)
    def _():
        w_scr[...] = w_ref[...].astype(BF16)

    o_ref[...] = x_ref[...] + _dot(a_ref[...], w_scr[...])


def _ffn_down(act, w_down, x1, tm, tn):
    m, dff = act.shape
    d = w_down.shape[1]
    return pl.pallas_call(
        _ffn_down_kernel,
        grid=(d // tn, m // tm),
        in_specs=[
            pl.BlockSpec((tm, dff), lambda j, i: (i, 0)),
            pl.BlockSpec((dff, tn), lambda j, i: (0, j)),
            pl.BlockSpec((tm, tn), lambda j, i: (i, j)),
        ],
        out_specs=pl.BlockSpec((tm, tn), lambda j, i: (i, j)),
        out_shape=jax.ShapeDtypeStruct((m, d), F32),
        scratch_shapes=[pltpu.VMEM((dff, tn), BF16)],
        compiler_params=_params(2),
        name="ffn_down",
    )(act, w_down, x1)


def _final_kernel(x_ref, pp_ref, ps_ref, wpg_ref, wple_ref, ln3_ref, lnf_ref, yp_ref, ys_ref, *,
                  n_prompt_tiles):
    i = pl.program_id(0)
    x2 = x_ref[...]
    gate = jax.nn.sigmoid(_dot(_rms(x2, ln3_ref[...]).astype(BF16), wpg_ref[...]))

    def body(p, y_ref):
        x3 = x2 + gate * _dot(p.astype(BF16), wple_ref[...])
        y_ref[...] = _rms(x3, lnf_ref[...])

    @pl.when(i < n_prompt_tiles)
    def _():
        body(pp_ref[...], yp_ref)

    @pl.when(i >= n_prompt_tiles)
    def _():
        body(ps_ref[...], ys_ref)


def _final(x2, pp, ps, w_pg, w_ple, ln3, ln_f, tm):
    m, d = x2.shape
    npr, ns = pp.shape[0], ps.shape[0]
    pd = pp.shape[1]
    npt = npr // tm
    pidx = lambda i: (jnp.minimum(i, npt - 1), 0)
    sidx = lambda i: (jnp.maximum(i - npt, 0), 0)
    return pl.pallas_call(
        functools.partial(_final_kernel, n_prompt_tiles=npt),
        grid=(m // tm,),
        in_specs=[
            pl.BlockSpec((tm, d), lambda i: (i, 0)),
            pl.BlockSpec((tm, pd), pidx),
            pl.BlockSpec((tm, pd), sidx),
            pl.BlockSpec((d, d), lambda i: (0, 0), pipeline_mode=pl.Buffered(1)),
            pl.BlockSpec((pd, d), lambda i: (0, 0)),
            pl.BlockSpec((1, d), lambda i: (0, 0)),
            pl.BlockSpec((1, d), lambda i: (0, 0)),
        ],
        out_specs=[pl.BlockSpec((tm, d), pidx), pl.BlockSpec((tm, d), sidx)],
        out_shape=[jax.ShapeDtypeStruct((npr, d), F32), jax.ShapeDtypeStruct((ns, d), F32)],
        compiler_params=_params(1),
        name="final",
    )(x2, pp, ps, w_pg, w_ple, ln3, ln_f)


def kernel(x_prompt, x_sample, state_hgrn, state_gla, p_prompt, p_sample, hg_lb, ln1, w_in, hg_norm,
           gla_w_gk, gla_b_gk, gla_norm, w_branch, w_out, ln2, w_gu, w_down, ln3, w_ple, w_pg, ln_f):
    batch, seq, d = x_prompt.shape
    nb, dseq, _ = x_sample.shape
    depth = w_in.shape[0]
    assert depth == 1, "single-layer step"
    npr, ns = batch * seq, nb * dseq
    xp = x_prompt.reshape(npr, d)
    xs = x_sample.reshape(ns, d)
    row = lambda v: v.reshape(1, -1)

    w_in0 = w_in[0]
    glr0 = MAIN_COLS
    mg0 = MAIN_COLS + GLA_GATE_RANK
    w_glr = jnp.pad(w_in0[:, glr0:mg0], ((0, 0), (0, LANE - GLA_GATE_RANK))).astype(BF16)
    w_mg = w_in0[:, mg0:].astype(BF16)
    wgk = jnp.pad(gla_w_gk[0], ((0, LANE - GLA_GATE_RANK), (0, 0)))
    bgk = row(gla_b_gk[0])

    h1, glr = _prep(xp, xs, row(ln1[0]), w_glr, tm=512)
    z = _in_proj(h1, w_in0, tm=1088, tn=1024)

    a, s_hp = _hgrn_prompt(z, hg_lb, row(hg_norm[0]), batch, seq, hpb=4)
    b, s_gp = _gla_prompt(z, glr, wgk, bgk, row(gla_norm[0]), batch, seq, hpb=2)
    a, b, s_hs, s_gs = _sample_scan(z, glr, a, b, state_hgrn, state_gla, hg_lb, row(hg_norm[0]),
                                    row(gla_norm[0]), wgk, bgk, row0=npr, nb=nb, seq=dseq, bb=8)

    merged = _merge(h1, a, b, w_mg, w_branch[0], tm=1088, tn=512)
    x1, h2 = _out_proj(merged, xp, xs, w_out[0].astype(BF16), row(ln2[0]), tm=512)
    act = _ffn_up(h2, w_gu[0], tm=1088, tf=512)
    x2 = _ffn_down(act, w_down[0], x1, tm=544, tn=512)
    yp, ys = _final(x2, p_prompt[0].reshape(npr, -1), p_sample[0].reshape(ns, -1),
                    w_pg[0].astype(BF16), w_ple[0].astype(BF16), row(ln3[0]), row(ln_f), tm=512)

    return (yp.reshape(batch, seq, d), ys.reshape(nb, dseq, d),
            s_hp[None], s_gp[None], s_hs, s_gs)
```

```python
import functools

import jax
import jax.numpy as jnp
from jax import lax
from jax.experimental import pallas as pl
from jax.experimental.pallas import tpu as pltpu

F32 = jnp.float32
BF16 = jnp.bfloat16

EPS = 1e-6
LANE = 128
HG_HEADS, HG_K, HG_V = 8, 128, 128
GLA_HEADS, GLA_K, GLA_V = 4, 128, 256
GLA_GATE_RANK = 16
GLA_GATE_NORM = 16.0
CHUNK = 32
MAIN_COLS = 7168
VMEM_LIMIT = 56 * 1024 * 1024

NT = (((1,), (1,)), ((), ()))
TN = (((0,), (0,)), ((), ()))


def _params(n_axes, vmem=VMEM_LIMIT):
    return pltpu.CompilerParams(dimension_semantics=("arbitrary",) * n_axes,
                                vmem_limit_bytes=vmem)


def _dot(a, b):
    return jnp.dot(a, b, preferred_element_type=F32)


def _rms(x, g):
    return x * lax.rsqrt(jnp.mean(x * x, axis=-1, keepdims=True) + EPS) * g


def _prep_kernel(xp_ref, xs_ref, ln_ref, wglr_ref, h_ref, glr_ref, *, n_prompt_tiles):
    i = pl.program_id(0)

    d = wglr_ref.shape[1]
    wglr = jnp.concatenate([wglr_ref[...], jnp.zeros((LANE - GLA_GATE_RANK, d), F32)], axis=0)
    wglr = wglr.astype(BF16)

    def body(x):
        h = _rms(x, ln_ref[...]).astype(BF16)
        h_ref[...] = h
        glr_ref[...] = lax.dot_general(h, wglr, NT, preferred_element_type=F32)

    @pl.when(i < n_prompt_tiles)
    def _():
        body(xp_ref[...])

    @pl.when(i >= n_prompt_tiles)
    def _():
        body(xs_ref[...])


def _prep(xp, xs, ln1, w_in_t, tm):
    npr, d = xp.shape
    ns = xs.shape[0]
    npt, nst = npr // tm, ns // tm
    m = npr + ns
    return pl.pallas_call(
        functools.partial(_prep_kernel, n_prompt_tiles=npt),
        grid=(npt + nst,),
        in_specs=[
            pl.BlockSpec((tm, d), lambda i: (jnp.minimum(i, npt - 1), 0)),
            pl.BlockSpec((tm, d), lambda i: (jnp.maximum(i - npt, 0), 0)),
            pl.BlockSpec((1, d), lambda i: (0, 0)),
            pl.BlockSpec((GLA_GATE_RANK, d), lambda i: (MAIN_COLS // GLA_GATE_RANK, 0)),
        ],
        out_specs=[
            pl.BlockSpec((tm, d), lambda i: (i, 0)),
            pl.BlockSpec((tm, LANE), lambda i: (i, 0)),
        ],
        out_shape=[jax.ShapeDtypeStruct((m, d), BF16),
                   jax.ShapeDtypeStruct((m, LANE), F32)],
        compiler_params=_params(1),
        name="prep",
    )(xp, xs, ln1, w_in_t)


TRANSPOSE_ROWS = 256


def _stage_transposed(w_t, dst_ref, n=None):
    for r in range(0, w_t.shape[0], TRANSPOSE_ROWS):
        blk = w_t[r:r + TRANSPOSE_ROWS, :].T.astype(BF16)
        if n is None:
            dst_ref[:, r:r + TRANSPOSE_ROWS] = blk
        else:
            dst_ref[n, :, r:r + TRANSPOSE_ROWS] = blk


def _in_proj_kernel(h_ref, w_ref, z_ref, wbf_ref):
    @pl.when(pl.program_id(1) == 0)
    def _():
        _stage_transposed(w_ref[...], wbf_ref)

    z_ref[...] = _dot(h_ref[...], wbf_ref[...]).astype(z_ref.dtype)


def _in_proj(h1, w_in_t, tm, tn):
    m, d = h1.shape
    return pl.pallas_call(
        _in_proj_kernel,
        grid=(MAIN_COLS // tn, m // tm),
        in_specs=[
            pl.BlockSpec((tm, d), lambda j, i: (i, 0)),
            pl.BlockSpec((tn, d), lambda j, i: (j, 0)),
        ],
        out_specs=pl.BlockSpec((tm, tn), lambda j, i: (i, j)),
        out_shape=jax.ShapeDtypeStruct((m, MAIN_COLS), BF16),
        scratch_shapes=[pltpu.VMEM((d, tn), BF16)],
        compiler_params=_params(2),
        name="in_proj",
    )(h1, w_in_t)


def _cumsum_rows(x, length):
    row = lax.broadcasted_iota(jnp.int32, x.shape, 0) % length
    s = 1
    while s < length:
        x = x + jnp.where(row >= s, pltpu.roll(x, s, axis=0), 0.0)
        s *= 2
    return x


def _forget_lower_bound(lb_rows):
    mx = jnp.max(lb_rows, axis=0, keepdims=True)
    e = jnp.exp(lb_rows - mx)
    return e[0:1, :] / jnp.sum(e, axis=0, keepdims=True)


def _col_broadcast(row):
    return jnp.broadcast_to(row, (LANE, LANE)).T


def _gated_out(o, g, gain):
    return (_rms(o, gain) * jax.nn.silu(g)).astype(BF16)


PREP_UNROLL = 4
SCAN_UNROLL = 2


def _chunk_rows(n):
    return pl.ds(pl.multiple_of(n * CHUNK, CHUNK), CHUNK)


def _store_chunk_factors(n, q, k, logf, qe_scr, ke_scr, qin_scr, kout_scr, dec_scr):
    rows = _chunk_rows(n)
    cum = _cumsum_rows(logf, CHUNK)
    ref = cum[CHUNK // 2:CHUNK // 2 + 1, :]
    last = cum[CHUNK - 1:CHUNK, :]
    qe = q * jnp.exp(cum - ref)
    ke = k * jnp.exp(ref - cum)
    qe_scr[rows, :] = qe.astype(BF16)
    ke_scr[rows, :] = ke.astype(BF16)
    qin_scr[rows, :] = (qe * jnp.exp(ref)).astype(BF16)
    kout_scr[rows, :] = (ke * jnp.exp(last - ref)).astype(BF16)
    dec_scr[pl.ds(n, 1), :] = jnp.exp(last)


def _scan_chunk(n, heads, kdim, vdim, qe_scr, ke_scr, qin_scr, kout_scr, dec_scr, v_ref, g_ref,
                gain, o_ref, s_ref):
    rows = _chunk_rows(n)
    tril = (lax.broadcasted_iota(jnp.int32, (CHUNK, CHUNK), 0)
            >= lax.broadcasted_iota(jnp.int32, (CHUNK, CHUNK), 1))
    drow = dec_scr[pl.ds(n, 1), :]
    for h in range(heads):
        ks = slice(h * kdim, (h + 1) * kdim)
        vs = slice(h * vdim, (h + 1) * vdim)
        v_b = v_ref[rows, vs]
        s = s_ref[0, h]
        att = lax.dot_general(qe_scr[rows, ks], ke_scr[rows, ks], NT, preferred_element_type=F32)
        att = jnp.where(tril, att, 0.0).astype(BF16)
        o = _dot(att, v_b) + _dot(qin_scr[rows, ks], s.astype(BF16))
        kv = lax.dot_general(kout_scr[rows, ks], v_b, TN, preferred_element_type=F32)
        dcol = _col_broadcast(drow[:, ks])
        if vdim != LANE:
            dcol = jnp.concatenate([dcol] * (vdim // LANE), axis=1)
        s_ref[0, h] = s * dcol + kv
        o_ref[rows, vs] = _gated_out(o, g_ref[rows, vs].astype(F32), gain)


def _hgrn_prompt_kernel(lb_ref, gain_ref, q_ref, f_ref, i_ref, g_ref, o_ref, s_ref,
                        qe_scr, ke_scr, qin_scr, kout_scr, dec_scr, *, hpb):
    n_chunks = q_ref.shape[0] // CHUNK
    lb = _forget_lower_bound(lb_ref[...])
    gain = gain_ref[...]
    s_ref[...] = jnp.zeros_like(s_ref)

    def prep(n, carry):
        rows = _chunk_rows(n)
        q = jax.nn.silu(q_ref[rows, :].astype(F32)) * (HG_K ** -0.5)
        fr = f_ref[rows, :].astype(F32)
        logf = jnp.log(lb + (1.0 - lb) * jax.nn.sigmoid(fr))
        k = (1.0 - lb) * jax.nn.sigmoid(-fr)
        _store_chunk_factors(n, q, k, logf, qe_scr, ke_scr, qin_scr, kout_scr, dec_scr)
        return carry

    def scan(n, carry):
        _scan_chunk(n, hpb, HG_K, HG_V, qe_scr, ke_scr, qin_scr, kout_scr, dec_scr, i_ref, g_ref,
                    gain, o_ref, s_ref)
        return carry

    lax.fori_loop(0, n_chunks, prep, 0, unroll=PREP_UNROLL)
    lax.fori_loop(0, n_chunks, scan, 0, unroll=SCAN_UNROLL)


def _hgrn_prompt(z, hg_lb, gain, batch, seq, hpb):
    w = hpb * HG_K
    sec = (HG_HEADS * HG_K) // w
    zspec = lambda s: pl.BlockSpec((seq, w), lambda b, g, s=s: (b, s * sec + g))
    return pl.pallas_call(
        functools.partial(_hgrn_prompt_kernel, hpb=hpb),
        grid=(batch, HG_HEADS // hpb),
        in_specs=[
            pl.BlockSpec((hg_lb.shape[0], w), lambda b, g: (0, g)),
            pl.BlockSpec((1, HG_V), lambda b, g: (0, 0)),
            zspec(0), zspec(1), zspec(2), zspec(3),
        ],
        out_specs=[
            pl.BlockSpec((seq, w), lambda b, g: (b, g)),
            pl.BlockSpec((1, hpb, HG_K, HG_V), lambda b, g: (b, g, 0, 0)),
        ],
        out_shape=[jax.ShapeDtypeStruct((z.shape[0], HG_HEADS * HG_V), BF16),
                   jax.ShapeDtypeStruct((batch, HG_HEADS, HG_K, HG_V), F32)],
        scratch_shapes=[pltpu.VMEM((seq, w), BF16)] * 4 + [pltpu.VMEM((seq // CHUNK, w), F32)],
        compiler_params=_params(2),
        name="hgrn_prompt",
    )(hg_lb, gain, z, z, z, z)


def _gla_prompt_kernel(gain_ref, wgk_ref, bgk_ref, glr_ref, q_ref, k_ref, v_ref, g_ref,
                       o_ref, s_ref, qe_scr, ke_scr, qin_scr, kout_scr, dec_scr, *, hpb):
    n_chunks = q_ref.shape[0] // CHUNK
    gain = gain_ref[...]
    wgk = wgk_ref[...].astype(BF16)
    bgk = bgk_ref[...]
    s_ref[...] = jnp.zeros_like(s_ref)

    def prep(n, carry):
        rows = _chunk_rows(n)
        gk = _dot(glr_ref[rows, :].astype(BF16), wgk) + bgk
        logf = jax.nn.log_sigmoid(gk) / GLA_GATE_NORM
        q = q_ref[rows, :].astype(F32) * (GLA_K ** -0.5)
        _store_chunk_factors(n, q, k_ref[rows, :].astype(F32), logf, qe_scr, ke_scr, qin_scr,
                             kout_scr, dec_scr)
        return carry

    def scan(n, carry):
        _scan_chunk(n, hpb, GLA_K, GLA_V, qe_scr, ke_scr, qin_scr, kout_scr, dec_scr, v_ref, g_ref,
                    gain, o_ref, s_ref)
        return carry

    lax.fori_loop(0, n_chunks, prep, 0, unroll=PREP_UNROLL)
    lax.fori_loop(0, n_chunks, scan, 0, unroll=SCAN_UNROLL)


def _gla_prompt(z, glr, wgk, bgk, gain, batch, seq, hpb):
    kw, vw = hpb * GLA_K, hpb * GLA_V
    q0 = (4 * HG_HEADS * HG_K) // kw
    k0 = q0 + (GLA_HEADS * GLA_K) // kw
    v0 = (4 * HG_HEADS * HG_K + 2 * GLA_HEADS * GLA_K) // vw
    g0 = v0 + (GLA_HEADS * GLA_V) // vw
    return pl.pallas_call(
        functools.partial(_gla_prompt_kernel, hpb=hpb),
        grid=(batch, GLA_HEADS // hpb),
        in_specs=[
            pl.BlockSpec((1, GLA_V), lambda b, g: (0, 0)),
            pl.BlockSpec((LANE, kw), lambda b, g: (0, g)),
            pl.BlockSpec((1, kw), lambda b, g: (0, g)),
            pl.BlockSpec((seq, LANE), lambda b, g: (b, 0)),
            pl.BlockSpec((seq, kw), lambda b, g: (b, q0 + g)),
            pl.BlockSpec((seq, kw), lambda b, g: (b, k0 + g)),
            pl.BlockSpec((seq, vw), lambda b, g: (b, v0 + g)),
            pl.BlockSpec((seq, vw), lambda b, g: (b, g0 + g)),
        ],
        out_specs=[
            pl.BlockSpec((seq, vw), lambda b, g: (b, g)),
            pl.BlockSpec((1, hpb, GLA_K, GLA_V), lambda b, g: (b, g, 0, 0)),
        ],
        out_shape=[jax.ShapeDtypeStruct((z.shape[0], GLA_HEADS * GLA_V), BF16),
                   jax.ShapeDtypeStruct((batch, GLA_HEADS, GLA_K, GLA_V), F32)],
        scratch_shapes=[pltpu.VMEM((seq, kw), BF16)] * 4 + [pltpu.VMEM((seq // CHUNK, kw), F32)],
        compiler_params=_params(2),
        name="gla_prompt",
    )(gain, wgk, bgk, glr, z, z, z, z)


def _seg_pick(x, seg, idx):
    n = x.shape[0]
    rowmod = lax.broadcasted_iota(jnp.int32, x.shape, 0) % seg
    out = x
    for m in range(seg):
        if m != idx:
            out = jnp.where(rowmod == m, pltpu.roll(x, (m - idx) % n, axis=0), out)
    return out


def _sample_branch(q, k, v_b, logf, g, gain, s_ref, ns_ref, o_ref, o_scr, *, heads, kdim, vdim,
                   seq, pairs):
    tile = 2 * seq
    cum = _cumsum_rows(logf, seq)
    ref = _seg_pick(cum, seq, seq // 2)
    last = _seg_pick(cum, seq, seq - 1)
    qe = (q * jnp.exp(cum - ref)).astype(BF16)
    ke = (k * jnp.exp(ref - cum)).astype(BF16)
    q_in = (q * jnp.exp(cum)).astype(BF16)
    k_out = k * jnp.exp(last - cum)
    dec = jnp.exp(last)
    r_i = lax.broadcasted_iota(jnp.int32, (tile, tile), 0)
    c_i = lax.broadcasted_iota(jnp.int32, (tile, tile), 1)
    amask = (r_i // seq == c_i // seq) & (c_i <= r_i)
    half = lax.broadcasted_iota(jnp.int32, (tile, 1), 0) // seq
    for p in range(pairs):
        rs = slice(p * tile, (p + 1) * tile)
        for h in range(heads):
            ks = slice(h * kdim, (h + 1) * kdim)
            vs = slice(h * vdim, (h + 1) * vdim)
            v_t = v_b[rs, vs]
            att = lax.dot_general(qe[rs, ks], ke[rs, ks], NT, preferred_element_type=F32)
            o = _dot(jnp.where(amask, att, 0.0).astype(BF16), v_t)
            for bi in range(2):
                b = 2 * p + bi
                s0 = s_ref[0, b, h]
                o = o + jnp.where(half == bi, _dot(q_in[rs, ks], s0.astype(BF16)), 0.0)
                km = jnp.where(half == bi, k_out[rs, ks], 0.0).astype(BF16)
                kv = lax.dot_general(km, v_t, TN, preferred_element_type=F32)
                r0 = p * tile + bi * seq
                dcol = _col_broadcast(dec[r0:r0 + 1, ks])
                if vdim != LANE:
                    dcol = jnp.concatenate([dcol] * (vdim // LANE), axis=1)
                ns_ref[0, b, h] = s0 * dcol + kv
            o_scr[rs, vs] = o
    for h in range(heads):
        vs = slice(h * vdim, (h + 1) * vdim)
        o_ref[:, vs] = _gated_out(o_scr[:, vs], g[:, vs].astype(F32), gain)


def _sample_kernel(lb_ref, hgain_ref, ggain_ref, wgk_ref, bgk_ref,
                   hq_ref, hf_ref, hi_ref, hg_ref, gq_ref, gk_ref, gv_ref, gg_ref, glr_ref,
                   sh_ref, sg_ref, a_alias, b_alias,
                   a_ref, b_ref, nsh_ref, nsg_ref, oh_scr, og_scr, *, seq, pairs):
    del a_alias, b_alias
    lb = _forget_lower_bound(lb_ref[...])
    fr = hf_ref[...].astype(F32)
    _sample_branch(
        jax.nn.silu(hq_ref[...].astype(F32)) * (HG_K ** -0.5),
        (1.0 - lb) * jax.nn.sigmoid(-fr),
        hi_ref[...],
        jnp.log(lb + (1.0 - lb) * jax.nn.sigmoid(fr)),
        hg_ref[...], hgain_ref[...], sh_ref, nsh_ref, a_ref, oh_scr,
        heads=HG_HEADS, kdim=HG_K, vdim=HG_V, seq=seq, pairs=pairs)
    gk = _dot(glr_ref[...].astype(BF16), wgk_ref[...].astype(BF16)) + bgk_ref[...]
    _sample_branch(
        gq_ref[...].astype(F32) * (GLA_K ** -0.5),
        gk_ref[...].astype(F32),
        gv_ref[...],
        jax.nn.log_sigmoid(gk) / GLA_GATE_NORM,
        gg_ref[...], ggain_ref[...], sg_ref, nsg_ref, b_ref, og_scr,
        heads=GLA_HEADS, kdim=GLA_K, vdim=GLA_V, seq=seq, pairs=pairs)


def _sample_scan(z, glr, a, b, state_hgrn, state_gla, hg_lb, hgain, ggain, wgk, bgk,
                 row0, nb, seq, bb):
    rows = bb * seq
    rb0 = row0 // rows
    hw, kw, vw = HG_HEADS * HG_K, GLA_HEADS * GLA_K, GLA_HEADS * GLA_V
    gq0 = (4 * hw) // kw
    gv0 = (4 * hw + 2 * kw) // vw
    zs = lambda w, c: pl.BlockSpec((rows, w), lambda i, c=c: (rb0 + i, c))
    const = lambda shape: pl.BlockSpec(shape, lambda i: (0,) * len(shape))
    st = lambda h, k, v: pl.BlockSpec((1, bb, h, k, v), lambda i: (0, i, 0, 0, 0))
    any_spec = pl.BlockSpec(memory_space=pl.ANY)
    return pl.pallas_call(
        functools.partial(_sample_kernel, seq=seq, pairs=bb // 2),
        grid=(nb // bb,),
        in_specs=[
            const(hg_lb.shape), const((1, HG_V)), const((1, GLA_V)), const((LANE, kw)),
            const((1, kw)),
            zs(hw, 0), zs(hw, 1), zs(hw, 2), zs(hw, 3),
            zs(kw, gq0), zs(kw, gq0 + 1), zs(vw, gv0), zs(vw, gv0 + 1),
            pl.BlockSpec((rows, LANE), lambda i: (rb0 + i, 0)),
            st(HG_HEADS, HG_K, HG_V), st(GLA_HEADS, GLA_K, GLA_V),
            any_spec, any_spec,
        ],
        out_specs=[
            pl.BlockSpec((rows, hw), lambda i: (rb0 + i, 0)),
            pl.BlockSpec((rows, vw), lambda i: (rb0 + i, 0)),
            st(HG_HEADS, HG_K, HG_V), st(GLA_HEADS, GLA_K, GLA_V),
        ],
        out_shape=[jax.ShapeDtypeStruct(a.shape, a.dtype), jax.ShapeDtypeStruct(b.shape, b.dtype),
                   jax.ShapeDtypeStruct(state_hgrn.shape, state_hgrn.dtype),
                   jax.ShapeDtypeStruct(state_gla.shape, state_gla.dtype)],
        scratch_shapes=[pltpu.VMEM((rows, hw), F32), pltpu.VMEM((rows, vw), F32)],
        input_output_aliases={16: 0, 17: 1},
        compiler_params=_params(1),
        name="sample_scan",
    )(hg_lb, hgain, ggain, wgk, bgk, z, z, z, z, z, z, z, z, glr, state_hgrn, state_gla, a, b)


def _merge_kernel(h_ref, a_ref, b_ref, wg0_ref, wx0_ref, wg1_ref, wx1_ref, wb0_ref, wb1_ref, o_ref,
                  wg_scr, wb_scr):
    tn = o_ref.shape[1]

    @pl.when(pl.program_id(1) == 0)
    def _():
        for n, (wg_ref, wx_ref) in enumerate(((wg0_ref, wx0_ref), (wg1_ref, wx1_ref))):
            cat = jnp.concatenate([wg_ref[...], wx_ref[...]], axis=0)
            _stage_transposed(cat[GLA_GATE_RANK:GLA_GATE_RANK + tn, :], wg_scr, n)
        wb_scr[0] = wb0_ref[0].astype(BF16)
        wb_scr[1] = wb1_ref[0].astype(BF16)

    h = h_ref[...]
    up0 = _dot(a_ref[...], wb_scr[0])
    m = jax.nn.sigmoid(_dot(h, wg_scr[0])) * up0
    up1 = _dot(b_ref[...], wb_scr[1])
    m = m + jax.nn.sigmoid(_dot(h, wg_scr[1])) * up1
    o_ref[...] = m.astype(o_ref.dtype)


def _merge(h1, a, b, w_in_t, w_branch, tm, tn):
    m, d = h1.shape
    bw = a.shape[1]
    rk = GLA_GATE_RANK
    main = lambda n: pl.BlockSpec((tn, d), lambda j, i, n=n: ((MAIN_COLS + n * d) // tn + j, 0))
    extra = lambda n: pl.BlockSpec(
        (rk, d), lambda j, i, n=n: ((MAIN_COLS + n * d) // rk + (j + 1) * (tn // rk), 0))
    return pl.pallas_call(
        _merge_kernel,
        grid=(d // tn, m // tm),
        in_specs=[
            pl.BlockSpec((tm, d), lambda j, i: (i, 0)),
            pl.BlockSpec((tm, bw), lambda j, i: (i, 0)),
            pl.BlockSpec((tm, bw), lambda j, i: (i, 0)),
            main(0), extra(0), main(1), extra(1),
            pl.BlockSpec((1, bw, tn), lambda j, i: (0, 0, j)),
            pl.BlockSpec((1, bw, tn), lambda j, i: (1, 0, j)),
        ],
        out_specs=pl.BlockSpec((tm, tn), lambda j, i: (i, j)),
        out_shape=jax.ShapeDtypeStruct((m, d), BF16),
        scratch_shapes=[pltpu.VMEM((2, d, tn), BF16), pltpu.VMEM((2, bw, tn), BF16)],
        compiler_params=_params(2),
        name="merge",
    )(h1, a, b, w_in_t, w_in_t, w_in_t, w_in_t, w_branch, w_branch)


def _out_proj_kernel(m_ref, xp_ref, xs_ref, w_ref, ln_ref, x1_ref, h2_ref, *, n_prompt_tiles):
    i = pl.program_id(0)
    y = _dot(m_ref[...], w_ref[...])

    def body(x):
        x1 = x + y
        x1_ref[...] = x1
        h2_ref[...] = _rms(x1, ln_ref[...]).astype(BF16)

    @pl.when(i < n_prompt_tiles)
    def _():
        body(xp_ref[...])

    @pl.when(i >= n_prompt_tiles)
    def _():
        body(xs_ref[...])


def _out_proj(merged, xp, xs, w_out, ln2, tm):
    m, d = merged.shape
    npt = xp.shape[0] // tm
    return pl.pallas_call(
        functools.partial(_out_proj_kernel, n_prompt_tiles=npt),
        grid=(m // tm,),
        in_specs=[
            pl.BlockSpec((tm, d), lambda i: (i, 0)),
            pl.BlockSpec((tm, d), lambda i: (jnp.minimum(i, npt - 1), 0)),
            pl.BlockSpec((tm, d), lambda i: (jnp.maximum(i - npt, 0), 0)),
            pl.BlockSpec((d, d), lambda i: (0, 0), pipeline_mode=pl.Buffered(1)),
            pl.BlockSpec((1, d), lambda i: (0, 0)),
        ],
        out_specs=[pl.BlockSpec((tm, d), lambda i: (i, 0)),
                   pl.BlockSpec((tm, d), lambda i: (i, 0))],
        out_shape=[jax.ShapeDtypeStruct((m, d), F32), jax.ShapeDtypeStruct((m, d), BF16)],
        compiler_params=_params(1),
        name="out_proj",
    )(merged, xp, xs, w_out, ln2)


def _ffn_up_kernel(h_ref, wg_ref, wu_ref, o_ref, w_scr):
    @pl.when(pl.program_id(1) == 0)
    def _():
        w_scr[0] = wg_ref[...].astype(BF16)
        w_scr[1] = wu_ref[...].astype(BF16)

    h = h_ref[...]
    gate = _dot(h, w_scr[0])
    o_ref[...] = (jax.nn.silu(gate) * _dot(h, w_scr[1])).astype(o_ref.dtype)


def _ffn_up(h2, w_gu, tm, tf):
    m, d = h2.shape
    dff = w_gu.shape[1] // 2
    nj = dff // tf
    return pl.pallas_call(
        _ffn_up_kernel,
        grid=(nj, m // tm),
        in_specs=[
            pl.BlockSpec((tm, d), lambda j, i: (i, 0)),
            pl.BlockSpec((d, tf), lambda j, i: (0, j)),
            pl.BlockSpec((d, tf), lambda j, i: (0, nj + j)),
        ],
        out_specs=pl.BlockSpec((tm, tf), lambda j, i: (i, j)),
        out_shape=jax.ShapeDtypeStruct((m, dff), BF16),
        scratch_shapes=[pltpu.VMEM((2, d, tf), BF16)],
        compiler_params=_params(2),
        name="ffn_up",
    )(h2, w_gu, w_gu)


def _ffn_down_kernel(a_ref, w_ref, x_ref, o_ref, w_scr):
    @pl.when(pl.program_id(1) == 0)
    def _():
        w_scr[...] = w_ref[...].astype(BF16)

    o_ref[...] = x_ref[...] + _dot(a_ref[...], w_scr[...])


def _ffn_down(act, w_down, x1, tm, tn):
    m, dff = act.shape
    d = w_down.shape[1]
    return pl.pallas_call(
        _ffn_down_kernel,
        grid=(d // tn, m // tm),
        in_specs=[
            pl.BlockSpec((tm, dff), lambda j, i: (i, 0)),
            pl.BlockSpec((dff, tn), lambda j, i: (0, j)),
            pl.BlockSpec((tm, tn), lambda j, i: (i, j)),
        ],
        out_specs=pl.BlockSpec((tm, tn), lambda j, i: (i, j)),
        out_shape=jax.ShapeDtypeStruct((m, d), F32),
        scratch_shapes=[pltpu.VMEM((dff, tn), BF16)],
        compiler_params=_params(2),
        name="ffn_down",
    )(act, w_down, x1)


def _final_kernel(x_ref, pp_ref, ps_ref, wpg_ref, wple_ref, ln3_ref, lnf_ref, yp_ref, ys_ref, *,
                  n_prompt_tiles):
    i = pl.program_id(0)
    x2 = x_ref[...]
    gate = jax.nn.sigmoid(_dot(_rms(x2, ln3_ref[...]).astype(BF16), wpg_ref[...]))

    def body(p, y_ref):
        x3 = x2 + gate * _dot(p.astype(BF16), wple_ref[...])
        y_ref[...] = _rms(x3, lnf_ref[...])

    @pl.when(i < n_prompt_tiles)
    def _():
        body(pp_ref[...], yp_ref)

    @pl.when(i >= n_prompt_tiles)
    def _():
        body(ps_ref[...], ys_ref)


def _final(x2, pp, ps, w_pg, w_ple, ln3, ln_f, tm):
    m, d = x2.shape
    npr, ns = pp.shape[0], ps.shape[0]
    pd = pp.shape[1]
    npt = npr // tm
    pidx = lambda i: (jnp.minimum(i, npt - 1), 0)
    sidx = lambda i: (jnp.maximum(i - npt, 0), 0)
    return pl.pallas_call(
        functools.partial(_final_kernel, n_prompt_tiles=npt),
        grid=(m // tm,),
        in_specs=[
            pl.BlockSpec((tm, d), lambda i: (i, 0)),
            pl.BlockSpec((tm, pd), pidx),
            pl.BlockSpec((tm, pd), sidx),
            pl.BlockSpec((d, d), lambda i: (0, 0), pipeline_mode=pl.Buffered(1)),
            pl.BlockSpec((pd, d), lambda i: (0, 0)),
            pl.BlockSpec((1, d), lambda i: (0, 0)),
            pl.BlockSpec((1, d), lambda i: (0, 0)),
        ],
        out_specs=[pl.BlockSpec((tm, d), pidx), pl.BlockSpec((tm, d), sidx)],
        out_shape=[jax.ShapeDtypeStruct((npr, d), F32), jax.ShapeDtypeStruct((ns, d), F32)],
        compiler_params=_params(1),
        name="final",
    )(x2, pp, ps, w_pg, w_ple, ln3, ln_f)


def kernel(x_prompt, x_sample, state_hgrn, state_gla, p_prompt, p_sample, hg_lb, ln1, w_in, hg_norm,
           gla_w_gk, gla_b_gk, gla_norm, w_branch, w_out, ln2, w_gu, w_down, ln3, w_ple, w_pg, ln_f):
    batch, seq, d = x_prompt.shape
    nb, dseq, _ = x_sample.shape
    depth = w_in.shape[0]
    assert depth == 1, "single-layer step"
    npr, ns = batch * seq, nb * dseq
    xp = x_prompt.reshape(npr, d)
    xs = x_sample.reshape(ns, d)
    row = lambda v: v.reshape(1, -1)

    w_in0 = jnp.swapaxes(w_in[0], 0, 1)
    wgk = jnp.pad(gla_w_gk[0], ((0, LANE - GLA_GATE_RANK), (0, 0)))
    bgk = row(gla_b_gk[0])

    h1, glr = _prep(xp, xs, row(ln1[0]), w_in0, tm=512)
    z = _in_proj(h1, w_in0, tm=1088, tn=1024)

    a, s_hp = _hgrn_prompt(z, hg_lb, row(hg_norm[0]), batch, seq, hpb=4)
    b, s_gp = _gla_prompt(z, glr, wgk, bgk, row(gla_norm[0]), batch, seq, hpb=2)
    a, b, s_hs, s_gs = _sample_scan(z, glr, a, b, state_hgrn, state_gla, hg_lb, row(hg_norm[0]),
                                    row(gla_norm[0]), wgk, bgk, row0=npr, nb=nb, seq=dseq, bb=8)

    merged = _merge(h1, a, b, w_in0, w_branch[0], tm=544, tn=512)
    x1, h2 = _out_proj(merged, xp, xs, w_out[0].astype(BF16), row(ln2[0]), tm=512)
    act = _ffn_up(h2, w_gu[0], tm=1088, tf=512)
    x2 = _ffn_down(act, w_down[0], x1, tm=544, tn=512)
    yp, ys = _final(x2, p_prompt[0].reshape(npr, -1), p_sample[0].reshape(ns, -1),
                    w_pg[0].astype(BF16), w_ple[0].astype(BF16), row(ln3[0]), row(ln_f), tm=512)

    return (yp.reshape(batch, seq, d), ys.reshape(nb, dseq, d),
            s_hp[None], s_gp[None], s_hs, s_gs)
```

```python
import functools

import jax
import jax.numpy as jnp
from jax import lax
from jax.experimental import pallas as pl
from jax.experimental.pallas import tpu as pltpu

F32 = jnp.float32
BF16 = jnp.bfloat16

EPS = 1e-6
LANE = 128
HG_HEADS, HG_K, HG_V = 8, 128, 128
GLA_HEADS, GLA_K, GLA_V = 4, 128, 256
GLA_GATE_RANK = 16
GLA_GATE_NORM = 16.0
CHUNK = 32
MAIN_COLS = 7168
VMEM_LIMIT = 56 * 1024 * 1024

NT = (((1,), (1,)), ((), ()))
TN = (((0,), (0,)), ((), ()))


def _params(n_axes, vmem=VMEM_LIMIT):
    return pltpu.CompilerParams(dimension_semantics=("arbitrary",) * n_axes,
                                vmem_limit_bytes=vmem)


def _dot(a, b):
    return jnp.dot(a, b, preferred_element_type=F32)


def _rms(x, g):
    return x * lax.rsqrt(jnp.mean(x * x, axis=-1, keepdims=True) + EPS) * g


def _prep_kernel(xp_ref, xs_ref, ln_ref, wglr_ref, h_ref, glr_ref, *, n_prompt_tiles):
    i = pl.program_id(0)

    d = wglr_ref.shape[1]
    wglr = jnp.concatenate([wglr_ref[...], jnp.zeros((LANE - GLA_GATE_RANK, d), F32)], axis=0)
    wglr = wglr.astype(BF16)

    def body(x):
        h = _rms(x, ln_ref[...]).astype(BF16)
        h_ref[...] = h
        glr_ref[...] = lax.dot_general(h, wglr, NT, preferred_element_type=F32)

    @pl.when(i < n_prompt_tiles)
    def _():
        body(xp_ref[...])

    @pl.when(i >= n_prompt_tiles)
    def _():
        body(xs_ref[...])


def _prep(xp, xs, ln1, w_in_t, tm):
    npr, d = xp.shape
    ns = xs.shape[0]
    npt, nst = npr // tm, ns // tm
    m = npr + ns
    return pl.pallas_call(
        functools.partial(_prep_kernel, n_prompt_tiles=npt),
        grid=(npt + nst,),
        in_specs=[
            pl.BlockSpec((tm, d), lambda i: (jnp.minimum(i, npt - 1), 0)),
            pl.BlockSpec((tm, d), lambda i: (jnp.maximum(i - npt, 0), 0)),
            pl.BlockSpec((1, d), lambda i: (0, 0)),
            pl.BlockSpec((GLA_GATE_RANK, d), lambda i: (MAIN_COLS // GLA_GATE_RANK, 0)),
        ],
        out_specs=[
            pl.BlockSpec((tm, d), lambda i: (i, 0)),
            pl.BlockSpec((tm, LANE), lambda i: (i, 0)),
        ],
        out_shape=[jax.ShapeDtypeStruct((m, d), BF16),
                   jax.ShapeDtypeStruct((m, LANE), F32)],
        compiler_params=_params(1),
        name="prep",
    )(xp, xs, ln1, w_in_t)


TRANSPOSE_ROWS = 256


def _stage_transposed(w_t, dst_ref, n=None):
    for r in range(0, w_t.shape[0], TRANSPOSE_ROWS):
        blk = w_t[r:r + TRANSPOSE_ROWS, :].T.astype(BF16)
        if n is None:
            dst_ref[:, r:r + TRANSPOSE_ROWS] = blk
        else:
            dst_ref[n, :, r:r + TRANSPOSE_ROWS] = blk


def _in_proj_kernel(h_ref, w_ref, z_ref, wbf_ref):
    @pl.when(pl.program_id(1) == 0)
    def _():
        _stage_transposed(w_ref[...], wbf_ref)

    z_ref[...] = _dot(h_ref[...], wbf_ref[...]).astype(z_ref.dtype)


def _in_proj(h1, w_in_t, tm, tn):
    m, d = h1.shape
    return pl.pallas_call(
        _in_proj_kernel,
        grid=(MAIN_COLS // tn, m // tm),
        in_specs=[
            pl.BlockSpec((tm, d), lambda j, i: (i, 0)),
            pl.BlockSpec((tn, d), lambda j, i: (j, 0)),
        ],
        out_specs=pl.BlockSpec((tm, tn), lambda j, i: (i, j)),
        out_shape=jax.ShapeDtypeStruct((m, MAIN_COLS), BF16),
        scratch_shapes=[pltpu.VMEM((d, tn), BF16)],
        compiler_params=_params(2),
        name="in_proj",
    )(h1, w_in_t)


def _cumsum_rows(x, length):
    row = lax.broadcasted_iota(jnp.int32, x.shape, 0) % length
    s = 1
    while s < length:
        x = x + jnp.where(row >= s, pltpu.roll(x, s, axis=0), 0.0)
        s *= 2
    return x


def _forget_lower_bound(lb_rows):
    mx = jnp.max(lb_rows, axis=0, keepdims=True)
    e = jnp.exp(lb_rows - mx)
    return e[0:1, :] / jnp.sum(e, axis=0, keepdims=True)


def _col_broadcast(row):
    return jnp.broadcast_to(row, (LANE, LANE)).T


def _gated_out(o, g, gain):
    return (_rms(o, gain) * jax.nn.silu(g)).astype(BF16)


SLAB = 128
CHUNKS_PER_SLAB = SLAB // CHUNK
SUBLANES = 8


def _slab_rows(g):
    return pl.ds(pl.multiple_of(g * SLAB, SLAB), SLAB)


def _per_chunk_rows(rows_1w):
    w = rows_1w[0].shape[1]
    return jnp.concatenate([jnp.broadcast_to(r, (CHUNK, w)) for r in rows_1w], axis=0)


def _intra_slab(g, slot, q, k, logf, heads, kdim, vdim, v_ref, qin_scr, kout_scr, dec_scr, o_scr):
    cum = _cumsum_rows(logf, CHUNK)
    ref_rows = [cum[c * CHUNK + CHUNK // 2:c * CHUNK + CHUNK // 2 + 1, :]
                for c in range(CHUNKS_PER_SLAB)]
    last_rows = [cum[(c + 1) * CHUNK - 1:(c + 1) * CHUNK, :] for c in range(CHUNKS_PER_SLAB)]
    ref = _per_chunk_rows(ref_rows)
    qe = q * jnp.exp(cum - ref)
    ke = k * jnp.exp(ref - cum)
    qin_scr[slot] = (qe * _per_chunk_rows([jnp.exp(r) for r in ref_rows])).astype(BF16)
    kout_scr[slot] = (
        ke * _per_chunk_rows([jnp.exp(l - r) for l, r in zip(last_rows, ref_rows)])).astype(BF16)
    for c in range(CHUNKS_PER_SLAB):
        dec_scr[slot, c:c + 1, :] = jnp.exp(last_rows[c])
    qe = qe.astype(BF16)
    ke = ke.astype(BF16)
    r_i = lax.broadcasted_iota(jnp.int32, (SLAB, SLAB), 0)
    c_i = lax.broadcasted_iota(jnp.int32, (SLAB, SLAB), 1)
    causal = (r_i // CHUNK == c_i // CHUNK) & (c_i <= r_i)
    rows = _slab_rows(g)
    for h in range(heads):
        ks = slice(h * kdim, (h + 1) * kdim)
        vs = slice(h * vdim, (h + 1) * vdim)
        att = lax.dot_general(qe[:, ks], ke[:, ks], NT, preferred_element_type=F32)
        o_scr[slot, :, vs] = _dot(jnp.where(causal, att, 0.0).astype(BF16), v_ref[rows, vs])


def _state_slab(g, slot, heads, kdim, vdim, v_ref, g_ref, gain, qin_scr, kout_scr, dec_scr, o_scr,
                st_scr, o_ref):
    for c in range(CHUNKS_PER_SLAB):
        cr = slice(c * CHUNK, (c + 1) * CHUNK)
        rows = pl.ds(pl.multiple_of(g * SLAB + c * CHUNK, CHUNK), CHUNK)
        drow = dec_scr[slot, c:c + 1, :]
        for h in range(heads):
            ks = slice(h * kdim, (h + 1) * kdim)
            vs = slice(h * vdim, (h + 1) * vdim)
            st = st_scr[h]
            o = o_scr[slot, cr, vs] + lax.dot_general(
                qin_scr[slot, cr, ks], st.astype(BF16), NT, preferred_element_type=F32)
            kv_t = lax.dot_general(v_ref[rows, vs], kout_scr[slot, cr, ks], TN,
                                   preferred_element_type=F32)
            st_scr[h] = st * drow[:, ks] + kv_t
            o_ref[rows, vs] = _gated_out(o, g_ref[rows, vs].astype(F32), gain)


def _run_pipelined(n_slabs, intra, state, st_scr, s_ref):
    assert n_slabs % 2 == 0 and n_slabs >= 4
    st_scr[...] = jnp.zeros_like(st_scr)
    intra(0, 0)

    def pair(i, carry):
        intra(2 * i + 1, 1)
        state(2 * i, 0)
        intra(2 * i + 2, 0)
        state(2 * i + 1, 1)
        return carry

    lax.fori_loop(0, n_slabs // 2 - 1, pair, 0)
    intra(n_slabs - 1, 1)
    state(n_slabs - 2, 0)
    state(n_slabs - 1, 1)
    for h in range(st_scr.shape[0]):
        s_ref[0, h] = st_scr[h].T


def _hgrn_prompt_kernel(lb_ref, gain_ref, q_ref, f_ref, i_ref, g_ref, o_ref, s_ref,
                        qin_scr, kout_scr, dec_scr, o_scr, st_scr, *, hpb):
    lb = _forget_lower_bound(lb_ref[...])
    gain = gain_ref[...]

    def intra(g, slot):
        rows = _slab_rows(g)
        q = jax.nn.silu(q_ref[rows, :].astype(F32)) * (HG_K ** -0.5)
        sig = jax.nn.sigmoid(f_ref[rows, :].astype(F32))
        logf = jnp.log(lb + (1.0 - lb) * sig)
        k = (1.0 - lb) * (1.0 - sig)
        _intra_slab(g, slot, q, k, logf, hpb, HG_K, HG_V, i_ref, qin_scr, kout_scr, dec_scr, o_scr)

    def state(g, slot):
        _state_slab(g, slot, hpb, HG_K, HG_V, i_ref, g_ref, gain, qin_scr, kout_scr, dec_scr,
                    o_scr, st_scr, o_ref)

    _run_pipelined(q_ref.shape[0] // SLAB, intra, state, st_scr, s_ref)


def _hgrn_prompt(z, hg_lb, gain, batch, seq, hpb):
    w = hpb * HG_K
    sec = (HG_HEADS * HG_K) // w
    zspec = lambda s: pl.BlockSpec((seq, w), lambda b, g, s=s: (b, s * sec + g))
    return pl.pallas_call(
        functools.partial(_hgrn_prompt_kernel, hpb=hpb),
        grid=(batch, HG_HEADS // hpb),
        in_specs=[
            pl.BlockSpec((hg_lb.shape[0], w), lambda b, g: (0, g)),
            pl.BlockSpec((1, HG_V), lambda b, g: (0, 0)),
            zspec(0), zspec(1), zspec(2), zspec(3),
        ],
        out_specs=[
            pl.BlockSpec((seq, w), lambda b, g: (b, g)),
            pl.BlockSpec((1, hpb, HG_K, HG_V), lambda b, g: (b, g, 0, 0)),
        ],
        out_shape=[jax.ShapeDtypeStruct((batch * seq, HG_HEADS * HG_V), BF16),
                   jax.ShapeDtypeStruct((batch, HG_HEADS, HG_K, HG_V), F32)],
        scratch_shapes=[pltpu.VMEM((2, SLAB, w), BF16), pltpu.VMEM((2, SLAB, w), BF16),
                        pltpu.VMEM((2, SUBLANES, w), F32), pltpu.VMEM((2, SLAB, w), F32),
                        pltpu.VMEM((hpb, HG_V, HG_K), F32)],
        compiler_params=_params(2),
        name="hgrn_prompt",
    )(hg_lb, gain, z, z, z, z)


def _gla_prompt_kernel(gain_ref, wgk_ref, bgk_ref, glr_ref, q_ref, k_ref, v_ref, g_ref,
                       o_ref, s_ref, qin_scr, kout_scr, dec_scr, o_scr, st_scr, *, hpb):
    gain = gain_ref[...]
    wgk = wgk_ref[...].astype(BF16)
    bgk = bgk_ref[...]

    def intra(g, slot):
        rows = _slab_rows(g)
        gk = _dot(glr_ref[rows, :].astype(BF16), wgk) + bgk
        logf = jax.nn.log_sigmoid(gk) / GLA_GATE_NORM
        q = q_ref[rows, :].astype(F32) * (GLA_K ** -0.5)
        _intra_slab(g, slot, q, k_ref[rows, :].astype(F32), logf, hpb, GLA_K, GLA_V, v_ref,
                    qin_scr, kout_scr, dec_scr, o_scr)

    def state(g, slot):
        _state_slab(g, slot, hpb, GLA_K, GLA_V, v_ref, g_ref, gain, qin_scr, kout_scr, dec_scr,
                    o_scr, st_scr, o_ref)

    _run_pipelined(q_ref.shape[0] // SLAB, intra, state, st_scr, s_ref)


def _gla_prompt(z, glr, wgk, bgk, gain, batch, seq, hpb):
    kw, vw = hpb * GLA_K, hpb * GLA_V
    q0 = (4 * HG_HEADS * HG_K) // kw
    k0 = q0 + (GLA_HEADS * GLA_K) // kw
    v0 = (4 * HG_HEADS * HG_K + 2 * GLA_HEADS * GLA_K) // vw
    g0 = v0 + (GLA_HEADS * GLA_V) // vw
    return pl.pallas_call(
        functools.partial(_gla_prompt_kernel, hpb=hpb),
        grid=(batch, GLA_HEADS // hpb),
        in_specs=[
            pl.BlockSpec((1, GLA_V), lambda b, g: (0, 0)),
            pl.BlockSpec((LANE, kw), lambda b, g: (0, g)),
            pl.BlockSpec((1, kw), lambda b, g: (0, g)),
            pl.BlockSpec((seq, LANE), lambda b, g: (b, 0)),
            pl.BlockSpec((seq, kw), lambda b, g: (b, q0 + g)),
            pl.BlockSpec((seq, kw), lambda b, g: (b, k0 + g)),
            pl.BlockSpec((seq, vw), lambda b, g: (b, v0 + g)),
            pl.BlockSpec((seq, vw), lambda b, g: (b, g0 + g)),
        ],
        out_specs=[
            pl.BlockSpec((seq, vw), lambda b, g: (b, g)),
            pl.BlockSpec((1, hpb, GLA_K, GLA_V), lambda b, g: (b, g, 0, 0)),
        ],
        out_shape=[jax.ShapeDtypeStruct((batch * seq, GLA_HEADS * GLA_V), BF16),
                   jax.ShapeDtypeStruct((batch, GLA_HEADS, GLA_K, GLA_V), F32)],
        scratch_shapes=[pltpu.VMEM((2, SLAB, kw), BF16), pltpu.VMEM((2, SLAB, kw), BF16),
                        pltpu.VMEM((2, SUBLANES, kw), F32), pltpu.VMEM((2, SLAB, vw), F32),
                        pltpu.VMEM((hpb, GLA_V, GLA_K), F32)],
        compiler_params=_params(2),
        name="gla_prompt",
    )(gain, wgk, bgk, glr, z, z, z, z)


def _seg_pick(x, seg, idx):
    n = x.shape[0]
    rowmod = lax.broadcasted_iota(jnp.int32, x.shape, 0) % seg
    out = x
    for m in range(seg):
        if m != idx:
            out = jnp.where(rowmod == m, pltpu.roll(x, (m - idx) % n, axis=0), out)
    return out


def _sample_branch(q, k, v_b, logf, g, gain, s_ref, ns_ref, o_ref, o_scr, *, heads, kdim, vdim,
                   seq, pairs):
    tile = 2 * seq
    cum = _cumsum_rows(logf, seq)
    ref = _seg_pick(cum, seq, seq // 2)
    last = _seg_pick(cum, seq, seq - 1)
    qe = (q * jnp.exp(cum - ref)).astype(BF16)
    ke = (k * jnp.exp(ref - cum)).astype(BF16)
    q_in = (q * jnp.exp(cum)).astype(BF16)
    k_out = k * jnp.exp(last - cum)
    dec = jnp.exp(last)
    r_i = lax.broadcasted_iota(jnp.int32, (tile, tile), 0)
    c_i = lax.broadcasted_iota(jnp.int32, (tile, tile), 1)
    amask = (r_i // seq == c_i // seq) & (c_i <= r_i)
    half = lax.broadcasted_iota(jnp.int32, (tile, 1), 0) // seq
    for p in range(pairs):
        rs = slice(p * tile, (p + 1) * tile)
        for h in range(heads):
            ks = slice(h * kdim, (h + 1) * kdim)
            vs = slice(h * vdim, (h + 1) * vdim)
            v_t = v_b[rs, vs]
            att = lax.dot_general(qe[rs, ks], ke[rs, ks], NT, preferred_element_type=F32)
            o = _dot(jnp.where(amask, att, 0.0).astype(BF16), v_t)
            for bi in range(2):
                b = 2 * p + bi
                s0 = s_ref[0, b, h]
                o = o + jnp.where(half == bi, _dot(q_in[rs, ks], s0.astype(BF16)), 0.0)
                km = jnp.where(half == bi, k_out[rs, ks], 0.0).astype(BF16)
                kv = lax.dot_general(km, v_t, TN, preferred_element_type=F32)
                r0 = p * tile + bi * seq
                dcol = _col_broadcast(dec[r0:r0 + 1, ks])
                if vdim != LANE:
                    dcol = jnp.concatenate([dcol] * (vdim // LANE), axis=1)
                ns_ref[0, b, h] = s0 * dcol + kv
            o_scr[rs, vs] = o
    for h in range(heads):
        vs = slice(h * vdim, (h + 1) * vdim)
        o_ref[:, vs] = _gated_out(o_scr[:, vs], g[:, vs].astype(F32), gain)


def _sample_kernel(lb_ref, hgain_ref, ggain_ref, wgk_ref, bgk_ref,
                   hq_ref, hf_ref, hi_ref, hg_ref, gq_ref, gk_ref, gv_ref, gg_ref, glr_ref,
                   sh_ref, sg_ref,
                   a_ref, b_ref, nsh_ref, nsg_ref, oh_scr, og_scr, *, seq, pairs):
    lb = _forget_lower_bound(lb_ref[...])
    sig = jax.nn.sigmoid(hf_ref[...].astype(F32))
    _sample_branch(
        jax.nn.silu(hq_ref[...].astype(F32)) * (HG_K ** -0.5),
        (1.0 - lb) * (1.0 - sig),
        hi_ref[...],
        jnp.log(lb + (1.0 - lb) * sig),
        hg_ref[...], hgain_ref[...], sh_ref, nsh_ref, a_ref, oh_scr,
        heads=HG_HEADS, kdim=HG_K, vdim=HG_V, seq=seq, pairs=pairs)
    gk = _dot(glr_ref[...].astype(BF16), wgk_ref[...].astype(BF16)) + bgk_ref[...]
    _sample_branch(
        gq_ref[...].astype(F32) * (GLA_K ** -0.5),
        gk_ref[...].astype(F32),
        gv_ref[...],
        jax.nn.log_sigmoid(gk) / GLA_GATE_NORM,
        gg_ref[...], ggain_ref[...], sg_ref, nsg_ref, b_ref, og_scr,
        heads=GLA_HEADS, kdim=GLA_K, vdim=GLA_V, seq=seq, pairs=pairs)


def _sample_scan(z, glr, state_hgrn, state_gla, hg_lb, hgain, ggain, wgk, bgk, row0, nb, seq, bb):
    rows = bb * seq
    rb0 = row0 // rows
    hw, kw, vw = HG_HEADS * HG_K, GLA_HEADS * GLA_K, GLA_HEADS * GLA_V
    gq0 = (4 * hw) // kw
    gv0 = (4 * hw + 2 * kw) // vw
    zs = lambda w, c: pl.BlockSpec((rows, w), lambda i, c=c: (rb0 + i, c))
    const = lambda shape: pl.BlockSpec(shape, lambda i: (0,) * len(shape))
    st = lambda h, k, v: pl.BlockSpec((1, bb, h, k, v), lambda i: (0, i, 0, 0, 0))
    return pl.pallas_call(
        functools.partial(_sample_kernel, seq=seq, pairs=bb // 2),
        grid=(nb // bb,),
        in_specs=[
            const(hg_lb.shape), const((1, HG_V)), const((1, GLA_V)), const((LANE, kw)),
            const((1, kw)),
            zs(hw, 0), zs(hw, 1), zs(hw, 2), zs(hw, 3),
            zs(kw, gq0), zs(kw, gq0 + 1), zs(vw, gv0), zs(vw, gv0 + 1),
            pl.BlockSpec((rows, LANE), lambda i: (rb0 + i, 0)),
            st(HG_HEADS, HG_K, HG_V), st(GLA_HEADS, GLA_K, GLA_V),
        ],
        out_specs=[
            pl.BlockSpec((rows, hw), lambda i: (i, 0)),
            pl.BlockSpec((rows, vw), lambda i: (i, 0)),
            st(HG_HEADS, HG_K, HG_V), st(GLA_HEADS, GLA_K, GLA_V),
        ],
        out_shape=[jax.ShapeDtypeStruct((nb * seq, hw), BF16),
                   jax.ShapeDtypeStruct((nb * seq, vw), BF16),
                   jax.ShapeDtypeStruct(state_hgrn.shape, state_hgrn.dtype),
                   jax.ShapeDtypeStruct(state_gla.shape, state_gla.dtype)],
        scratch_shapes=[pltpu.VMEM((rows, hw), F32), pltpu.VMEM((rows, vw), F32)],
        compiler_params=_params(1),
        name="sample_scan",
    )(hg_lb, hgain, ggain, wgk, bgk, z, z, z, z, z, z, z, z, glr, state_hgrn, state_gla)


def _merge_kernel(h_ref, ap_ref, as_ref, bp_ref, bs_ref, wg0_ref, wx0_ref, wg1_ref, wx1_ref,
                  wb0_ref, wb1_ref, o_ref, wg_scr, wb_scr, *, n_prompt_tiles):
    tn = o_ref.shape[1]
    is_prompt = pl.program_id(1) < n_prompt_tiles

    @pl.when(pl.program_id(1) == 0)
    def _():
        for n, (wg_ref, wx_ref) in enumerate(((wg0_ref, wx0_ref), (wg1_ref, wx1_ref))):
            cat = jnp.concatenate([wg_ref[...], wx_ref[...]], axis=0)
            _stage_transposed(cat[GLA_GATE_RANK:GLA_GATE_RANK + tn, :], wg_scr, n)
        wb_scr[0] = wb0_ref[0].astype(BF16)
        wb_scr[1] = wb1_ref[0].astype(BF16)

    h = h_ref[...]
    up0 = _dot(jnp.where(is_prompt, ap_ref[...], as_ref[...]), wb_scr[0])
    m = jax.nn.sigmoid(_dot(h, wg_scr[0])) * up0
    up1 = _dot(jnp.where(is_prompt, bp_ref[...], bs_ref[...]), wb_scr[1])
    m = m + jax.nn.sigmoid(_dot(h, wg_scr[1])) * up1
    o_ref[...] = m.astype(o_ref.dtype)


def _merge(h1, a_p, a_s, b_p, b_s, w_in_t, w_branch, tm, tn):
    m, d = h1.shape
    bw = a_p.shape[1]
    npt = a_p.shape[0] // tm
    rk = GLA_GATE_RANK
    pspec = pl.BlockSpec((tm, bw), lambda j, i: (jnp.minimum(i, npt - 1), 0))
    sspec = pl.BlockSpec((tm, bw), lambda j, i: (jnp.maximum(i - npt, 0), 0))
    main = lambda n: pl.BlockSpec((tn, d), lambda j, i, n=n: ((MAIN_COLS + n * d) // tn + j, 0))
    extra = lambda n: pl.BlockSpec(
        (rk, d), lambda j, i, n=n: ((MAIN_COLS + n * d) // rk + (j + 1) * (tn // rk), 0))
    return pl.pallas_call(
        functools.partial(_merge_kernel, n_prompt_tiles=npt),
        grid=(d // tn, m // tm),
        in_specs=[
            pl.BlockSpec((tm, d), lambda j, i: (i, 0)),
            pspec, sspec, pspec, sspec,
            main(0), extra(0), main(1), extra(1),
            pl.BlockSpec((1, bw, tn), lambda j, i: (0, 0, j)),
            pl.BlockSpec((1, bw, tn), lambda j, i: (1, 0, j)),
        ],
        out_specs=pl.BlockSpec((tm, tn), lambda j, i: (i, j)),
        out_shape=jax.ShapeDtypeStruct((m, d), BF16),
        scratch_shapes=[pltpu.VMEM((2, d, tn), BF16), pltpu.VMEM((2, bw, tn), BF16)],
        compiler_params=_params(2),
        name="merge",
    )(h1, a_p, a_s, b_p, b_s, w_in_t, w_in_t, w_in_t, w_in_t, w_branch, w_branch)


def _out_proj_kernel(m_ref, xp_ref, xs_ref, w_ref, ln_ref, x1_ref, h2_ref, *, n_prompt_tiles):
    i = pl.program_id(0)
    y = _dot(m_ref[...], w_ref[...])

    def body(x):
        x1 = x + y
        x1_ref[...] = x1
        h2_ref[...] = _rms(x1, ln_ref[...]).astype(BF16)

    @pl.when(i < n_prompt_tiles)
    def _():
        body(xp_ref[...])

    @pl.when(i >= n_prompt_tiles)
    def _():
        body(xs_ref[...])


def _out_proj(merged, xp, xs, w_out, ln2, tm):
    m, d = merged.shape
    npt = xp.shape[0] // tm
    return pl.pallas_call(
        functools.partial(_out_proj_kernel, n_prompt_tiles=npt),
        grid=(m // tm,),
        in_specs=[
            pl.BlockSpec((tm, d), lambda i: (i, 0)),
            pl.BlockSpec((tm, d), lambda i: (jnp.minimum(i, npt - 1), 0)),
            pl.BlockSpec((tm, d), lambda i: (jnp.maximum(i - npt, 0), 0)),
            pl.BlockSpec((d, d), lambda i: (0, 0), pipeline_mode=pl.Buffered(1)),
            pl.BlockSpec((1, d), lambda i: (0, 0)),
        ],
        out_specs=[pl.BlockSpec((tm, d), lambda i: (i, 0)),
                   pl.BlockSpec((tm, d), lambda i: (i, 0))],
        out_shape=[jax.ShapeDtypeStruct((m, d), F32), jax.ShapeDtypeStruct((m, d), BF16)],
        compiler_params=_params(1),
        name="out_proj",
    )(merged, xp, xs, w_out, ln2)


def _ffn_up_kernel(h_ref, wg_ref, wu_ref, o_ref, w_scr):
    @pl.when(pl.program_id(1) == 0)
    def _():
        w_scr[0] = wg_ref[...].astype(BF16)
        w_scr[1] = wu_ref[...].astype(BF16)

    h = h_ref[...]
    gate = _dot(h, w_scr[0])
    o_ref[...] = (jax.nn.silu(gate) * _dot(h, w_scr[1])).astype(o_ref.dtype)


def _ffn_up(h2, w_gu, tm, tf):
    m, d = h2.shape
    dff = w_gu.shape[1] // 2
    nj = dff // tf
    return pl.pallas_call(
        _ffn_up_kernel,
        grid=(nj, m // tm),
        in_specs=[
            pl.BlockSpec((tm, d), lambda j, i: (i, 0)),
            pl.BlockSpec((d, tf), lambda j, i: (0, j)),
            pl.BlockSpec((d, tf), lambda j, i: (0, nj + j)),
        ],
        out_specs=pl.BlockSpec((tm, tf), lambda j, i: (i, j)),
        out_shape=jax.ShapeDtypeStruct((m, dff), BF16),
        scratch_shapes=[pltpu.VMEM((2, d, tf), BF16)],
        compiler_params=_params(2),
        name="ffn_up",
    )(h2, w_gu, w_gu)


def _ffn_down_kernel(a_ref, w_ref, x_ref, o_ref, w_scr):
    @pl.when(pl.program_id(1) == 0)
    def _():
        w_scr[...] = w_ref[...].astype(BF16)

    o_ref[...] = x_ref[...] + _dot(a_ref[...], w_scr[...])


def _ffn_down(act, w_down, x1, tm, tn):
    m, dff = act.shape
    d = w_down.shape[1]
    return pl.pallas_call(
        _ffn_down_kernel,
        grid=(d // tn, m // tm),
        in_specs=[
            pl.BlockSpec((tm, dff), lambda j, i: (i, 0)),
            pl.BlockSpec((dff, tn), lambda j, i: (0, j)),
            pl.BlockSpec((tm, tn), lambda j, i: (i, j)),
        ],
        out_specs=pl.BlockSpec((tm, tn), lambda j, i: (i, j)),
        out_shape=jax.ShapeDtypeStruct((m, d), F32),
        scratch_shapes=[pltpu.VMEM((dff, tn), BF16)],
        compiler_params=_params(2),
        name="ffn_down",
    )(act, w_down, x1)


def _final_kernel(x_ref, pp_ref, ps_ref, wpg_ref, wple_ref, ln3_ref, lnf_ref, yp_ref, ys_ref, *,
                  n_prompt_tiles):
    i = pl.program_id(0)
    x2 = x_ref[...]
    gate = jax.nn.sigmoid(_dot(_rms(x2, ln3_ref[...]).astype(BF16), wpg_ref[...]))

    def body(p, y_ref):
        x3 = x2 + gate * _dot(p.astype(BF16), wple_ref[...])
        y_ref[...] = _rms(x3, lnf_ref[...])

    @pl.when(i < n_prompt_tiles)
    def _():
        body(pp_ref[...], yp_ref)

    @pl.when(i >= n_prompt_tiles)
    def _():
        body(ps_ref[...], ys_ref)


def _final(x2, pp, ps, w_pg, w_ple, ln3, ln_f, tm):
    m, d = x2.shape
    npr, ns = pp.shape[0], ps.shape[0]
    pd = pp.shape[1]
    npt = npr // tm
    pidx = lambda i: (jnp.minimum(i, npt - 1), 0)
    sidx = lambda i: (jnp.maximum(i - npt, 0), 0)
    return pl.pallas_call(
        functools.partial(_final_kernel, n_prompt_tiles=npt),
        grid=(m // tm,),
        in_specs=[
            pl.BlockSpec((tm, d), lambda i: (i, 0)),
            pl.BlockSpec((tm, pd), pidx),
            pl.BlockSpec((tm, pd), sidx),
            pl.BlockSpec((d, d), lambda i: (0, 0), pipeline_mode=pl.Buffered(1)),
            pl.BlockSpec((pd, d), lambda i: (0, 0)),
            pl.BlockSpec((1, d), lambda i: (0, 0)),
            pl.BlockSpec((1, d), lambda i: (0, 0)),
        ],
        out_specs=[pl.BlockSpec((tm, d), pidx), pl.BlockSpec((tm, d), sidx)],
        out_shape=[jax.ShapeDtypeStruct((npr, d), F32), jax.ShapeDtypeStruct((ns, d), F32)],
        compiler_params=_params(1),
        name="final",
    )(x2, pp, ps, w_pg, w_ple, ln3, ln_f)


def kernel(x_prompt, x_sample, state_hgrn, state_gla, p_prompt, p_sample, hg_lb, ln1, w_in, hg_norm,
           gla_w_gk, gla_b_gk, gla_norm, w_branch, w_out, ln2, w_gu, w_down, ln3, w_ple, w_pg, ln_f):
    batch, seq, d = x_prompt.shape
    nb, dseq, _ = x_sample.shape
    depth = w_in.shape[0]
    assert depth == 1, "single-layer step"
    npr, ns = batch * seq, nb * dseq
    xp = x_prompt.reshape(npr, d)
    xs = x_sample.reshape(ns, d)
    row = lambda v: v.reshape(1, -1)

    w_in0 = jnp.swapaxes(w_in[0], 0, 1)
    wgk = jnp.pad(gla_w_gk[0], ((0, LANE - GLA_GATE_RANK), (0, 0)))
    bgk = row(gla_b_gk[0])

    h1, glr = _prep(xp, xs, row(ln1[0]), w_in0, tm=512)
    z = _in_proj(h1, w_in0, tm=1088, tn=1024)

    a_p, s_hp = _hgrn_prompt(z, hg_lb, row(hg_norm[0]), batch, seq, hpb=4)
    b_p, s_gp = _gla_prompt(z, glr, wgk, bgk, row(gla_norm[0]), batch, seq, hpb=2)
    a_s, b_s, s_hs, s_gs = _sample_scan(z, glr, state_hgrn, state_gla, hg_lb, row(hg_norm[0]),
                                        row(gla_norm[0]), wgk, bgk, row0=npr, nb=nb, seq=dseq, bb=8)

    merged = _merge(h1, a_p, a_s, b_p, b_s, w_in0, w_branch[0], tm=512, tn=512)
    x1, h2 = _out_proj(merged, xp, xs, w_out[0].astype(BF16), row(ln2[0]), tm=512)
    act = _ffn_up(h2, w_gu[0], tm=1088, tf=512)
    x2 = _ffn_down(act, w_down[0], x1, tm=544, tn=512)
    yp, ys = _final(x2, p_prompt[0].reshape(npr, -1), p_sample[0].reshape(ns, -1),
                    w_pg[0].astype(BF16), w_ple[0].astype(BF16), row(ln3[0]), row(ln_f), tm=512)

    return (yp.reshape(batch, seq, d), ys.reshape(nb, dseq, d),
            s_hp[None], s_gp[None], s_hs, s_gs)
```

```python
import functools

import jax
import jax.numpy as jnp
from jax import lax
from jax.experimental import pallas as pl
from jax.experimental.pallas import tpu as pltpu

F32 = jnp.float32
BF16 = jnp.bfloat16

EPS = 1e-6
LANE = 128
HG_HEADS, HG_K, HG_V = 8, 128, 128
GLA_HEADS, GLA_K, GLA_V = 4, 128, 256
GLA_GATE_RANK = 16
GLA_GATE_NORM = 16.0
CHUNK = 32
MAIN_COLS = 7168
VMEM_LIMIT = 56 * 1024 * 1024
EPILOGUE_COLS = 512
MERGE_COLS = 256

NT = (((1,), (1,)), ((), ()))
TN = (((0,), (0,)), ((), ()))


def _params(n_axes, vmem=VMEM_LIMIT):
    return pltpu.CompilerParams(dimension_semantics=("arbitrary",) * n_axes,
                                vmem_limit_bytes=vmem)


def _dot(a, b):
    return jnp.dot(a, b, preferred_element_type=F32)


def _rms(x, g):
    return x * lax.rsqrt(jnp.mean(x * x, axis=-1, keepdims=True) + EPS) * g


def _prep_kernel(xp_ref, xs_ref, ln_ref, wglr_ref, h_ref, glr_ref, *, n_prompt_tiles):
    i = pl.program_id(0)

    d = wglr_ref.shape[1]
    wglr = jnp.concatenate([wglr_ref[...], jnp.zeros((LANE - GLA_GATE_RANK, d), F32)], axis=0)
    wglr = wglr.astype(BF16)

    def body(x):
        h = _rms(x, ln_ref[...]).astype(BF16)
        h_ref[...] = h
        glr_ref[...] = lax.dot_general(h, wglr, NT, preferred_element_type=F32)

    @pl.when(i < n_prompt_tiles)
    def _():
        body(xp_ref[...])

    @pl.when(i >= n_prompt_tiles)
    def _():
        body(xs_ref[...])


def _prep(xp, xs, ln1, w_in_t, tm):
    npr, d = xp.shape
    ns = xs.shape[0]
    npt, nst = npr // tm, ns // tm
    m = npr + ns
    return pl.pallas_call(
        functools.partial(_prep_kernel, n_prompt_tiles=npt),
        grid=(npt + nst,),
        in_specs=[
            pl.BlockSpec((tm, d), lambda i: (jnp.minimum(i, npt - 1), 0)),
            pl.BlockSpec((tm, d), lambda i: (jnp.maximum(i - npt, 0), 0)),
            pl.BlockSpec((1, d), lambda i: (0, 0)),
            pl.BlockSpec((GLA_GATE_RANK, d), lambda i: (MAIN_COLS // GLA_GATE_RANK, 0)),
        ],
        out_specs=[
            pl.BlockSpec((tm, d), lambda i: (i, 0)),
            pl.BlockSpec((tm, LANE), lambda i: (i, 0)),
        ],
        out_shape=[jax.ShapeDtypeStruct((m, d), BF16),
                   jax.ShapeDtypeStruct((m, LANE), F32)],
        compiler_params=_params(1),
        name="prep",
    )(xp, xs, ln1, w_in_t)


TRANSPOSE_ROWS = 256


def _stage_transposed(w_t, dst_ref, n=None):
    for r in range(0, w_t.shape[0], TRANSPOSE_ROWS):
        blk = w_t[r:r + TRANSPOSE_ROWS, :].T.astype(BF16)
        if n is None:
            dst_ref[:, r:r + TRANSPOSE_ROWS] = blk
        else:
            dst_ref[n, :, r:r + TRANSPOSE_ROWS] = blk


def _in_proj_kernel(h_ref, w_ref, z_ref, wbf_ref):
    @pl.when(pl.program_id(1) == 0)
    def _():
        _stage_transposed(w_ref[...], wbf_ref)

    h = h_ref[...]
    for c0 in range(0, z_ref.shape[1], MERGE_COLS):
        cs = slice(c0, c0 + MERGE_COLS)
        z_ref[:, cs] = _dot(h, wbf_ref[:, cs]).astype(z_ref.dtype)


def _in_proj(h1, w_in_t, tm, tn):
    m, d = h1.shape
    return pl.pallas_call(
        _in_proj_kernel,
        grid=(MAIN_COLS // tn, m // tm),
        in_specs=[
            pl.BlockSpec((tm, d), lambda j, i: (i, 0)),
            pl.BlockSpec((tn, d), lambda j, i: (j, 0)),
        ],
        out_specs=pl.BlockSpec((tm, tn), lambda j, i: (i, j)),
        out_shape=jax.ShapeDtypeStruct((m, MAIN_COLS), BF16),
        scratch_shapes=[pltpu.VMEM((d, tn), BF16)],
        compiler_params=_params(2),
        name="in_proj",
    )(h1, w_in_t)


def _cumsum_rows(x, length):
    row = lax.broadcasted_iota(jnp.int32, x.shape, 0) % length
    s = 1
    while s < length:
        x = x + jnp.where(row >= s, pltpu.roll(x, s, axis=0), 0.0)
        s *= 2
    return x


def _forget_lower_bound(lb_rows):
    mx = jnp.max(lb_rows, axis=0, keepdims=True)
    e = jnp.exp(lb_rows - mx)
    return e[0:1, :] / jnp.sum(e, axis=0, keepdims=True)


def _col_broadcast(row):
    return jnp.broadcast_to(row, (LANE, LANE)).T


def _gated_out(o, g, gain):
    return (_rms(o, gain) * jax.nn.silu(g)).astype(BF16)


SLAB = 128
CHUNKS_PER_SLAB = SLAB // CHUNK
SUBLANES = 8
INTRA_PARTS = 1


def _slab_rows(g):
    return pl.ds(pl.multiple_of(g * SLAB, SLAB), SLAB)


def _per_chunk_rows(rows_1w):
    w = rows_1w[0].shape[1]
    return jnp.concatenate([jnp.broadcast_to(r, (CHUNK, w)) for r in rows_1w], axis=0)


def _intra_slab(slot, h0, q, k, logf, kdim, vdim, v_slab, qin_scr, kout_scr, dec_scr, o_scr):
    heads = q.shape[1] // kdim
    kl = slice(h0 * kdim, (h0 + heads) * kdim)
    cum = _cumsum_rows(logf, CHUNK)
    ref_rows = [cum[c * CHUNK + CHUNK // 2:c * CHUNK + CHUNK // 2 + 1, :]
                for c in range(CHUNKS_PER_SLAB)]
    last_rows = [cum[(c + 1) * CHUNK - 1:(c + 1) * CHUNK, :] for c in range(CHUNKS_PER_SLAB)]
    ref = _per_chunk_rows(ref_rows)
    qe = q * jnp.exp(cum - ref)
    ke = k * jnp.exp(ref - cum)
    qin_scr[slot, :, kl] = (qe * _per_chunk_rows([jnp.exp(r) for r in ref_rows])).astype(BF16)
    kout_scr[slot, :, kl] = (
        ke * _per_chunk_rows([jnp.exp(l - r) for l, r in zip(last_rows, ref_rows)])).astype(BF16)
    for c in range(CHUNKS_PER_SLAB):
        dec_scr[slot, c:c + 1, kl] = jnp.exp(last_rows[c])
    qe = qe.astype(BF16)
    ke = ke.astype(BF16)
    r_i = lax.broadcasted_iota(jnp.int32, (SLAB, SLAB), 0)
    c_i = lax.broadcasted_iota(jnp.int32, (SLAB, SLAB), 1)
    causal = (r_i // CHUNK == c_i // CHUNK) & (c_i <= r_i)
    for h in range(heads):
        ks = slice(h * kdim, (h + 1) * kdim)
        att = lax.dot_general(qe[:, ks], ke[:, ks], NT, preferred_element_type=F32)
        o_scr[slot, :, (h0 + h) * vdim:(h0 + h + 1) * vdim] = _dot(
            jnp.where(causal, att, 0.0).astype(BF16), v_slab[:, h * vdim:(h + 1) * vdim])


def _state_chunk(g, slot, c, heads, kdim, vdim, v_ref, g_ref, gain, qin_scr, kout_scr, dec_scr,
                 o_scr, st_scr, o_ref):
    cr = slice(c * CHUNK, (c + 1) * CHUNK)
    rows = pl.ds(pl.multiple_of(g * SLAB + c * CHUNK, CHUNK), CHUNK)
    drow = dec_scr[slot, c:c + 1, :]
    for h in range(heads):
        ks = slice(h * kdim, (h + 1) * kdim)
        vs = slice(h * vdim, (h + 1) * vdim)
        st = st_scr[h]
        o = o_scr[slot, cr, vs] + _dot(qin_scr[slot, cr, ks], st.astype(BF16))
        kv = lax.dot_general(kout_scr[slot, cr, ks], v_ref[rows, vs], TN,
                             preferred_element_type=F32)
        dcol = _col_broadcast(drow[:, ks])
        if vdim != LANE:
            dcol = jnp.concatenate([dcol] * (vdim // LANE), axis=1)
        st_scr[h] = st * dcol + kv
        o_ref[rows, vs] = _gated_out(o, g_ref[rows, vs].astype(F32), gain)


def _run_pipelined(n_slabs, n_parts, intra, state, st_scr, s_ref):
    assert n_slabs % 2 == 0 and n_slabs >= 4

    def overlapped(g_intra, slot_intra, g_state, slot_state):
        for p in range(max(n_parts, CHUNKS_PER_SLAB)):
            if p < n_parts:
                intra(g_intra, slot_intra, p)
            if p < CHUNKS_PER_SLAB:
                state(g_state, slot_state, p)

    st_scr[...] = jnp.zeros_like(st_scr)
    for p in range(n_parts):
        intra(0, 0, p)

    def pair(i, carry):
        overlapped(2 * i + 1, 1, 2 * i, 0)
        overlapped(2 * i + 2, 0, 2 * i + 1, 1)
        return carry

    lax.fori_loop(0, n_slabs // 2 - 1, pair, 0)
    overlapped(n_slabs - 1, 1, n_slabs - 2, 0)
    for c in range(CHUNKS_PER_SLAB):
        state(n_slabs - 1, 1, c)
    for h in range(st_scr.shape[0]):
        s_ref[0, h] = st_scr[h]


def _hgrn_prompt_kernel(lb_ref, gain_ref, q_ref, f_ref, i_ref, g_ref, o_ref, s_ref,
                        qin_scr, kout_scr, dec_scr, o_scr, st_scr, *, hpb):
    lb = _forget_lower_bound(lb_ref[...])
    gain = gain_ref[...]

    hpp = hpb // INTRA_PARTS

    def intra(g, slot, part):
        rows = _slab_rows(g)
        ls = slice(part * hpp * HG_K, (part + 1) * hpp * HG_K)
        lbp = lb[:, ls]
        q = jax.nn.silu(q_ref[rows, ls].astype(F32)) * (HG_K ** -0.5)
        sig = jax.nn.sigmoid(f_ref[rows, ls].astype(F32))
        logf = jnp.log(lbp + (1.0 - lbp) * sig)
        k = (1.0 - lbp) * (1.0 - sig)
        _intra_slab(slot, part * hpp, q, k, logf, HG_K, HG_V, i_ref[rows, ls], qin_scr, kout_scr,
                    dec_scr, o_scr)

    def state(g, slot, c):
        _state_chunk(g, slot, c, hpb, HG_K, HG_V, i_ref, g_ref, gain, qin_scr, kout_scr, dec_scr,
                     o_scr, st_scr, o_ref)

    _run_pipelined(q_ref.shape[0] // SLAB, hpb // hpp, intra, state, st_scr, s_ref)


def _hgrn_prompt(z, hg_lb, gain, batch, seq, hpb):
    w = hpb * HG_K
    sec = (HG_HEADS * HG_K) // w
    zspec = lambda s: pl.BlockSpec((seq, w), lambda b, g, s=s: (b, s * sec + g))
    return pl.pallas_call(
        functools.partial(_hgrn_prompt_kernel, hpb=hpb),
        grid=(batch, HG_HEADS // hpb),
        in_specs=[
            pl.BlockSpec((hg_lb.shape[0], w), lambda b, g: (0, g)),
            pl.BlockSpec((1, HG_V), lambda b, g: (0, 0)),
            zspec(0), zspec(1), zspec(2), zspec(3),
        ],
        out_specs=[
            pl.BlockSpec((seq, w), lambda b, g: (b, g)),
            pl.BlockSpec((1, hpb, HG_K, HG_V), lambda b, g: (b, g, 0, 0)),
        ],
        out_shape=[jax.ShapeDtypeStruct((batch * seq, HG_HEADS * HG_V), BF16),
                   jax.ShapeDtypeStruct((batch, HG_HEADS, HG_K, HG_V), F32)],
        scratch_shapes=[pltpu.VMEM((2, SLAB, w), BF16), pltpu.VMEM((2, SLAB, w), BF16),
                        pltpu.VMEM((2, SUBLANES, w), F32), pltpu.VMEM((2, SLAB, w), F32),
                        pltpu.VMEM((hpb, HG_K, HG_V), F32)],
        compiler_params=_params(2),
        name="hgrn_prompt",
    )(hg_lb, gain, z, z, z, z)


def _gla_prompt_kernel(gain_ref, wgk_ref, bgk_ref, glr_ref, q_ref, k_ref, v_ref, g_ref,
                       o_ref, s_ref, qin_scr, kout_scr, dec_scr, o_scr, st_scr, *, hpb):
    gain = gain_ref[...]
    wgk = wgk_ref[...].astype(BF16)
    bgk = bgk_ref[...]

    hpp = hpb // INTRA_PARTS

    def intra(g, slot, part):
        rows = _slab_rows(g)
        kl = slice(part * hpp * GLA_K, (part + 1) * hpp * GLA_K)
        vl = slice(part * hpp * GLA_V, (part + 1) * hpp * GLA_V)
        gk = _dot(glr_ref[rows, :].astype(BF16), wgk[:, kl]) + bgk[:, kl]
        logf = jax.nn.log_sigmoid(gk) / GLA_GATE_NORM
        q = q_ref[rows, kl].astype(F32) * (GLA_K ** -0.5)
        _intra_slab(slot, part * hpp, q, k_ref[rows, kl].astype(F32), logf, GLA_K, GLA_V,
                    v_ref[rows, vl], qin_scr, kout_scr, dec_scr, o_scr)

    def state(g, slot, c):
        _state_chunk(g, slot, c, hpb, GLA_K, GLA_V, v_ref, g_ref, gain, qin_scr, kout_scr, dec_scr,
                     o_scr, st_scr, o_ref)

    _run_pipelined(q_ref.shape[0] // SLAB, hpb // hpp, intra, state, st_scr, s_ref)


def _gla_prompt(z, glr, wgk, bgk, gain, batch, seq, hpb):
    kw, vw = hpb * GLA_K, hpb * GLA_V
    q0 = (4 * HG_HEADS * HG_K) // kw
    k0 = q0 + (GLA_HEADS * GLA_K) // kw
    v0 = (4 * HG_HEADS * HG_K + 2 * GLA_HEADS * GLA_K) // vw
    g0 = v0 + (GLA_HEADS * GLA_V) // vw
    return pl.pallas_call(
        functools.partial(_gla_prompt_kernel, hpb=hpb),
        grid=(batch, GLA_HEADS // hpb),
        in_specs=[
            pl.BlockSpec((1, GLA_V), lambda b, g: (0, 0)),
            pl.BlockSpec((LANE, kw), lambda b, g: (0, g)),
            pl.BlockSpec((1, kw), lambda b, g: (0, g)),
            pl.BlockSpec((seq, LANE), lambda b, g: (b, 0)),
            pl.BlockSpec((seq, kw), lambda b, g: (b, q0 + g)),
            pl.BlockSpec((seq, kw), lambda b, g: (b, k0 + g)),
            pl.BlockSpec((seq, vw), lambda b, g: (b, v0 + g)),
            pl.BlockSpec((seq, vw), lambda b, g: (b, g0 + g)),
        ],
        out_specs=[
            pl.BlockSpec((seq, vw), lambda b, g: (b, g)),
            pl.BlockSpec((1, hpb, GLA_K, GLA_V), lambda b, g: (b, g, 0, 0)),
        ],
        out_shape=[jax.ShapeDtypeStruct((batch * seq, GLA_HEADS * GLA_V), BF16),
                   jax.ShapeDtypeStruct((batch, GLA_HEADS, GLA_K, GLA_V), F32)],
        scratch_shapes=[pltpu.VMEM((2, SLAB, kw), BF16), pltpu.VMEM((2, SLAB, kw), BF16),
                        pltpu.VMEM((2, SUBLANES, kw), F32), pltpu.VMEM((2, SLAB, vw), F32),
                        pltpu.VMEM((hpb, GLA_K, GLA_V), F32)],
        compiler_params=_params(2),
        name="gla_prompt",
    )(gain, wgk, bgk, glr, z, z, z, z)


def _seg_pick(x, seg, idx):
    n = x.shape[0]
    rowmod = lax.broadcasted_iota(jnp.int32, x.shape, 0) % seg
    out = x
    for m in range(seg):
        if m != idx:
            out = jnp.where(rowmod == m, pltpu.roll(x, (m - idx) % n, axis=0), out)
    return out


def _sample_branch(q, k, v_b, logf, g, gain, s_ref, ns_ref, o_ref, o_scr, *, heads, kdim, vdim,
                   seq, pairs):
    tile = 2 * seq
    cum = _cumsum_rows(logf, seq)
    ref = _seg_pick(cum, seq, seq // 2)
    last = _seg_pick(cum, seq, seq - 1)
    qe = (q * jnp.exp(cum - ref)).astype(BF16)
    ke = (k * jnp.exp(ref - cum)).astype(BF16)
    q_in = (q * jnp.exp(cum)).astype(BF16)
    k_out = k * jnp.exp(last - cum)
    dec = jnp.exp(last)
    r_i = lax.broadcasted_iota(jnp.int32, (tile, tile), 0)
    c_i = lax.broadcasted_iota(jnp.int32, (tile, tile), 1)
    amask = (r_i // seq == c_i // seq) & (c_i <= r_i)
    half = lax.broadcasted_iota(jnp.int32, (tile, 1), 0) // seq
    for p in range(pairs):
        rs = slice(p * tile, (p + 1) * tile)
        for h in range(heads):
            ks = slice(h * kdim, (h + 1) * kdim)
            vs = slice(h * vdim, (h + 1) * vdim)
            v_t = v_b[rs, vs]
            att = lax.dot_general(qe[rs, ks], ke[rs, ks], NT, preferred_element_type=F32)
            o = _dot(jnp.where(amask, att, 0.0).astype(BF16), v_t)
            for bi in range(2):
                b = 2 * p + bi
                s0 = s_ref[0, b, h]
                o = o + jnp.where(half == bi, _dot(q_in[rs, ks], s0.astype(BF16)), 0.0)
                km = jnp.where(half == bi, k_out[rs, ks], 0.0).astype(BF16)
                kv = lax.dot_general(km, v_t, TN, preferred_element_type=F32)
                r0 = p * tile + bi * seq
                dcol = _col_broadcast(dec[r0:r0 + 1, ks])
                if vdim != LANE:
                    dcol = jnp.concatenate([dcol] * (vdim // LANE), axis=1)
                ns_ref[0, b, h] = s0 * dcol + kv
            o_scr[rs, vs] = o
    for h in range(heads):
        vs = slice(h * vdim, (h + 1) * vdim)
        o_ref[:, vs] = _gated_out(o_scr[:, vs], g[:, vs].astype(F32), gain)


def _sample_kernel(lb_ref, hgain_ref, ggain_ref, wgk_ref, bgk_ref,
                   hq_ref, hf_ref, hi_ref, hg_ref, gq_ref, gk_ref, gv_ref, gg_ref, glr_ref,
                   sh_ref, sg_ref,
                   a_ref, b_ref, nsh_ref, nsg_ref, oh_scr, og_scr, *, seq, pairs):
    lb = _forget_lower_bound(lb_ref[...])
    sig = jax.nn.sigmoid(hf_ref[...].astype(F32))
    _sample_branch(
        jax.nn.silu(hq_ref[...].astype(F32)) * (HG_K ** -0.5),
        (1.0 - lb) * (1.0 - sig),
        hi_ref[...],
        jnp.log(lb + (1.0 - lb) * sig),
        hg_ref[...], hgain_ref[...], sh_ref, nsh_ref, a_ref, oh_scr,
        heads=HG_HEADS, kdim=HG_K, vdim=HG_V, seq=seq, pairs=pairs)
    gk = _dot(glr_ref[...].astype(BF16), wgk_ref[...].astype(BF16)) + bgk_ref[...]
    _sample_branch(
        gq_ref[...].astype(F32) * (GLA_K ** -0.5),
        gk_ref[...].astype(F32),
        gv_ref[...],
        jax.nn.log_sigmoid(gk) / GLA_GATE_NORM,
        gg_ref[...], ggain_ref[...], sg_ref, nsg_ref, b_ref, og_scr,
        heads=GLA_HEADS, kdim=GLA_K, vdim=GLA_V, seq=seq, pairs=pairs)


def _sample_scan(z, glr, state_hgrn, state_gla, hg_lb, hgain, ggain, wgk, bgk, row0, nb, seq, bb):
    rows = bb * seq
    rb0 = row0 // rows
    hw, kw, vw = HG_HEADS * HG_K, GLA_HEADS * GLA_K, GLA_HEADS * GLA_V
    gq0 = (4 * hw) // kw
    gv0 = (4 * hw + 2 * kw) // vw
    zs = lambda w, c: pl.BlockSpec((rows, w), lambda i, c=c: (rb0 + i, c))
    const = lambda shape: pl.BlockSpec(shape, lambda i: (0,) * len(shape))
    st = lambda h, k, v: pl.BlockSpec((1, bb, h, k, v), lambda i: (0, i, 0, 0, 0))
    return pl.pallas_call(
        functools.partial(_sample_kernel, seq=seq, pairs=bb // 2),
        grid=(nb // bb,),
        in_specs=[
            const(hg_lb.shape), const((1, HG_V)), const((1, GLA_V)), const((LANE, kw)),
            const((1, kw)),
            zs(hw, 0), zs(hw, 1), zs(hw, 2), zs(hw, 3),
            zs(kw, gq0), zs(kw, gq0 + 1), zs(vw, gv0), zs(vw, gv0 + 1),
            pl.BlockSpec((rows, LANE), lambda i: (rb0 + i, 0)),
            st(HG_HEADS, HG_K, HG_V), st(GLA_HEADS, GLA_K, GLA_V),
        ],
        out_specs=[
            pl.BlockSpec((rows, hw), lambda i: (i, 0)),
            pl.BlockSpec((rows, vw), lambda i: (i, 0)),
            st(HG_HEADS, HG_K, HG_V), st(GLA_HEADS, GLA_K, GLA_V),
        ],
        out_shape=[jax.ShapeDtypeStruct((nb * seq, hw), BF16),
                   jax.ShapeDtypeStruct((nb * seq, vw), BF16),
                   jax.ShapeDtypeStruct(state_hgrn.shape, state_hgrn.dtype),
                   jax.ShapeDtypeStruct(state_gla.shape, state_gla.dtype)],
        scratch_shapes=[pltpu.VMEM((rows, hw), F32), pltpu.VMEM((rows, vw), F32)],
        compiler_params=_params(1),
        name="sample_scan",
    )(hg_lb, hgain, ggain, wgk, bgk, z, z, z, z, z, z, z, z, glr, state_hgrn, state_gla)


def _merge_kernel(h_ref, ap_ref, as_ref, bp_ref, bs_ref, wg0_ref, wx0_ref, wg1_ref, wx1_ref,
                  wb0_ref, wb1_ref, o_ref, wg_scr, wb_scr, *, n_prompt_tiles):
    tn = o_ref.shape[1]
    is_prompt = pl.program_id(1) < n_prompt_tiles

    @pl.when(pl.program_id(1) == 0)
    def _():
        for n, (wg_ref, wx_ref) in enumerate(((wg0_ref, wx0_ref), (wg1_ref, wx1_ref))):
            cat = jnp.concatenate([wg_ref[...], wx_ref[...]], axis=0)
            _stage_transposed(cat[GLA_GATE_RANK:GLA_GATE_RANK + tn, :], wg_scr, n)
        wb_scr[0] = wb0_ref[0].astype(BF16)
        wb_scr[1] = wb1_ref[0].astype(BF16)

    h = h_ref[...]
    a = jnp.where(is_prompt, ap_ref[...], as_ref[...])
    b = jnp.where(is_prompt, bp_ref[...], bs_ref[...])
    for c0 in range(0, tn, MERGE_COLS):
        cs = slice(c0, c0 + MERGE_COLS)
        m = jax.nn.sigmoid(_dot(h, wg_scr[0, :, cs])) * _dot(a, wb_scr[0, :, cs])
        m = m + jax.nn.sigmoid(_dot(h, wg_scr[1, :, cs])) * _dot(b, wb_scr[1, :, cs])
        o_ref[:, cs] = m.astype(o_ref.dtype)


def _merge(h1, a_p, a_s, b_p, b_s, w_in_t, w_branch, tm, tn):
    m, d = h1.shape
    bw = a_p.shape[1]
    npt = a_p.shape[0] // tm
    rk = GLA_GATE_RANK
    pspec = pl.BlockSpec((tm, bw), lambda j, i: (jnp.minimum(i, npt - 1), 0))
    sspec = pl.BlockSpec((tm, bw), lambda j, i: (jnp.maximum(i - npt, 0), 0))
    main = lambda n: pl.BlockSpec((tn, d), lambda j, i, n=n: ((MAIN_COLS + n * d) // tn + j, 0))
    extra = lambda n: pl.BlockSpec(
        (rk, d), lambda j, i, n=n: ((MAIN_COLS + n * d) // rk + (j + 1) * (tn // rk), 0))
    return pl.pallas_call(
        functools.partial(_merge_kernel, n_prompt_tiles=npt),
        grid=(d // tn, m // tm),
        in_specs=[
            pl.BlockSpec((tm, d), lambda j, i: (i, 0)),
            pspec, sspec, pspec, sspec,
            main(0), extra(0), main(1), extra(1),
            pl.BlockSpec((1, bw, tn), lambda j, i: (0, 0, j)),
            pl.BlockSpec((1, bw, tn), lambda j, i: (1, 0, j)),
        ],
        out_specs=pl.BlockSpec((tm, tn), lambda j, i: (i, j)),
        out_shape=jax.ShapeDtypeStruct((m, d), BF16),
        scratch_shapes=[pltpu.VMEM((2, d, tn), BF16), pltpu.VMEM((2, bw, tn), BF16)],
        compiler_params=_params(2),
        name="merge",
    )(h1, a_p, a_s, b_p, b_s, w_in_t, w_in_t, w_in_t, w_in_t, w_branch, w_branch)


def _out_proj_kernel(m_ref, xp_ref, xs_ref, w_ref, ln_ref, x1_ref, h2_ref, *, n_prompt_tiles):
    is_prompt = pl.program_id(0) < n_prompt_tiles
    m = m_ref[...]
    tm, d = x1_ref.shape
    ss = jnp.zeros((tm, 1), F32)
    for c0 in range(0, d, EPILOGUE_COLS):
        cs = slice(c0, c0 + EPILOGUE_COLS)
        x1 = jnp.where(is_prompt, xp_ref[:, cs], xs_ref[:, cs]) + _dot(m, w_ref[:, cs])
        x1_ref[:, cs] = x1
        ss = ss + jnp.sum(x1 * x1, axis=-1, keepdims=True)
    inv = lax.rsqrt(ss * (1.0 / d) + EPS)
    for c0 in range(0, d, EPILOGUE_COLS):
        cs = slice(c0, c0 + EPILOGUE_COLS)
        h2_ref[:, cs] = (x1_ref[:, cs] * inv * ln_ref[:, cs]).astype(BF16)


def _out_proj(merged, xp, xs, w_out, ln2, tm):
    m, d = merged.shape
    npt = xp.shape[0] // tm
    return pl.pallas_call(
        functools.partial(_out_proj_kernel, n_prompt_tiles=npt),
        grid=(m // tm,),
        in_specs=[
            pl.BlockSpec((tm, d), lambda i: (i, 0)),
            pl.BlockSpec((tm, d), lambda i: (jnp.minimum(i, npt - 1), 0)),
            pl.BlockSpec((tm, d), lambda i: (jnp.maximum(i - npt, 0), 0)),
            pl.BlockSpec((d, d), lambda i: (0, 0), pipeline_mode=pl.Buffered(1)),
            pl.BlockSpec((1, d), lambda i: (0, 0)),
        ],
        out_specs=[pl.BlockSpec((tm, d), lambda i: (i, 0)),
                   pl.BlockSpec((tm, d), lambda i: (i, 0))],
        out_shape=[jax.ShapeDtypeStruct((m, d), F32), jax.ShapeDtypeStruct((m, d), BF16)],
        compiler_params=_params(1),
        name="out_proj",
    )(merged, xp, xs, w_out, ln2)


def _ffn_up_kernel(h_ref, wg_ref, wu_ref, o_ref, w_scr):
    @pl.when(pl.program_id(1) == 0)
    def _():
        w_scr[0] = wg_ref[...].astype(BF16)
        w_scr[1] = wu_ref[...].astype(BF16)

    h = h_ref[...]
    for c0 in range(0, o_ref.shape[1], MERGE_COLS):
        cs = slice(c0, c0 + MERGE_COLS)
        gate = _dot(h, w_scr[0, :, cs])
        o_ref[:, cs] = (jax.nn.silu(gate) * _dot(h, w_scr[1, :, cs])).astype(o_ref.dtype)


def _ffn_up(h2, w_gu, tm, tf):
    m, d = h2.shape
    dff = w_gu.shape[1] // 2
    nj = dff // tf
    return pl.pallas_call(
        _ffn_up_kernel,
        grid=(nj, m // tm),
        in_specs=[
            pl.BlockSpec((tm, d), lambda j, i: (i, 0)),
            pl.BlockSpec((d, tf), lambda j, i: (0, j)),
            pl.BlockSpec((d, tf), lambda j, i: (0, nj + j)),
        ],
        out_specs=pl.BlockSpec((tm, tf), lambda j, i: (i, j)),
        out_shape=jax.ShapeDtypeStruct((m, dff), BF16),
        scratch_shapes=[pltpu.VMEM((2, d, tf), BF16)],
        compiler_params=_params(2),
        name="ffn_up",
    )(h2, w_gu, w_gu)


def _ffn_down_kernel(a_ref, w_ref, x_ref, o_ref, w_scr):
    @pl.when(pl.program_id(1) == 0)
    def _():
        w_scr[...] = w_ref[...].astype(BF16)

    o_ref[...] = x_ref[...] + _dot(a_ref[...], w_scr[...])


def _ffn_down(act, w_down, x1, tm, tn):
    m, dff = act.shape
    d = w_down.shape[1]
    return pl.pallas_call(
        _ffn_down_kernel,
        grid=(d // tn, m // tm),
        in_specs=[
            pl.BlockSpec((tm, dff), lambda j, i: (i, 0)),
            pl.BlockSpec((dff, tn), lambda j, i: (0, j)),
            pl.BlockSpec((tm, tn), lambda j, i: (i, j)),
        ],
        out_specs=pl.BlockSpec((tm, tn), lambda j, i: (i, j)),
        out_shape=jax.ShapeDtypeStruct((m, d), F32),
        scratch_shapes=[pltpu.VMEM((dff, tn), BF16)],
        compiler_params=_params(2),
        name="ffn_down",
    )(act, w_down, x1)


def _final_kernel(x_ref, pp_ref, ps_ref, wpg_ref, wple_ref, ln3_ref, lnf_ref, yp_ref, ys_ref,
                  x3_scr, *, n_prompt_tiles):
    i = pl.program_id(0)
    is_prompt = i < n_prompt_tiles
    tm, d = x_ref.shape
    h3 = _rms(x_ref[...], ln3_ref[...]).astype(BF16)
    p = jnp.where(is_prompt, pp_ref[...], ps_ref[...]).astype(BF16)
    ss = jnp.zeros((tm, 1), F32)
    for c0 in range(0, d, EPILOGUE_COLS):
        cs = slice(c0, c0 + EPILOGUE_COLS)
        gate = jax.nn.sigmoid(_dot(h3, wpg_ref[:, cs]))
        x3 = x_ref[:, cs] + gate * _dot(p, wple_ref[:, cs])
        x3_scr[:, cs] = x3
        ss = ss + jnp.sum(x3 * x3, axis=-1, keepdims=True)
    inv = lax.rsqrt(ss * (1.0 / d) + EPS)

    def write(y_ref):
        for c0 in range(0, d, EPILOGUE_COLS):
            cs = slice(c0, c0 + EPILOGUE_COLS)
            y_ref[:, cs] = x3_scr[:, cs] * inv * lnf_ref[:, cs]

    @pl.when(is_prompt)
    def _():
        write(yp_ref)

    @pl.when(jnp.logical_not(is_prompt))
    def _():
        write(ys_ref)


def _final(x2, pp, ps, w_pg, w_ple, ln3, ln_f, tm):
    m, d = x2.shape
    npr, ns = pp.shape[0], ps.shape[0]
    pd = pp.shape[1]
    npt = npr // tm
    pidx = lambda i: (jnp.minimum(i, npt - 1), 0)
    sidx = lambda i: (jnp.maximum(i - npt, 0), 0)
    return pl.pallas_call(
        functools.partial(_final_kernel, n_prompt_tiles=npt),
        grid=(m // tm,),
        in_specs=[
            pl.BlockSpec((tm, d), lambda i: (i, 0)),
            pl.BlockSpec((tm, pd), pidx),
            pl.BlockSpec((tm, pd), sidx),
            pl.BlockSpec((d, d), lambda i: (0, 0), pipeline_mode=pl.Buffered(1)),
            pl.BlockSpec((pd, d), lambda i: (0, 0)),
            pl.BlockSpec((1, d), lambda i: (0, 0)),
            pl.BlockSpec((1, d), lambda i: (0, 0)),
        ],
        out_specs=[pl.BlockSpec((tm, d), pidx), pl.BlockSpec((tm, d), sidx)],
        out_shape=[jax.ShapeDtypeStruct((npr, d), F32), jax.ShapeDtypeStruct((ns, d), F32)],
        scratch_shapes=[pltpu.VMEM((tm, d), F32)],
        compiler_params=_params(1),
        name="final",
    )(x2, pp, ps, w_pg, w_ple, ln3, ln_f)


def kernel(x_prompt, x_sample, state_hgrn, state_gla, p_prompt, p_sample, hg_lb, ln1, w_in, hg_norm,
           gla_w_gk, gla_b_gk, gla_norm, w_branch, w_out, ln2, w_gu, w_down, ln3, w_ple, w_pg, ln_f):
    batch, seq, d = x_prompt.shape
    nb, dseq, _ = x_sample.shape
    depth = w_in.shape[0]
    assert depth == 1, "single-layer step"
    npr, ns = batch * seq, nb * dseq
    xp = x_prompt.reshape(npr, d)
    xs = x_sample.reshape(ns, d)
    row = lambda v: v.reshape(1, -1)

    w_in0 = jnp.swapaxes(w_in[0], 0, 1)
    wgk = jnp.pad(gla_w_gk[0], ((0, LANE - GLA_GATE_RANK), (0, 0)))
    bgk = row(gla_b_gk[0])

    h1, glr = _prep(xp, xs, row(ln1[0]), w_in0, tm=512)
    z = _in_proj(h1, w_in0, tm=1088, tn=1024)

    a_p, s_hp = _hgrn_prompt(z, hg_lb, row(hg_norm[0]), batch, seq, hpb=8)
    b_p, s_gp = _gla_prompt(z, glr, wgk, bgk, row(gla_norm[0]), batch, seq, hpb=4)
    a_s, b_s, s_hs, s_gs = _sample_scan(z, glr, state_hgrn, state_gla, hg_lb, row(hg_norm[0]),
                                        row(gla_norm[0]), wgk, bgk, row0=npr, nb=nb, seq=dseq, bb=8)

    merged = _merge(h1, a_p, a_s, b_p, b_s, w_in0, w_branch[0], tm=512, tn=512)
    x1, h2 = _out_proj(merged, xp, xs, w_out[0].astype(BF16), row(ln2[0]), tm=512)
    act = _ffn_up(h2, w_gu[0], tm=1088, tf=512)
    x2 = _ffn_down(act, w_down[0], x1, tm=544, tn=512)
    yp, ys = _final(x2, p_prompt[0].reshape(npr, -1), p_sample[0].reshape(ns, -1),
                    w_pg[0].astype(BF16), w_ple[0].astype(BF16), row(ln3[0]), row(ln_f), tm=512)

    return (yp.reshape(batch, seq, d), ys.reshape(nb, dseq, d),
            s_hp[None], s_gp[None], s_hs, s_gs)
```

```python
import functools

import jax
import jax.numpy as jnp
from jax import lax
from jax.experimental import pallas as pl
from jax.experimental.pallas import tpu as pltpu

F32 = jnp.float32
BF16 = jnp.bfloat16

EPS = 1e-6
LANE = 128
HG_HEADS, HG_K, HG_V = 8, 128, 128
GLA_HEADS, GLA_K, GLA_V = 4, 128, 256
GLA_GATE_RANK = 16
GLA_GATE_NORM = 16.0
CHUNK = 32
MAIN_COLS = 7168
VMEM_LIMIT = 56 * 1024 * 1024
VMEM_LIMIT_LARGE = 60 * 1024 * 1024
EPILOGUE_COLS = 512
MERGE_COLS = 256

NT = (((1,), (1,)), ((), ()))
TN = (((0,), (0,)), ((), ()))


def _params(n_axes, vmem=VMEM_LIMIT):
    return pltpu.CompilerParams(dimension_semantics=("arbitrary",) * n_axes,
                                vmem_limit_bytes=vmem)


def _dot(a, b):
    return jnp.dot(a, b, preferred_element_type=F32)


def _rms(x, g):
    return x * lax.rsqrt(jnp.mean(x * x, axis=-1, keepdims=True) + EPS) * g


def _prep_kernel(xp_ref, xs_ref, ln_ref, wglr_ref, h_ref, glr_ref, *, n_prompt_tiles):
    i = pl.program_id(0)

    d = wglr_ref.shape[1]
    wglr = jnp.concatenate([wglr_ref[...], jnp.zeros((LANE - GLA_GATE_RANK, d), F32)], axis=0)
    wglr = wglr.astype(BF16)

    def body(x):
        h = _rms(x, ln_ref[...]).astype(BF16)
        h_ref[...] = h
        glr_ref[...] = lax.dot_general(h, wglr, NT, preferred_element_type=F32)

    @pl.when(i < n_prompt_tiles)
    def _():
        body(xp_ref[...])

    @pl.when(i >= n_prompt_tiles)
    def _():
        body(xs_ref[...])


def _prep(xp, xs, ln1, w_in_t, tm):
    npr, d = xp.shape
    ns = xs.shape[0]
    npt, nst = npr // tm, ns // tm
    m = npr + ns
    return pl.pallas_call(
        functools.partial(_prep_kernel, n_prompt_tiles=npt),
        grid=(npt + nst,),
        in_specs=[
            pl.BlockSpec((tm, d), lambda i: (jnp.minimum(i, npt - 1), 0)),
            pl.BlockSpec((tm, d), lambda i: (jnp.maximum(i - npt, 0), 0)),
            pl.BlockSpec((1, d), lambda i: (0, 0)),
            pl.BlockSpec((GLA_GATE_RANK, d), lambda i: (MAIN_COLS // GLA_GATE_RANK, 0)),
        ],
        out_specs=[
            pl.BlockSpec((tm, d), lambda i: (i, 0)),
            pl.BlockSpec((tm, LANE), lambda i: (i, 0)),
        ],
        out_shape=[jax.ShapeDtypeStruct((m, d), BF16),
                   jax.ShapeDtypeStruct((m, LANE), F32)],
        compiler_params=_params(1),
        name="prep",
    )(xp, xs, ln1, w_in_t)


TRANSPOSE_ROWS = 256


def _stage_transposed(w_t, dst_ref, n=None):
    for r in range(0, w_t.shape[0], TRANSPOSE_ROWS):
        blk = w_t[r:r + TRANSPOSE_ROWS, :].T.astype(BF16)
        if n is None:
            dst_ref[:, r:r + TRANSPOSE_ROWS] = blk
        else:
            dst_ref[n, :, r:r + TRANSPOSE_ROWS] = blk


def _in_proj_kernel(h_ref, w_ref, z_ref, wbf_ref):
    @pl.when(pl.program_id(1) == 0)
    def _():
        _stage_transposed(w_ref[...], wbf_ref)

    h = h_ref[...]
    for c0 in range(0, z_ref.shape[1], MERGE_COLS):
        cs = slice(c0, c0 + MERGE_COLS)
        z_ref[:, cs] = _dot(h, wbf_ref[:, cs]).astype(z_ref.dtype)


def _in_proj(h1, w_in_t, tm, tn):
    m, d = h1.shape
    return pl.pallas_call(
        _in_proj_kernel,
        grid=(MAIN_COLS // tn, m // tm),
        in_specs=[
            pl.BlockSpec((tm, d), lambda j, i: (i, 0)),
            pl.BlockSpec((tn, d), lambda j, i: (j, 0)),
        ],
        out_specs=pl.BlockSpec((tm, tn), lambda j, i: (i, j)),
        out_shape=jax.ShapeDtypeStruct((m, MAIN_COLS), BF16),
        scratch_shapes=[pltpu.VMEM((d, tn), BF16)],
        compiler_params=_params(2, VMEM_LIMIT_LARGE),
        name="in_proj",
    )(h1, w_in_t)


def _cumsum_rows(x, length):
    row = lax.broadcasted_iota(jnp.int32, x.shape, 0) % length
    s = 1
    while s < length:
        x = x + jnp.where(row >= s, pltpu.roll(x, s, axis=0), 0.0)
        s *= 2
    return x


def _forget_lower_bound(lb_rows):
    mx = jnp.max(lb_rows, axis=0, keepdims=True)
    e = jnp.exp(lb_rows - mx)
    return e[0:1, :] / jnp.sum(e, axis=0, keepdims=True)


def _col_broadcast(row):
    return jnp.broadcast_to(row, (LANE, LANE)).T


def _gated_out(o, g, gain):
    return (_rms(o, gain) * jax.nn.silu(g)).astype(BF16)


SLAB = 256
CHUNKS_PER_SLAB = SLAB // CHUNK
ATT_BLOCK = SLAB
SUBLANES = 8
DEC_ROWS = -(-CHUNKS_PER_SLAB // SUBLANES) * SUBLANES
INTRA_PARTS = 1


def _slab_rows(g):
    return pl.ds(pl.multiple_of(g * SLAB, SLAB), SLAB)


def _per_chunk_rows(rows_1w):
    w = rows_1w[0].shape[1]
    return jnp.concatenate([jnp.broadcast_to(r, (CHUNK, w)) for r in rows_1w], axis=0)


def _intra_slab(slot, h0, q, k, logf, kdim, vdim, v_slab, qin_scr, kout_scr, dec_scr, o_scr):
    heads = q.shape[1] // kdim
    kl = slice(h0 * kdim, (h0 + heads) * kdim)
    cum = _cumsum_rows(logf, CHUNK)
    ref_rows = [cum[c * CHUNK + CHUNK // 2:c * CHUNK + CHUNK // 2 + 1, :]
                for c in range(CHUNKS_PER_SLAB)]
    last_rows = [cum[(c + 1) * CHUNK - 1:(c + 1) * CHUNK, :] for c in range(CHUNKS_PER_SLAB)]
    ref = _per_chunk_rows(ref_rows)
    qe = q * jnp.exp(cum - ref)
    ke = k * jnp.exp(ref - cum)
    qin_scr[slot, :, kl] = (qe * _per_chunk_rows([jnp.exp(r) for r in ref_rows])).astype(BF16)
    kout_scr[slot, :, kl] = (
        ke * _per_chunk_rows([jnp.exp(l - r) for l, r in zip(last_rows, ref_rows)])).astype(BF16)
    for c in range(CHUNKS_PER_SLAB):
        dec_scr[slot, c:c + 1, kl] = jnp.exp(last_rows[c])
    qe = qe.astype(BF16)
    ke = ke.astype(BF16)
    r_i = lax.broadcasted_iota(jnp.int32, (ATT_BLOCK, ATT_BLOCK), 0)
    c_i = lax.broadcasted_iota(jnp.int32, (ATT_BLOCK, ATT_BLOCK), 1)
    causal = (r_i // CHUNK == c_i // CHUNK) & (c_i <= r_i)
    for h in range(heads):
        ks = slice(h * kdim, (h + 1) * kdim)
        for r0 in range(0, SLAB, ATT_BLOCK):
            rs = slice(r0, r0 + ATT_BLOCK)
            att = lax.dot_general(qe[rs, ks], ke[rs, ks], NT, preferred_element_type=F32)
            o_scr[slot, rs, (h0 + h) * vdim:(h0 + h + 1) * vdim] = _dot(
                jnp.where(causal, att, 0.0).astype(BF16), v_slab[rs, h * vdim:(h + 1) * vdim])


def _state_chunk(g, slot, c, heads, kdim, vdim, v_ref, g_ref, gain, qin_scr, kout_scr, dec_scr,
                 o_scr, st_scr, o_ref):
    cr = slice(c * CHUNK, (c + 1) * CHUNK)
    rows = pl.ds(pl.multiple_of(g * SLAB + c * CHUNK, CHUNK), CHUNK)
    drow = dec_scr[slot, c:c + 1, :]
    for h in range(heads):
        ks = slice(h * kdim, (h + 1) * kdim)
        vs = slice(h * vdim, (h + 1) * vdim)
        st = st_scr[h]
        o = o_scr[slot, cr, vs] + _dot(qin_scr[slot, cr, ks], st.astype(BF16))
        kv = lax.dot_general(kout_scr[slot, cr, ks], v_ref[rows, vs], TN,
                             preferred_element_type=F32)
        dcol = _col_broadcast(drow[:, ks])
        if vdim != LANE:
            dcol = jnp.concatenate([dcol] * (vdim // LANE), axis=1)
        st_scr[h] = st * dcol + kv
        o_ref[rows, vs] = _gated_out(o, g_ref[rows, vs].astype(F32), gain)


def _run_pipelined(n_slabs, n_parts, intra, state, st_scr, s_ref):
    assert n_slabs % 2 == 0 and n_slabs >= 4

    def overlapped(g_intra, slot_intra, g_state, slot_state):
        for p in range(max(n_parts, CHUNKS_PER_SLAB)):
            if p < n_parts:
                intra(g_intra, slot_intra, p)
            if p < CHUNKS_PER_SLAB:
                state(g_state, slot_state, p)

    st_scr[...] = jnp.zeros_like(st_scr)
    for p in range(n_parts):
        intra(0, 0, p)

    def pair(i, carry):
        overlapped(2 * i + 1, 1, 2 * i, 0)
        overlapped(2 * i + 2, 0, 2 * i + 1, 1)
        return carry

    lax.fori_loop(0, n_slabs // 2 - 1, pair, 0)
    overlapped(n_slabs - 1, 1, n_slabs - 2, 0)
    for c in range(CHUNKS_PER_SLAB):
        state(n_slabs - 1, 1, c)
    for h in range(st_scr.shape[0]):
        s_ref[0, h] = st_scr[h]


def _hgrn_prompt_kernel(lb_ref, gain_ref, q_ref, f_ref, i_ref, g_ref, o_ref, s_ref,
                        qin_scr, kout_scr, dec_scr, o_scr, st_scr, *, hpb):
    lb = _forget_lower_bound(lb_ref[...])
    gain = gain_ref[...]

    hpp = hpb // INTRA_PARTS

    def intra(g, slot, part):
        rows = _slab_rows(g)
        ls = slice(part * hpp * HG_K, (part + 1) * hpp * HG_K)
        lbp = lb[:, ls]
        q = jax.nn.silu(q_ref[rows, ls].astype(F32)) * (HG_K ** -0.5)
        sig = jax.nn.sigmoid(f_ref[rows, ls].astype(F32))
        logf = jnp.log(lbp + (1.0 - lbp) * sig)
        k = (1.0 - lbp) * (1.0 - sig)
        _intra_slab(slot, part * hpp, q, k, logf, HG_K, HG_V, i_ref[rows, ls], qin_scr, kout_scr,
                    dec_scr, o_scr)

    def state(g, slot, c):
        _state_chunk(g, slot, c, hpb, HG_K, HG_V, i_ref, g_ref, gain, qin_scr, kout_scr, dec_scr,
                     o_scr, st_scr, o_ref)

    _run_pipelined(q_ref.shape[0] // SLAB, hpb // hpp, intra, state, st_scr, s_ref)


def _hgrn_prompt(z, hg_lb, gain, batch, seq, hpb):
    w = hpb * HG_K
    sec = (HG_HEADS * HG_K) // w
    zspec = lambda s: pl.BlockSpec((seq, w), lambda b, g, s=s: (b, s * sec + g))
    return pl.pallas_call(
        functools.partial(_hgrn_prompt_kernel, hpb=hpb),
        grid=(batch, HG_HEADS // hpb),
        in_specs=[
            pl.BlockSpec((hg_lb.shape[0], w), lambda b, g: (0, g)),
            pl.BlockSpec((1, HG_V), lambda b, g: (0, 0)),
            zspec(0), zspec(1), zspec(2), zspec(3),
        ],
        out_specs=[
            pl.BlockSpec((seq, w), lambda b, g: (b, g)),
            pl.BlockSpec((1, hpb, HG_K, HG_V), lambda b, g: (b, g, 0, 0)),
        ],
        out_shape=[jax.ShapeDtypeStruct((batch * seq, HG_HEADS * HG_V), BF16),
                   jax.ShapeDtypeStruct((batch, HG_HEADS, HG_K, HG_V), F32)],
        scratch_shapes=[pltpu.VMEM((2, SLAB, w), BF16), pltpu.VMEM((2, SLAB, w), BF16),
                        pltpu.VMEM((2, DEC_ROWS, w), F32), pltpu.VMEM((2, SLAB, w), F32),
                        pltpu.VMEM((hpb, HG_K, HG_V), F32)],
        compiler_params=_params(2),
        name="hgrn_prompt",
    )(hg_lb, gain, z, z, z, z)


def _gla_prompt_kernel(gain_ref, wgk_ref, bgk_ref, glr_ref, q_ref, k_ref, v_ref, g_ref,
                       o_ref, s_ref, qin_scr, kout_scr, dec_scr, o_scr, st_scr, *, hpb):
    gain = gain_ref[...]
    wgk = wgk_ref[...].astype(BF16)
    bgk = bgk_ref[...]

    hpp = hpb // INTRA_PARTS

    def intra(g, slot, part):
        rows = _slab_rows(g)
        kl = slice(part * hpp * GLA_K, (part + 1) * hpp * GLA_K)
        vl = slice(part * hpp * GLA_V, (part + 1) * hpp * GLA_V)
        gk = _dot(glr_ref[rows, :].astype(BF16), wgk[:, kl]) + bgk[:, kl]
        logf = jax.nn.log_sigmoid(gk) / GLA_GATE_NORM
        q = q_ref[rows, kl].astype(F32) * (GLA_K ** -0.5)
        _intra_slab(slot, part * hpp, q, k_ref[rows, kl].astype(F32), logf, GLA_K, GLA_V,
                    v_ref[rows, vl], qin_scr, kout_scr, dec_scr, o_scr)

    def state(g, slot, c):
        _state_chunk(g, slot, c, hpb, GLA_K, GLA_V, v_ref, g_ref, gain, qin_scr, kout_scr, dec_scr,
                     o_scr, st_scr, o_ref)

    _run_pipelined(q_ref.shape[0] // SLAB, hpb // hpp, intra, state, st_scr, s_ref)


def _gla_prompt(z, glr, wgk, bgk, gain, batch, seq, hpb):
    kw, vw = hpb * GLA_K, hpb * GLA_V
    q0 = (4 * HG_HEADS * HG_K) // kw
    k0 = q0 + (GLA_HEADS * GLA_K) // kw
    v0 = (4 * HG_HEADS * HG_K + 2 * GLA_HEADS * GLA_K) // vw
    g0 = v0 + (GLA_HEADS * GLA_V) // vw
    return pl.pallas_call(
        functools.partial(_gla_prompt_kernel, hpb=hpb),
        grid=(batch, GLA_HEADS // hpb),
        in_specs=[
            pl.BlockSpec((1, GLA_V), lambda b, g: (0, 0)),
            pl.BlockSpec((LANE, kw), lambda b, g: (0, g)),
            pl.BlockSpec((1, kw), lambda b, g: (0, g)),
            pl.BlockSpec((seq, LANE), lambda b, g: (b, 0)),
            pl.BlockSpec((seq, kw), lambda b, g: (b, q0 + g)),
            pl.BlockSpec((seq, kw), lambda b, g: (b, k0 + g)),
            pl.BlockSpec((seq, vw), lambda b, g: (b, v0 + g)),
            pl.BlockSpec((seq, vw), lambda b, g: (b, g0 + g)),
        ],
        out_specs=[
            pl.BlockSpec((seq, vw), lambda b, g: (b, g)),
            pl.BlockSpec((1, hpb, GLA_K, GLA_V), lambda b, g: (b, g, 0, 0)),
        ],
        out_shape=[jax.ShapeDtypeStruct((batch * seq, GLA_HEADS * GLA_V), BF16),
                   jax.ShapeDtypeStruct((batch, GLA_HEADS, GLA_K, GLA_V), F32)],
        scratch_shapes=[pltpu.VMEM((2, SLAB, kw), BF16), pltpu.VMEM((2, SLAB, kw), BF16),
                        pltpu.VMEM((2, DEC_ROWS, kw), F32), pltpu.VMEM((2, SLAB, vw), F32),
                        pltpu.VMEM((hpb, GLA_K, GLA_V), F32)],
        compiler_params=_params(2),
        name="gla_prompt",
    )(gain, wgk, bgk, glr, z, z, z, z)


def _seg_pick(x, seg, idx):
    n = x.shape[0]
    rowmod = lax.broadcasted_iota(jnp.int32, x.shape, 0) % seg
    out = x
    for m in range(seg):
        if m != idx:
            out = jnp.where(rowmod == m, pltpu.roll(x, (m - idx) % n, axis=0), out)
    return out


def _sample_branch(q, k, v_b, logf, g, gain, s_ref, ns_ref, o_ref, o_scr, *, heads, kdim, vdim,
                   seq, pairs):
    tile = 2 * seq
    cum = _cumsum_rows(logf, seq)
    ref = _seg_pick(cum, seq, seq // 2)
    last = _seg_pick(cum, seq, seq - 1)
    qe = (q * jnp.exp(cum - ref)).astype(BF16)
    ke = (k * jnp.exp(ref - cum)).astype(BF16)
    q_in = (q * jnp.exp(cum)).astype(BF16)
    k_out = k * jnp.exp(last - cum)
    dec = jnp.exp(last)
    r_i = lax.broadcasted_iota(jnp.int32, (tile, tile), 0)
    c_i = lax.broadcasted_iota(jnp.int32, (tile, tile), 1)
    amask = (r_i // seq == c_i // seq) & (c_i <= r_i)
    half = lax.broadcasted_iota(jnp.int32, (tile, 1), 0) // seq
    for p in range(pairs):
        rs = slice(p * tile, (p + 1) * tile)
        for h in range(heads):
            ks = slice(h * kdim, (h + 1) * kdim)
            vs = slice(h * vdim, (h + 1) * vdim)
            v_t = v_b[rs, vs]
            att = lax.dot_general(qe[rs, ks], ke[rs, ks], NT, preferred_element_type=F32)
            o = _dot(jnp.where(amask, att, 0.0).astype(BF16), v_t)
            for bi in range(2):
                b = 2 * p + bi
                s0 = s_ref[0, b, h]
                o = o + jnp.where(half == bi, _dot(q_in[rs, ks], s0.astype(BF16)), 0.0)
                km = jnp.where(half == bi, k_out[rs, ks], 0.0).astype(BF16)
                kv = lax.dot_general(km, v_t, TN, preferred_element_type=F32)
                r0 = p * tile + bi * seq
                dcol = _col_broadcast(dec[r0:r0 + 1, ks])
                if vdim != LANE:
                    dcol = jnp.concatenate([dcol] * (vdim // LANE), axis=1)
                ns_ref[0, b, h] = s0 * dcol + kv
            o_scr[rs, vs] = o
    for h in range(heads):
        vs = slice(h * vdim, (h + 1) * vdim)
        o_ref[:, vs] = _gated_out(o_scr[:, vs], g[:, vs].astype(F32), gain)


def _sample_kernel(lb_ref, hgain_ref, ggain_ref, wgk_ref, bgk_ref,
                   hq_ref, hf_ref, hi_ref, hg_ref, gq_ref, gk_ref, gv_ref, gg_ref, glr_ref,
                   sh_ref, sg_ref,
                   a_ref, b_ref, nsh_ref, nsg_ref, oh_scr, og_scr, *, seq, pairs):
    lb = _forget_lower_bound(lb_ref[...])
    sig = jax.nn.sigmoid(hf_ref[...].astype(F32))
    _sample_branch(
        jax.nn.silu(hq_ref[...].astype(F32)) * (HG_K ** -0.5),
        (1.0 - lb) * (1.0 - sig),
        hi_ref[...],
        jnp.log(lb + (1.0 - lb) * sig),
        hg_ref[...], hgain_ref[...], sh_ref, nsh_ref, a_ref, oh_scr,
        heads=HG_HEADS, kdim=HG_K, vdim=HG_V, seq=seq, pairs=pairs)
    gk = _dot(glr_ref[...].astype(BF16), wgk_ref[...].astype(BF16)) + bgk_ref[...]
    _sample_branch(
        gq_ref[...].astype(F32) * (GLA_K ** -0.5),
        gk_ref[...].astype(F32),
        gv_ref[...],
        jax.nn.log_sigmoid(gk) / GLA_GATE_NORM,
        gg_ref[...], ggain_ref[...], sg_ref, nsg_ref, b_ref, og_scr,
        heads=GLA_HEADS, kdim=GLA_K, vdim=GLA_V, seq=seq, pairs=pairs)


def _sample_scan(z, glr, state_hgrn, state_gla, hg_lb, hgain, ggain, wgk, bgk, row0, nb, seq, bb):
    rows = bb * seq
    rb0 = row0 // rows
    hw, kw, vw = HG_HEADS * HG_K, GLA_HEADS * GLA_K, GLA_HEADS * GLA_V
    gq0 = (4 * hw) // kw
    gv0 = (4 * hw + 2 * kw) // vw
    zs = lambda w, c: pl.BlockSpec((rows, w), lambda i, c=c: (rb0 + i, c))
    const = lambda shape: pl.BlockSpec(shape, lambda i: (0,) * len(shape))
    st = lambda h, k, v: pl.BlockSpec((1, bb, h, k, v), lambda i: (0, i, 0, 0, 0))
    return pl.pallas_call(
        functools.partial(_sample_kernel, seq=seq, pairs=bb // 2),
        grid=(nb // bb,),
        in_specs=[
            const(hg_lb.shape), const((1, HG_V)), const((1, GLA_V)), const((LANE, kw)),
            const((1, kw)),
            zs(hw, 0), zs(hw, 1), zs(hw, 2), zs(hw, 3),
            zs(kw, gq0), zs(kw, gq0 + 1), zs(vw, gv0), zs(vw, gv0 + 1),
            pl.BlockSpec((rows, LANE), lambda i: (rb0 + i, 0)),
            st(HG_HEADS, HG_K, HG_V), st(GLA_HEADS, GLA_K, GLA_V),
        ],
        out_specs=[
            pl.BlockSpec((rows, hw), lambda i: (i, 0)),
            pl.BlockSpec((rows, vw), lambda i: (i, 0)),
            st(HG_HEADS, HG_K, HG_V), st(GLA_HEADS, GLA_K, GLA_V),
        ],
        out_shape=[jax.ShapeDtypeStruct((nb * seq, hw), BF16),
                   jax.ShapeDtypeStruct((nb * seq, vw), BF16),
                   jax.ShapeDtypeStruct(state_hgrn.shape, state_hgrn.dtype),
                   jax.ShapeDtypeStruct(state_gla.shape, state_gla.dtype)],
        scratch_shapes=[pltpu.VMEM((rows, hw), F32), pltpu.VMEM((rows, vw), F32)],
        compiler_params=_params(1),
        name="sample_scan",
    )(hg_lb, hgain, ggain, wgk, bgk, z, z, z, z, z, z, z, z, glr, state_hgrn, state_gla)


def _merge_kernel(h_ref, ap_ref, as_ref, bp_ref, bs_ref, wg0_ref, wx0_ref, wg1_ref, wx1_ref,
                  wb0_ref, wb1_ref, o_ref, wg_scr, wb_scr, *, n_prompt_tiles):
    tn = o_ref.shape[1]
    is_prompt = pl.program_id(1) < n_prompt_tiles

    @pl.when(pl.program_id(1) == 0)
    def _():
        for n, (wg_ref, wx_ref) in enumerate(((wg0_ref, wx0_ref), (wg1_ref, wx1_ref))):
            cat = jnp.concatenate([wg_ref[...], wx_ref[...]], axis=0)
            _stage_transposed(cat[GLA_GATE_RANK:GLA_GATE_RANK + tn, :], wg_scr, n)
        wb_scr[0] = wb0_ref[0].astype(BF16)
        wb_scr[1] = wb1_ref[0].astype(BF16)

    h = h_ref[...]
    a = jnp.where(is_prompt, ap_ref[...], as_ref[...])
    b = jnp.where(is_prompt, bp_ref[...], bs_ref[...])
    for c0 in range(0, tn, MERGE_COLS):
        cs = slice(c0, c0 + MERGE_COLS)
        m = jax.nn.sigmoid(_dot(h, wg_scr[0, :, cs])) * _dot(a, wb_scr[0, :, cs])
        m = m + jax.nn.sigmoid(_dot(h, wg_scr[1, :, cs])) * _dot(b, wb_scr[1, :, cs])
        o_ref[:, cs] = m.astype(o_ref.dtype)


def _merge(h1, a_p, a_s, b_p, b_s, w_in_t, w_branch, tm, tn):
    m, d = h1.shape
    bw = a_p.shape[1]
    npt = a_p.shape[0] // tm
    rk = GLA_GATE_RANK
    pspec = pl.BlockSpec((tm, bw), lambda j, i: (jnp.minimum(i, npt - 1), 0))
    sspec = pl.BlockSpec((tm, bw), lambda j, i: (jnp.maximum(i - npt, 0), 0))
    main = lambda n: pl.BlockSpec((tn, d), lambda j, i, n=n: ((MAIN_COLS + n * d) // tn + j, 0))
    extra = lambda n: pl.BlockSpec(
        (rk, d), lambda j, i, n=n: ((MAIN_COLS + n * d) // rk + (j + 1) * (tn // rk), 0))
    return pl.pallas_call(
        functools.partial(_merge_kernel, n_prompt_tiles=npt),
        grid=(d // tn, m // tm),
        in_specs=[
            pl.BlockSpec((tm, d), lambda j, i: (i, 0)),
            pspec, sspec, pspec, sspec,
            main(0), extra(0), main(1), extra(1),
            pl.BlockSpec((1, bw, tn), lambda j, i: (0, 0, j)),
            pl.BlockSpec((1, bw, tn), lambda j, i: (1, 0, j)),
        ],
        out_specs=pl.BlockSpec((tm, tn), lambda j, i: (i, j)),
        out_shape=jax.ShapeDtypeStruct((m, d), BF16),
        scratch_shapes=[pltpu.VMEM((2, d, tn), BF16), pltpu.VMEM((2, bw, tn), BF16)],
        compiler_params=_params(2),
        name="merge",
    )(h1, a_p, a_s, b_p, b_s, w_in_t, w_in_t, w_in_t, w_in_t, w_branch, w_branch)


def _out_proj_kernel(m_ref, xp_ref, xs_ref, w_ref, ln_ref, x1_ref, h2_ref, *, n_prompt_tiles):
    is_prompt = pl.program_id(0) < n_prompt_tiles
    m = m_ref[...]
    tm, d = x1_ref.shape
    ss = jnp.zeros((tm, 1), F32)
    for c0 in range(0, d, EPILOGUE_COLS):
        cs = slice(c0, c0 + EPILOGUE_COLS)
        x1 = jnp.where(is_prompt, xp_ref[:, cs], xs_ref[:, cs]) + _dot(m, w_ref[:, cs])
        x1_ref[:, cs] = x1
        ss = ss + jnp.sum(x1 * x1, axis=-1, keepdims=True)
    inv = lax.rsqrt(ss * (1.0 / d) + EPS)
    for c0 in range(0, d, EPILOGUE_COLS):
        cs = slice(c0, c0 + EPILOGUE_COLS)
        h2_ref[:, cs] = (x1_ref[:, cs] * inv * ln_ref[:, cs]).astype(BF16)


def _out_proj(merged, xp, xs, w_out, ln2, tm):
    m, d = merged.shape
    npt = xp.shape[0] // tm
    return pl.pallas_call(
        functools.partial(_out_proj_kernel, n_prompt_tiles=npt),
        grid=(m // tm,),
        in_specs=[
            pl.BlockSpec((tm, d), lambda i: (i, 0)),
            pl.BlockSpec((tm, d), lambda i: (jnp.minimum(i, npt - 1), 0)),
            pl.BlockSpec((tm, d), lambda i: (jnp.maximum(i - npt, 0), 0)),
            pl.BlockSpec((d, d), lambda i: (0, 0), pipeline_mode=pl.Buffered(1)),
            pl.BlockSpec((1, d), lambda i: (0, 0)),
        ],
        out_specs=[pl.BlockSpec((tm, d), lambda i: (i, 0)),
                   pl.BlockSpec((tm, d), lambda i: (i, 0))],
        out_shape=[jax.ShapeDtypeStruct((m, d), F32), jax.ShapeDtypeStruct((m, d), BF16)],
        compiler_params=_params(1),
        name="out_proj",
    )(merged, xp, xs, w_out, ln2)


def _ffn_up_kernel(h_ref, wg_ref, wu_ref, o_ref, w_scr):
    @pl.when(pl.program_id(1) == 0)
    def _():
        w_scr[0] = wg_ref[...].astype(BF16)
        w_scr[1] = wu_ref[...].astype(BF16)

    h = h_ref[...]
    for c0 in range(0, o_ref.shape[1], MERGE_COLS):
        cs = slice(c0, c0 + MERGE_COLS)
        gate = _dot(h, w_scr[0, :, cs])
        o_ref[:, cs] = (jax.nn.silu(gate) * _dot(h, w_scr[1, :, cs])).astype(o_ref.dtype)


def _ffn_up(h2, w_gu, tm, tf):
    m, d = h2.shape
    dff = w_gu.shape[1] // 2
    nj = dff // tf
    return pl.pallas_call(
        _ffn_up_kernel,
        grid=(nj, m // tm),
        in_specs=[
            pl.BlockSpec((tm, d), lambda j, i: (i, 0)),
            pl.BlockSpec((d, tf), lambda j, i: (0, j)),
            pl.BlockSpec((d, tf), lambda j, i: (0, nj + j)),
        ],
        out_specs=pl.BlockSpec((tm, tf), lambda j, i: (i, j)),
        out_shape=jax.ShapeDtypeStruct((m, dff), BF16),
        scratch_shapes=[pltpu.VMEM((2, d, tf), BF16)],
        compiler_params=_params(2),
        name="ffn_up",
    )(h2, w_gu, w_gu)


def _ffn_down_kernel(a_ref, w_ref, x_ref, o_ref, w_scr):
    @pl.when(pl.program_id(1) == 0)
    def _():
        w_scr[...] = w_ref[...].astype(BF16)

    o_ref[...] = x_ref[...] + _dot(a_ref[...], w_scr[...])


def _ffn_down(act, w_down, x1, tm, tn):
    m, dff = act.shape
    d = w_down.shape[1]
    return pl.pallas_call(
        _ffn_down_kernel,
        grid=(d // tn, m // tm),
        in_specs=[
            pl.BlockSpec((tm, dff), lambda j, i: (i, 0)),
            pl.BlockSpec((dff, tn), lambda j, i: (0, j)),
            pl.BlockSpec((tm, tn), lambda j, i: (i, j)),
        ],
        out_specs=pl.BlockSpec((tm, tn), lambda j, i: (i, j)),
        out_shape=jax.ShapeDtypeStruct((m, d), F32),
        scratch_shapes=[pltpu.VMEM((dff, tn), BF16)],
        compiler_params=_params(2),
        name="ffn_down",
    )(act, w_down, x1)


def _final_kernel(x_ref, pp_ref, ps_ref, wpg_ref, wple_ref, ln3_ref, lnf_ref, yp_ref, ys_ref,
                  x3_scr, *, n_prompt_tiles):
    i = pl.program_id(0)
    is_prompt = i < n_prompt_tiles
    tm, d = x_ref.shape
    h3 = _rms(x_ref[...], ln3_ref[...]).astype(BF16)
    p = jnp.where(is_prompt, pp_ref[...], ps_ref[...]).astype(BF16)
    ss = jnp.zeros((tm, 1), F32)
    for c0 in range(0, d, EPILOGUE_COLS):
        cs = slice(c0, c0 + EPILOGUE_COLS)
        gate = jax.nn.sigmoid(_dot(h3, wpg_ref[:, cs]))
        x3 = x_ref[:, cs] + gate * _dot(p, wple_ref[:, cs])
        x3_scr[:, cs] = x3
        ss = ss + jnp.sum(x3 * x3, axis=-1, keepdims=True)
    inv = lax.rsqrt(ss * (1.0 / d) + EPS)

    def write(y_ref):
        for c0 in range(0, d, EPILOGUE_COLS):
            cs = slice(c0, c0 + EPILOGUE_COLS)
            y_ref[:, cs] = x3_scr[:, cs] * inv * lnf_ref[:, cs]

    @pl.when(is_prompt)
    def _():
        write(yp_ref)

    @pl.when(jnp.logical_not(is_prompt))
    def _():
        write(ys_ref)


def _final(x2, pp, ps, w_pg, w_ple, ln3, ln_f, tm):
    m, d = x2.shape
    npr, ns = pp.shape[0], ps.shape[0]
    pd = pp.shape[1]
    npt = npr // tm
    pidx = lambda i: (jnp.minimum(i, npt - 1), 0)
    sidx = lambda i: (jnp.maximum(i - npt, 0), 0)
    return pl.pallas_call(
        functools.partial(_final_kernel, n_prompt_tiles=npt),
        grid=(m // tm,),
        in_specs=[
            pl.BlockSpec((tm, d), lambda i: (i, 0)),
            pl.BlockSpec((tm, pd), pidx),
            pl.BlockSpec((tm, pd), sidx),
            pl.BlockSpec((d, d), lambda i: (0, 0), pipeline_mode=pl.Buffered(1)),
            pl.BlockSpec((pd, d), lambda i: (0, 0)),
            pl.BlockSpec((1, d), lambda i: (0, 0)),
            pl.BlockSpec((1, d), lambda i: (0, 0)),
        ],
        out_specs=[pl.BlockSpec((tm, d), pidx), pl.BlockSpec((tm, d), sidx)],
        out_shape=[jax.ShapeDtypeStruct((npr, d), F32), jax.ShapeDtypeStruct((ns, d), F32)],
        scratch_shapes=[pltpu.VMEM((tm, d), F32)],
        compiler_params=_params(1),
        name="final",
    )(x2, pp, ps, w_pg, w_ple, ln3, ln_f)


def kernel(x_prompt, x_sample, state_hgrn, state_gla, p_prompt, p_sample, hg_lb, ln1, w_in, hg_norm,
           gla_w_gk, gla_b_gk, gla_norm, w_branch, w_out, ln2, w_gu, w_down, ln3, w_ple, w_pg, ln_f):
    batch, seq, d = x_prompt.shape
    nb, dseq, _ = x_sample.shape
    depth = w_in.shape[0]
    assert depth == 1, "single-layer step"
    npr, ns = batch * seq, nb * dseq
    xp = x_prompt.reshape(npr, d)
    xs = x_sample.reshape(ns, d)
    row = lambda v: v.reshape(1, -1)

    w_in0 = jnp.swapaxes(w_in[0], 0, 1)
    wgk = jnp.pad(gla_w_gk[0], ((0, LANE - GLA_GATE_RANK), (0, 0)))
    bgk = row(gla_b_gk[0])

    h1, glr = _prep(xp, xs, row(ln1[0]), w_in0, tm=512)
    z = _in_proj(h1, w_in0, tm=2176, tn=1024)

    a_p, s_hp = _hgrn_prompt(z, hg_lb, row(hg_norm[0]), batch, seq, hpb=8)
    b_p, s_gp = _gla_prompt(z, glr, wgk, bgk, row(gla_norm[0]), batch, seq, hpb=4)
    a_s, b_s, s_hs, s_gs = _sample_scan(z, glr, state_hgrn, state_gla, hg_lb, row(hg_norm[0]),
                                        row(gla_norm[0]), wgk, bgk, row0=npr, nb=nb, seq=dseq, bb=8)

    merged = _merge(h1, a_p, a_s, b_p, b_s, w_in0, w_branch[0], tm=512, tn=512)
    x1, h2 = _out_proj(merged, xp, xs, w_out[0].astype(BF16), row(ln2[0]), tm=512)
    act = _ffn_up(h2, w_gu[0], tm=2176, tf=512)
    x2 = _ffn_down(act, w_down[0], x1, tm=544, tn=512)
    yp, ys = _final(x2, p_prompt[0].reshape(npr, -1), p_sample[0].reshape(ns, -1),
                    w_pg[0].astype(BF16), w_ple[0].astype(BF16), row(ln3[0]), row(ln_f), tm=512)

    return (yp.reshape(batch, seq, d), ys.reshape(nb, dseq, d),
            s_hp[None], s_gp[None], s_hs, s_gs)
```

```python
import functools

import jax
import jax.numpy as jnp
from jax import lax
from jax.experimental import pallas as pl
from jax.experimental.pallas import tpu as pltpu

F32 = jnp.float32
BF16 = jnp.bfloat16

EPS = 1e-6
LANE = 128
HG_HEADS, HG_K, HG_V = 8, 128, 128
GLA_HEADS, GLA_K, GLA_V = 4, 128, 256
GLA_GATE_RANK = 16
GLA_GATE_NORM = 16.0
CHUNK = 32
MAIN_COLS = 7168
VMEM_LIMIT = 56 * 1024 * 1024
VMEM_LIMIT_LARGE = 60 * 1024 * 1024
EPILOGUE_COLS = 512
MERGE_COLS = 256

NT = (((1,), (1,)), ((), ()))
TN = (((0,), (0,)), ((), ()))


def _params(n_axes, vmem=VMEM_LIMIT):
    return pltpu.CompilerParams(dimension_semantics=("arbitrary",) * n_axes,
                                vmem_limit_bytes=vmem)


def _dot(a, b):
    return jnp.dot(a, b, preferred_element_type=F32)


def _rms(x, g):
    return x * lax.rsqrt(jnp.mean(x * x, axis=-1, keepdims=True) + EPS) * g


def _prep_kernel(xp_ref, xs_ref, ln_ref, wglr_ref, h_ref, glr_ref, *, n_prompt_tiles):
    i = pl.program_id(0)

    d = wglr_ref.shape[1]
    wglr = jnp.concatenate([wglr_ref[...], jnp.zeros((LANE - GLA_GATE_RANK, d), F32)], axis=0)
    wglr = wglr.astype(BF16)

    def body(x):
        h = _rms(x, ln_ref[...]).astype(BF16)
        h_ref[...] = h
        glr_ref[...] = lax.dot_general(h, wglr, NT, preferred_element_type=F32)

    @pl.when(i < n_prompt_tiles)
    def _():
        body(xp_ref[...])

    @pl.when(i >= n_prompt_tiles)
    def _():
        body(xs_ref[...])


def _prep(xp, xs, ln1, w_in_t, tm):
    npr, d = xp.shape
    ns = xs.shape[0]
    npt, nst = npr // tm, ns // tm
    m = npr + ns
    return pl.pallas_call(
        functools.partial(_prep_kernel, n_prompt_tiles=npt),
        grid=(npt + nst,),
        in_specs=[
            pl.BlockSpec((tm, d), lambda i: (jnp.minimum(i, npt - 1), 0)),
            pl.BlockSpec((tm, d), lambda i: (jnp.maximum(i - npt, 0), 0)),
            pl.BlockSpec((1, d), lambda i: (0, 0)),
            pl.BlockSpec((GLA_GATE_RANK, d), lambda i: (MAIN_COLS // GLA_GATE_RANK, 0)),
        ],
        out_specs=[
            pl.BlockSpec((tm, d), lambda i: (i, 0)),
            pl.BlockSpec((tm, LANE), lambda i: (i, 0)),
        ],
        out_shape=[jax.ShapeDtypeStruct((m, d), BF16),
                   jax.ShapeDtypeStruct((m, LANE), F32)],
        compiler_params=_params(1),
        name="prep",
    )(xp, xs, ln1, w_in_t)


TRANSPOSE_ROWS = 256


def _stage_transposed(w_t, dst_ref, n=None):
    for r in range(0, w_t.shape[0], TRANSPOSE_ROWS):
        blk = w_t[r:r + TRANSPOSE_ROWS, :].T.astype(BF16)
        if n is None:
            dst_ref[:, r:r + TRANSPOSE_ROWS] = blk
        else:
            dst_ref[n, :, r:r + TRANSPOSE_ROWS] = blk


def _in_proj_kernel(h_ref, w_ref, z_ref, wbf_ref):
    @pl.when(pl.program_id(1) == 0)
    def _():
        _stage_transposed(w_ref[...], wbf_ref)

    h = h_ref[...]
    for c0 in range(0, z_ref.shape[1], MERGE_COLS):
        cs = slice(c0, c0 + MERGE_COLS)
        z_ref[:, cs] = _dot(h, wbf_ref[:, cs]).astype(z_ref.dtype)


def _in_proj(h1, w_in_t, tm, tn):
    m, d = h1.shape
    return pl.pallas_call(
        _in_proj_kernel,
        grid=(MAIN_COLS // tn, m // tm),
        in_specs=[
            pl.BlockSpec((tm, d), lambda j, i: (i, 0)),
            pl.BlockSpec((tn, d), lambda j, i: (j, 0)),
        ],
        out_specs=pl.BlockSpec((tm, tn), lambda j, i: (i, j)),
        out_shape=jax.ShapeDtypeStruct((m, MAIN_COLS), BF16),
        scratch_shapes=[pltpu.VMEM((d, tn), BF16)],
        compiler_params=_params(2, VMEM_LIMIT_LARGE),
        name="in_proj",
    )(h1, w_in_t)


def _cumsum_rows(x, length):
    row = lax.broadcasted_iota(jnp.int32, x.shape, 0) % length
    s = 1
    while s < length:
        x = x + jnp.where(row >= s, pltpu.roll(x, s, axis=0), 0.0)
        s *= 2
    return x


def _forget_lower_bound(lb_rows):
    mx = jnp.max(lb_rows, axis=0, keepdims=True)
    e = jnp.exp(lb_rows - mx)
    return e[0:1, :] / jnp.sum(e, axis=0, keepdims=True)


def _col_broadcast(row):
    return jnp.broadcast_to(row, (LANE, LANE)).T


def _gated_out(o, g, gain):
    return (_rms(o, gain) * jax.nn.silu(g)).astype(BF16)


SLAB = 256
CHUNKS_PER_SLAB = SLAB // CHUNK
ATT_BLOCK = SLAB
SUBLANES = 8
DEC_ROWS = -(-CHUNKS_PER_SLAB // SUBLANES) * SUBLANES
INTRA_PARTS = 1


def _slab_rows(g):
    return pl.ds(pl.multiple_of(g * SLAB, SLAB), SLAB)


def _per_chunk_rows(rows_1w):
    w = rows_1w[0].shape[1]
    return jnp.concatenate([jnp.broadcast_to(r, (CHUNK, w)) for r in rows_1w], axis=0)


def _intra_slab(slot, h0, q, k, logf, kdim, vdim, v_slab, qin_scr, kout_scr, dec_scr, o_scr):
    heads = q.shape[1] // kdim
    kl = slice(h0 * kdim, (h0 + heads) * kdim)
    cum = _cumsum_rows(logf, CHUNK)
    ref_rows = [cum[c * CHUNK + CHUNK // 2:c * CHUNK + CHUNK // 2 + 1, :]
                for c in range(CHUNKS_PER_SLAB)]
    last_rows = [cum[(c + 1) * CHUNK - 1:(c + 1) * CHUNK, :] for c in range(CHUNKS_PER_SLAB)]
    ref = _per_chunk_rows(ref_rows)
    qe = q * jnp.exp(cum - ref)
    ke = k * jnp.exp(ref - cum)
    qin_scr[slot, :, kl] = (qe * _per_chunk_rows([jnp.exp(r) for r in ref_rows])).astype(BF16)
    kout_scr[slot, :, kl] = (
        ke * _per_chunk_rows([jnp.exp(l - r) for l, r in zip(last_rows, ref_rows)])).astype(BF16)
    for c in range(CHUNKS_PER_SLAB):
        dec_scr[slot, c:c + 1, kl] = jnp.exp(last_rows[c])
    qe = qe.astype(BF16)
    ke = ke.astype(BF16)
    r_i = lax.broadcasted_iota(jnp.int32, (ATT_BLOCK, ATT_BLOCK), 0)
    c_i = lax.broadcasted_iota(jnp.int32, (ATT_BLOCK, ATT_BLOCK), 1)
    causal = (r_i // CHUNK == c_i // CHUNK) & (c_i <= r_i)
    for h in range(heads):
        ks = slice(h * kdim, (h + 1) * kdim)
        for r0 in range(0, SLAB, ATT_BLOCK):
            rs = slice(r0, r0 + ATT_BLOCK)
            att = lax.dot_general(qe[rs, ks], ke[rs, ks], NT, preferred_element_type=F32)
            o_scr[slot, rs, (h0 + h) * vdim:(h0 + h + 1) * vdim] = _dot(
                jnp.where(causal, att, 0.0).astype(BF16), v_slab[rs, h * vdim:(h + 1) * vdim])


def _state_chunk(g, slot, c, heads, kdim, vdim, v_ref, g_ref, gain, qin_scr, kout_scr, dec_scr,
                 o_scr, st_scr, o_ref):
    cr = slice(c * CHUNK, (c + 1) * CHUNK)
    rows = pl.ds(pl.multiple_of(g * SLAB + c * CHUNK, CHUNK), CHUNK)
    drow = dec_scr[slot, c:c + 1, :]
    for h in range(heads):
        ks = slice(h * kdim, (h + 1) * kdim)
        vs = slice(h * vdim, (h + 1) * vdim)
        st = st_scr[h]
        o = o_scr[slot, cr, vs] + _dot(qin_scr[slot, cr, ks], st.astype(BF16))
        kv = lax.dot_general(kout_scr[slot, cr, ks], v_ref[rows, vs], TN,
                             preferred_element_type=F32)
        dcol = _col_broadcast(drow[:, ks])
        if vdim != LANE:
            dcol = jnp.concatenate([dcol] * (vdim // LANE), axis=1)
        st_scr[h] = st * dcol + kv
        o_ref[rows, vs] = _gated_out(o, g_ref[rows, vs].astype(F32), gain)


def _run_pipelined(n_slabs, n_parts, intra, state, st_scr, s_ref):
    assert n_slabs % 2 == 0 and n_slabs >= 4

    def overlapped(g_intra, slot_intra, g_state, slot_state):
        for p in range(max(n_parts, CHUNKS_PER_SLAB)):
            if p < n_parts:
                intra(g_intra, slot_intra, p)
            if p < CHUNKS_PER_SLAB:
                state(g_state, slot_state, p)

    st_scr[...] = jnp.zeros_like(st_scr)
    for p in range(n_parts):
        intra(0, 0, p)

    def pair(i, carry):
        overlapped(2 * i + 1, 1, 2 * i, 0)
        overlapped(2 * i + 2, 0, 2 * i + 1, 1)
        return carry

    lax.fori_loop(0, n_slabs // 2 - 1, pair, 0)
    overlapped(n_slabs - 1, 1, n_slabs - 2, 0)
    for c in range(CHUNKS_PER_SLAB):
        state(n_slabs - 1, 1, c)
    for h in range(st_scr.shape[0]):
        s_ref[0, h] = st_scr[h]


def _hgrn_prompt_kernel(lb_ref, gain_ref, q_ref, f_ref, i_ref, g_ref, o_ref, s_ref,
                        qin_scr, kout_scr, dec_scr, o_scr, st_scr, *, hpb):
    lb = _forget_lower_bound(lb_ref[...])
    gain = gain_ref[...]

    hpp = hpb // INTRA_PARTS

    def intra(g, slot, part):
        rows = _slab_rows(g)
        ls = slice(part * hpp * HG_K, (part + 1) * hpp * HG_K)
        lbp = lb[:, ls]
        q = jax.nn.silu(q_ref[rows, ls].astype(F32)) * (HG_K ** -0.5)
        sig = jax.nn.sigmoid(f_ref[rows, ls].astype(F32))
        logf = jnp.log(lbp + (1.0 - lbp) * sig)
        k = (1.0 - lbp) * (1.0 - sig)
        _intra_slab(slot, part * hpp, q, k, logf, HG_K, HG_V, i_ref[rows, ls], qin_scr, kout_scr,
                    dec_scr, o_scr)

    def state(g, slot, c):
        _state_chunk(g, slot, c, hpb, HG_K, HG_V, i_ref, g_ref, gain, qin_scr, kout_scr, dec_scr,
                     o_scr, st_scr, o_ref)

    _run_pipelined(q_ref.shape[0] // SLAB, hpb // hpp, intra, state, st_scr, s_ref)


def _hgrn_prompt(z, hg_lb, gain, batch, seq, hpb):
    w = hpb * HG_K
    sec = (HG_HEADS * HG_K) // w
    zspec = lambda s: pl.BlockSpec((seq, w), lambda b, g, s=s: (b, s * sec + g))
    return pl.pallas_call(
        functools.partial(_hgrn_prompt_kernel, hpb=hpb),
        grid=(batch, HG_HEADS // hpb),
        in_specs=[
            pl.BlockSpec((hg_lb.shape[0], w), lambda b, g: (0, g)),
            pl.BlockSpec((1, HG_V), lambda b, g: (0, 0)),
            zspec(0), zspec(1), zspec(2), zspec(3),
        ],
        out_specs=[
            pl.BlockSpec((seq, w), lambda b, g: (b, g)),
            pl.BlockSpec((1, hpb, HG_K, HG_V), lambda b, g: (b, g, 0, 0)),
        ],
        out_shape=[jax.ShapeDtypeStruct((batch * seq, HG_HEADS * HG_V), BF16),
                   jax.ShapeDtypeStruct((batch, HG_HEADS, HG_K, HG_V), F32)],
        scratch_shapes=[pltpu.VMEM((2, SLAB, w), BF16), pltpu.VMEM((2, SLAB, w), BF16),
                        pltpu.VMEM((2, DEC_ROWS, w), F32), pltpu.VMEM((2, SLAB, w), F32),
                        pltpu.VMEM((hpb, HG_K, HG_V), F32)],
        compiler_params=_params(2),
        name="hgrn_prompt",
    )(hg_lb, gain, z, z, z, z)


def _gla_prompt_kernel(gain_ref, wgk_ref, bgk_ref, glr_ref, q_ref, k_ref, v_ref, g_ref,
                       o_ref, s_ref, qin_scr, kout_scr, dec_scr, o_scr, st_scr, *, hpb):
    gain = gain_ref[...]
    wgk = wgk_ref[...].astype(BF16)
    bgk = bgk_ref[...]

    hpp = hpb // INTRA_PARTS

    def intra(g, slot, part):
        rows = _slab_rows(g)
        kl = slice(part * hpp * GLA_K, (part + 1) * hpp * GLA_K)
        vl = slice(part * hpp * GLA_V, (part + 1) * hpp * GLA_V)
        gk = _dot(glr_ref[rows, :].astype(BF16), wgk[:, kl]) + bgk[:, kl]
        logf = jax.nn.log_sigmoid(gk) / GLA_GATE_NORM
        q = q_ref[rows, kl].astype(F32) * (GLA_K ** -0.5)
        _intra_slab(slot, part * hpp, q, k_ref[rows, kl].astype(F32), logf, GLA_K, GLA_V,
                    v_ref[rows, vl], qin_scr, kout_scr, dec_scr, o_scr)

    def state(g, slot, c):
        _state_chunk(g, slot, c, hpb, GLA_K, GLA_V, v_ref, g_ref, gain, qin_scr, kout_scr, dec_scr,
                     o_scr, st_scr, o_ref)

    _run_pipelined(q_ref.shape[0] // SLAB, hpb // hpp, intra, state, st_scr, s_ref)


def _gla_prompt(z, glr, wgk, bgk, gain, batch, seq, hpb):
    kw, vw = hpb * GLA_K, hpb * GLA_V
    q0 = (4 * HG_HEADS * HG_K) // kw
    k0 = q0 + (GLA_HEADS * GLA_K) // kw
    v0 = (4 * HG_HEADS * HG_K + 2 * GLA_HEADS * GLA_K) // vw
    g0 = v0 + (GLA_HEADS * GLA_V) // vw
    return pl.pallas_call(
        functools.partial(_gla_prompt_kernel, hpb=hpb),
        grid=(batch, GLA_HEADS // hpb),
        in_specs=[
            pl.BlockSpec((1, GLA_V), lambda b, g: (0, 0)),
            pl.BlockSpec((LANE, kw), lambda b, g: (0, g)),
            pl.BlockSpec((1, kw), lambda b, g: (0, g)),
            pl.BlockSpec((seq, LANE), lambda b, g: (b, 0)),
            pl.BlockSpec((seq, kw), lambda b, g: (b, q0 + g)),
            pl.BlockSpec((seq, kw), lambda b, g: (b, k0 + g)),
            pl.BlockSpec((seq, vw), lambda b, g: (b, v0 + g)),
            pl.BlockSpec((seq, vw), lambda b, g: (b, g0 + g)),
        ],
        out_specs=[
            pl.BlockSpec((seq, vw), lambda b, g: (b, g)),
            pl.BlockSpec((1, hpb, GLA_K, GLA_V), lambda b, g: (b, g, 0, 0)),
        ],
        out_shape=[jax.ShapeDtypeStruct((batch * seq, GLA_HEADS * GLA_V), BF16),
                   jax.ShapeDtypeStruct((batch, GLA_HEADS, GLA_K, GLA_V), F32)],
        scratch_shapes=[pltpu.VMEM((2, SLAB, kw), BF16), pltpu.VMEM((2, SLAB, kw), BF16),
                        pltpu.VMEM((2, DEC_ROWS, kw), F32), pltpu.VMEM((2, SLAB, vw), F32),
                        pltpu.VMEM((hpb, GLA_K, GLA_V), F32)],
        compiler_params=_params(2),
        name="gla_prompt",
    )(gain, wgk, bgk, glr, z, z, z, z)


def _seg_pick(x, seg, idx):
    n = x.shape[0]
    rowmod = lax.broadcasted_iota(jnp.int32, x.shape, 0) % seg
    out = x
    for m in range(seg):
        if m != idx:
            out = jnp.where(rowmod == m, pltpu.roll(x, (m - idx) % n, axis=0), out)
    return out


def _sample_branch(q, k, v_b, logf, g, gain, s_ref, ns_ref, o_ref, o_scr, *, heads, kdim, vdim,
                   seq, pairs):
    tile = 2 * seq
    cum = _cumsum_rows(logf, seq)
    ref = _seg_pick(cum, seq, seq // 2)
    last = _seg_pick(cum, seq, seq - 1)
    qe = (q * jnp.exp(cum - ref)).astype(BF16)
    ke = (k * jnp.exp(ref - cum)).astype(BF16)
    q_in = (q * jnp.exp(cum)).astype(BF16)
    k_out = k * jnp.exp(last - cum)
    dec = jnp.exp(last)
    r_i = lax.broadcasted_iota(jnp.int32, (tile, tile), 0)
    c_i = lax.broadcasted_iota(jnp.int32, (tile, tile), 1)
    amask = (r_i // seq == c_i // seq) & (c_i <= r_i)
    half = lax.broadcasted_iota(jnp.int32, (tile, 1), 0) // seq
    for p in range(pairs):
        rs = slice(p * tile, (p + 1) * tile)
        for h in range(heads):
            ks = slice(h * kdim, (h + 1) * kdim)
            vs = slice(h * vdim, (h + 1) * vdim)
            v_t = v_b[rs, vs]
            att = lax.dot_general(qe[rs, ks], ke[rs, ks], NT, preferred_element_type=F32)
            o = _dot(jnp.where(amask, att, 0.0).astype(BF16), v_t)
            for bi in range(2):
                b = 2 * p + bi
                s0 = s_ref[0, b, h]
                o = o + jnp.where(half == bi, _dot(q_in[rs, ks], s0.astype(BF16)), 0.0)
                km = jnp.where(half == bi, k_out[rs, ks], 0.0).astype(BF16)
                kv = lax.dot_general(km, v_t, TN, preferred_element_type=F32)
                r0 = p * tile + bi * seq
                dcol = _col_broadcast(dec[r0:r0 + 1, ks])
                if vdim != LANE:
                    dcol = jnp.concatenate([dcol] * (vdim // LANE), axis=1)
                ns_ref[0, b, h] = s0 * dcol + kv
            o_scr[rs, vs] = o
    for h in range(heads):
        vs = slice(h * vdim, (h + 1) * vdim)
        o_ref[:, vs] = _gated_out(o_scr[:, vs], g[:, vs].astype(F32), gain)


def _sample_kernel(lb_ref, hgain_ref, ggain_ref, wgk_ref, bgk_ref,
                   hq_ref, hf_ref, hi_ref, hg_ref, gq_ref, gk_ref, gv_ref, gg_ref, glr_ref,
                   sh_ref, sg_ref,
                   a_ref, b_ref, nsh_ref, nsg_ref, oh_scr, og_scr, *, seq, pairs):
    lb = _forget_lower_bound(lb_ref[...])
    sig = jax.nn.sigmoid(hf_ref[...].astype(F32))
    _sample_branch(
        jax.nn.silu(hq_ref[...].astype(F32)) * (HG_K ** -0.5),
        (1.0 - lb) * (1.0 - sig),
        hi_ref[...],
        jnp.log(lb + (1.0 - lb) * sig),
        hg_ref[...], hgain_ref[...], sh_ref, nsh_ref, a_ref, oh_scr,
        heads=HG_HEADS, kdim=HG_K, vdim=HG_V, seq=seq, pairs=pairs)
    gk = _dot(glr_ref[...].astype(BF16), wgk_ref[...].astype(BF16)) + bgk_ref[...]
    _sample_branch(
        gq_ref[...].astype(F32) * (GLA_K ** -0.5),
        gk_ref[...].astype(F32),
        gv_ref[...],
        jax.nn.log_sigmoid(gk) / GLA_GATE_NORM,
        gg_ref[...], ggain_ref[...], sg_ref, nsg_ref, b_ref, og_scr,
        heads=GLA_HEADS, kdim=GLA_K, vdim=GLA_V, seq=seq, pairs=pairs)


def _sample_scan(z, glr, state_hgrn, state_gla, hg_lb, hgain, ggain, wgk, bgk, row0, nb, seq, bb):
    rows = bb * seq
    rb0 = row0 // rows
    hw, kw, vw = HG_HEADS * HG_K, GLA_HEADS * GLA_K, GLA_HEADS * GLA_V
    gq0 = (4 * hw) // kw
    gv0 = (4 * hw + 2 * kw) // vw
    zs = lambda w, c: pl.BlockSpec((rows, w), lambda i, c=c: (rb0 + i, c))
    const = lambda shape: pl.BlockSpec(shape, lambda i: (0,) * len(shape))
    st = lambda h, k, v: pl.BlockSpec((1, bb, h, k, v), lambda i: (0, i, 0, 0, 0))
    return pl.pallas_call(
        functools.partial(_sample_kernel, seq=seq, pairs=bb // 2),
        grid=(nb // bb,),
        in_specs=[
            const(hg_lb.shape), const((1, HG_V)), const((1, GLA_V)), const((LANE, kw)),
            const((1, kw)),
            zs(hw, 0), zs(hw, 1), zs(hw, 2), zs(hw, 3),
            zs(kw, gq0), zs(kw, gq0 + 1), zs(vw, gv0), zs(vw, gv0 + 1),
            pl.BlockSpec((rows, LANE), lambda i: (rb0 + i, 0)),
            st(HG_HEADS, HG_K, HG_V), st(GLA_HEADS, GLA_K, GLA_V),
        ],
        out_specs=[
            pl.BlockSpec((rows, hw), lambda i: (i, 0)),
            pl.BlockSpec((rows, vw), lambda i: (i, 0)),
            st(HG_HEADS, HG_K, HG_V), st(GLA_HEADS, GLA_K, GLA_V),
        ],
        out_shape=[jax.ShapeDtypeStruct((nb * seq, hw), BF16),
                   jax.ShapeDtypeStruct((nb * seq, vw), BF16),
                   jax.ShapeDtypeStruct(state_hgrn.shape, state_hgrn.dtype),
                   jax.ShapeDtypeStruct(state_gla.shape, state_gla.dtype)],
        scratch_shapes=[pltpu.VMEM((rows, hw), F32), pltpu.VMEM((rows, vw), F32)],
        compiler_params=_params(1),
        name="sample_scan",
    )(hg_lb, hgain, ggain, wgk, bgk, z, z, z, z, z, z, z, z, glr, state_hgrn, state_gla)


def _merge_kernel(h_ref, ap_ref, as_ref, bp_ref, bs_ref, wg0_ref, wx0_ref, wg1_ref, wx1_ref,
                  wb0_ref, wb1_ref, *rest, n_prompt_tiles, n_casts):
    cast_in, (o_ref, *cast_out), (wg_scr, wb_scr) = (
        rest[:n_casts], rest[n_casts:2 * n_casts + 1], rest[2 * n_casts + 1:])
    tn = o_ref.shape[1]
    is_prompt = pl.program_id(1) < n_prompt_tiles
    for src, dst in zip(cast_in, cast_out):
        dst[...] = src[...].astype(BF16)

    @pl.when(pl.program_id(1) == 0)
    def _():
        for n, (wg_ref, wx_ref) in enumerate(((wg0_ref, wx0_ref), (wg1_ref, wx1_ref))):
            cat = jnp.concatenate([wg_ref[...], wx_ref[...]], axis=0)
            _stage_transposed(cat[GLA_GATE_RANK:GLA_GATE_RANK + tn, :], wg_scr, n)
        wb_scr[0] = wb0_ref[0].astype(BF16)
        wb_scr[1] = wb1_ref[0].astype(BF16)

    h = h_ref[...]
    a = jnp.where(is_prompt, ap_ref[...], as_ref[...])
    b = jnp.where(is_prompt, bp_ref[...], bs_ref[...])
    for c0 in range(0, tn, MERGE_COLS):
        cs = slice(c0, c0 + MERGE_COLS)
        m = jax.nn.sigmoid(_dot(h, wg_scr[0, :, cs])) * _dot(a, wb_scr[0, :, cs])
        m = m + jax.nn.sigmoid(_dot(h, wg_scr[1, :, cs])) * _dot(b, wb_scr[1, :, cs])
        o_ref[:, cs] = m.astype(o_ref.dtype)


def _merge(h1, a_p, a_s, b_p, b_s, w_in_t, w_branch, to_cast, tm, tn):
    m, d = h1.shape
    bw = a_p.shape[1]
    npt = a_p.shape[0] // tm
    rk = GLA_GATE_RANK
    n_i = m // tm
    n_steps = (d // tn) * n_i
    cast_specs = []
    for w, slab in to_cast:
        n_slabs = w.shape[0] // slab
        assert w.shape[0] % slab == 0 and n_slabs <= n_steps
        cast_specs.append(pl.BlockSpec(
            (slab, w.shape[1]),
            lambda j, i, n_slabs=n_slabs: (jnp.minimum(j * n_i + i, n_slabs - 1), 0)))
    pspec = pl.BlockSpec((tm, bw), lambda j, i: (jnp.minimum(i, npt - 1), 0))
    sspec = pl.BlockSpec((tm, bw), lambda j, i: (jnp.maximum(i - npt, 0), 0))
    main = lambda n: pl.BlockSpec((tn, d), lambda j, i, n=n: ((MAIN_COLS + n * d) // tn + j, 0))
    extra = lambda n: pl.BlockSpec(
        (rk, d), lambda j, i, n=n: ((MAIN_COLS + n * d) // rk + (j + 1) * (tn // rk), 0))
    return pl.pallas_call(
        functools.partial(_merge_kernel, n_prompt_tiles=npt, n_casts=len(to_cast)),
        grid=(d // tn, n_i),
        in_specs=[
            pl.BlockSpec((tm, d), lambda j, i: (i, 0)),
            pspec, sspec, pspec, sspec,
            main(0), extra(0), main(1), extra(1),
            pl.BlockSpec((1, bw, tn), lambda j, i: (0, 0, j)),
            pl.BlockSpec((1, bw, tn), lambda j, i: (1, 0, j)),
        ] + cast_specs,
        out_specs=[pl.BlockSpec((tm, tn), lambda j, i: (i, j))] + cast_specs,
        out_shape=[jax.ShapeDtypeStruct((m, d), BF16)]
        + [jax.ShapeDtypeStruct(w.shape, BF16) for w, _ in to_cast],
        scratch_shapes=[pltpu.VMEM((2, d, tn), BF16), pltpu.VMEM((2, bw, tn), BF16)],
        compiler_params=_params(2),
        name="merge",
    )(h1, a_p, a_s, b_p, b_s, w_in_t, w_in_t, w_in_t, w_in_t, w_branch, w_branch,
      *[w for w, _ in to_cast])


def _out_proj_kernel(m_ref, xp_ref, xs_ref, w_ref, ln_ref, x1_ref, h2_ref, *, n_prompt_tiles):
    is_prompt = pl.program_id(0) < n_prompt_tiles
    m = m_ref[...]
    tm, d = x1_ref.shape
    ss = jnp.zeros((tm, 1), F32)
    for c0 in range(0, d, EPILOGUE_COLS):
        cs = slice(c0, c0 + EPILOGUE_COLS)
        x1 = jnp.where(is_prompt, xp_ref[:, cs], xs_ref[:, cs]) + _dot(m, w_ref[:, cs])
        x1_ref[:, cs] = x1
        ss = ss + jnp.sum(x1 * x1, axis=-1, keepdims=True)
    inv = lax.rsqrt(ss * (1.0 / d) + EPS)
    for c0 in range(0, d, EPILOGUE_COLS):
        cs = slice(c0, c0 + EPILOGUE_COLS)
        h2_ref[:, cs] = (x1_ref[:, cs] * inv * ln_ref[:, cs]).astype(BF16)


def _out_proj(merged, xp, xs, w_out, ln2, tm):
    m, d = merged.shape
    npt = xp.shape[0] // tm
    return pl.pallas_call(
        functools.partial(_out_proj_kernel, n_prompt_tiles=npt),
        grid=(m // tm,),
        in_specs=[
            pl.BlockSpec((tm, d), lambda i: (i, 0)),
            pl.BlockSpec((tm, d), lambda i: (jnp.minimum(i, npt - 1), 0)),
            pl.BlockSpec((tm, d), lambda i: (jnp.maximum(i - npt, 0), 0)),
            pl.BlockSpec((d, d), lambda i: (0, 0), pipeline_mode=pl.Buffered(1)),
            pl.BlockSpec((1, d), lambda i: (0, 0)),
        ],
        out_specs=[pl.BlockSpec((tm, d), lambda i: (i, 0)),
                   pl.BlockSpec((tm, d), lambda i: (i, 0))],
        out_shape=[jax.ShapeDtypeStruct((m, d), F32), jax.ShapeDtypeStruct((m, d), BF16)],
        compiler_params=_params(1),
        name="out_proj",
    )(merged, xp, xs, w_out, ln2)


def _ffn_up_kernel(h_ref, wg_ref, wu_ref, o_ref, w_scr):
    @pl.when(pl.program_id(1) == 0)
    def _():
        w_scr[0] = wg_ref[...].astype(BF16)
        w_scr[1] = wu_ref[...].astype(BF16)

    h = h_ref[...]
    for c0 in range(0, o_ref.shape[1], MERGE_COLS):
        cs = slice(c0, c0 + MERGE_COLS)
        gate = _dot(h, w_scr[0, :, cs])
        o_ref[:, cs] = (jax.nn.silu(gate) * _dot(h, w_scr[1, :, cs])).astype(o_ref.dtype)


def _ffn_up(h2, w_gu, tm, tf):
    m, d = h2.shape
    dff = w_gu.shape[1] // 2
    nj = dff // tf
    return pl.pallas_call(
        _ffn_up_kernel,
        grid=(nj, m // tm),
        in_specs=[
            pl.BlockSpec((tm, d), lambda j, i: (i, 0)),
            pl.BlockSpec((d, tf), lambda j, i: (0, j)),
            pl.BlockSpec((d, tf), lambda j, i: (0, nj + j)),
        ],
        out_specs=pl.BlockSpec((tm, tf), lambda j, i: (i, j)),
        out_shape=jax.ShapeDtypeStruct((m, dff), BF16),
        scratch_shapes=[pltpu.VMEM((2, d, tf), BF16)],
        compiler_params=_params(2),
        name="ffn_up",
    )(h2, w_gu, w_gu)


def _ffn_down_kernel(a_ref, w_ref, x_ref, o_ref):
    a = a_ref[...]
    for c0 in range(0, o_ref.shape[1], MERGE_COLS):
        cs = slice(c0, c0 + MERGE_COLS)
        o_ref[:, cs] = x_ref[:, cs] + _dot(a, w_ref[:, cs])


def _ffn_down(act, w_down, x1, tm, tn):
    m, dff = act.shape
    d = w_down.shape[1]
    return pl.pallas_call(
        _ffn_down_kernel,
        grid=(d // tn, m // tm),
        in_specs=[
            pl.BlockSpec((tm, dff), lambda j, i: (i, 0)),
            pl.BlockSpec((dff, tn), lambda j, i: (0, j)),
            pl.BlockSpec((tm, tn), lambda j, i: (i, j)),
        ],
        out_specs=pl.BlockSpec((tm, tn), lambda j, i: (i, j)),
        out_shape=jax.ShapeDtypeStruct((m, d), F32),
        compiler_params=_params(2),
        name="ffn_down",
    )(act, w_down, x1)


def _final_kernel(x_ref, pp_ref, ps_ref, wpg_ref, wple_ref, ln3_ref, lnf_ref, yp_ref, ys_ref,
                  x3_scr, *, n_prompt_tiles):
    i = pl.program_id(0)
    is_prompt = i < n_prompt_tiles
    tm, d = x_ref.shape
    h3 = _rms(x_ref[...], ln3_ref[...]).astype(BF16)
    p = jnp.where(is_prompt, pp_ref[...], ps_ref[...]).astype(BF16)
    ss = jnp.zeros((tm, 1), F32)
    for c0 in range(0, d, EPILOGUE_COLS):
        cs = slice(c0, c0 + EPILOGUE_COLS)
        gate = jax.nn.sigmoid(_dot(h3, wpg_ref[:, cs]))
        x3 = x_ref[:, cs] + gate * _dot(p, wple_ref[:, cs])
        x3_scr[:, cs] = x3
        ss = ss + jnp.sum(x3 * x3, axis=-1, keepdims=True)
    inv = lax.rsqrt(ss * (1.0 / d) + EPS)

    def write(y_ref):
        for c0 in range(0, d, EPILOGUE_COLS):
            cs = slice(c0, c0 + EPILOGUE_COLS)
            y_ref[:, cs] = x3_scr[:, cs] * inv * lnf_ref[:, cs]

    @pl.when(is_prompt)
    def _():
        write(yp_ref)

    @pl.when(jnp.logical_not(is_prompt))
    def _():
        write(ys_ref)


def _final(x2, pp, ps, w_pg, w_ple, ln3, ln_f, tm):
    m, d = x2.shape
    npr, ns = pp.shape[0], ps.shape[0]
    pd = pp.shape[1]
    npt = npr // tm
    pidx = lambda i: (jnp.minimum(i, npt - 1), 0)
    sidx = lambda i: (jnp.maximum(i - npt, 0), 0)
    return pl.pallas_call(
        functools.partial(_final_kernel, n_prompt_tiles=npt),
        grid=(m // tm,),
        in_specs=[
            pl.BlockSpec((tm, d), lambda i: (i, 0)),
            pl.BlockSpec((tm, pd), pidx),
            pl.BlockSpec((tm, pd), sidx),
            pl.BlockSpec((d, d), lambda i: (0, 0), pipeline_mode=pl.Buffered(1)),
            pl.BlockSpec((pd, d), lambda i: (0, 0)),
            pl.BlockSpec((1, d), lambda i: (0, 0)),
            pl.BlockSpec((1, d), lambda i: (0, 0)),
        ],
        out_specs=[pl.BlockSpec((tm, d), pidx), pl.BlockSpec((tm, d), sidx)],
        out_shape=[jax.ShapeDtypeStruct((npr, d), F32), jax.ShapeDtypeStruct((ns, d), F32)],
        scratch_shapes=[pltpu.VMEM((tm, d), F32)],
        compiler_params=_params(1),
        name="final",
    )(x2, pp, ps, w_pg, w_ple, ln3, ln_f)


def kernel(x_prompt, x_sample, state_hgrn, state_gla, p_prompt, p_sample, hg_lb, ln1, w_in, hg_norm,
           gla_w_gk, gla_b_gk, gla_norm, w_branch, w_out, ln2, w_gu, w_down, ln3, w_ple, w_pg, ln_f):
    batch, seq, d = x_prompt.shape
    nb, dseq, _ = x_sample.shape
    depth = w_in.shape[0]
    assert depth == 1, "single-layer step"
    npr, ns = batch * seq, nb * dseq
    xp = x_prompt.reshape(npr, d)
    xs = x_sample.reshape(ns, d)
    row = lambda v: v.reshape(1, -1)

    w_in0 = jnp.swapaxes(w_in[0], 0, 1)
    wgk = jnp.pad(gla_w_gk[0], ((0, LANE - GLA_GATE_RANK), (0, 0)))
    bgk = row(gla_b_gk[0])

    h1, glr = _prep(xp, xs, row(ln1[0]), w_in0, tm=512)
    z = _in_proj(h1, w_in0, tm=2176, tn=1024)

    a_p, s_hp = _hgrn_prompt(z, hg_lb, row(hg_norm[0]), batch, seq, hpb=8)
    b_p, s_gp = _gla_prompt(z, glr, wgk, bgk, row(gla_norm[0]), batch, seq, hpb=4)
    a_s, b_s, s_hs, s_gs = _sample_scan(z, glr, state_hgrn, state_gla, hg_lb, row(hg_norm[0]),
                                        row(gla_norm[0]), wgk, bgk, row0=npr, nb=nb, seq=dseq, bb=8)

    merged, w_out_b, w_pg_b, w_ple_b, w_down_b = _merge(
        h1, a_p, a_s, b_p, b_s, w_in0, w_branch[0],
        to_cast=[(w_out[0], 32), (w_pg[0], 32), (w_ple[0], 32), (w_down[0], 128)], tm=512, tn=512)
    x1, h2 = _out_proj(merged, xp, xs, w_out_b, row(ln2[0]), tm=512)
    act = _ffn_up(h2, w_gu[0], tm=2176, tf=512)
    x2 = _ffn_down(act, w_down_b, x1, tm=544, tn=1024)
    yp, ys = _final(x2, p_prompt[0].reshape(npr, -1), p_sample[0].reshape(ns, -1),
                    w_pg_b, w_ple_b, row(ln3[0]), row(ln_f), tm=512)

    return (yp.reshape(batch, seq, d), ys.reshape(nb, dseq, d),
            s_hp[None], s_gp[None], s_hs, s_gs)
```

```python
import functools

import jax
import jax.numpy as jnp
from jax import lax
from jax.experimental import pallas as pl
from jax.experimental.pallas import tpu as pltpu

F32 = jnp.float32
BF16 = jnp.bfloat16

EPS = 1e-6
LANE = 128
HG_HEADS, HG_K, HG_V = 8, 128, 128
GLA_HEADS, GLA_K, GLA_V = 4, 128, 256
GLA_GATE_RANK = 16
GLA_GATE_NORM = 16.0
CHUNK = 32
MAIN_COLS = 7168
VMEM_LIMIT = 56 * 1024 * 1024
VMEM_LIMIT_LARGE = 60 * 1024 * 1024
EPILOGUE_COLS = 256
MERGE_COLS = 256

NT = (((1,), (1,)), ((), ()))
TN = (((0,), (0,)), ((), ()))


def _params(n_axes, vmem=VMEM_LIMIT):
    return pltpu.CompilerParams(dimension_semantics=("arbitrary",) * n_axes,
                                vmem_limit_bytes=vmem)


def _dot(a, b):
    return jnp.dot(a, b, preferred_element_type=F32)


def _rms(x, g):
    return x * lax.rsqrt(jnp.mean(x * x, axis=-1, keepdims=True) + EPS) * g


def _prep_kernel(xp_ref, xs_ref, ln_ref, wglr_ref, h_ref, glr_ref, *, n_prompt_tiles):
    i = pl.program_id(0)

    d = wglr_ref.shape[1]
    wglr = jnp.concatenate([wglr_ref[...], jnp.zeros((LANE - GLA_GATE_RANK, d), F32)], axis=0)
    wglr = wglr.astype(BF16)

    def body(x):
        h = _rms(x, ln_ref[...]).astype(BF16)
        h_ref[...] = h
        glr_ref[...] = lax.dot_general(h, wglr, NT, preferred_element_type=F32)

    @pl.when(i < n_prompt_tiles)
    def _():
        body(xp_ref[...])

    @pl.when(i >= n_prompt_tiles)
    def _():
        body(xs_ref[...])


def _prep(xp, xs, ln1, w_in_t, tm):
    npr, d = xp.shape
    ns = xs.shape[0]
    npt, nst = npr // tm, ns // tm
    m = npr + ns
    return pl.pallas_call(
        functools.partial(_prep_kernel, n_prompt_tiles=npt),
        grid=(npt + nst,),
        in_specs=[
            pl.BlockSpec((tm, d), lambda i: (jnp.minimum(i, npt - 1), 0)),
            pl.BlockSpec((tm, d), lambda i: (jnp.maximum(i - npt, 0), 0)),
            pl.BlockSpec((1, d), lambda i: (0, 0)),
            pl.BlockSpec((GLA_GATE_RANK, d), lambda i: (MAIN_COLS // GLA_GATE_RANK, 0)),
        ],
        out_specs=[
            pl.BlockSpec((tm, d), lambda i: (i, 0)),
            pl.BlockSpec((tm, LANE), lambda i: (i, 0)),
        ],
        out_shape=[jax.ShapeDtypeStruct((m, d), BF16),
                   jax.ShapeDtypeStruct((m, LANE), F32)],
        compiler_params=_params(1),
        name="prep",
    )(xp, xs, ln1, w_in_t)


TRANSPOSE_ROWS = 256


def _stage_transposed(w_t, dst_ref, n=None):
    for r in range(0, w_t.shape[0], TRANSPOSE_ROWS):
        blk = w_t[r:r + TRANSPOSE_ROWS, :].T.astype(BF16)
        if n is None:
            dst_ref[:, r:r + TRANSPOSE_ROWS] = blk
        else:
            dst_ref[n, :, r:r + TRANSPOSE_ROWS] = blk


def _in_proj_kernel(h_ref, w_ref, wx_ref, z_ref, wbf_ref, *, n_aligned):
    j = pl.program_id(0)
    stage = pl.program_id(1) == 0
    tn = z_ref.shape[1]

    @pl.when(stage & (j < n_aligned))
    def _():
        _stage_transposed(w_ref[...], wbf_ref)

    @pl.when(stage & (j >= n_aligned))
    def _():
        cat = jnp.concatenate([w_ref[...], wx_ref[...]], axis=0)
        _stage_transposed(cat[GLA_GATE_RANK:GLA_GATE_RANK + tn, :], wbf_ref)

    h = h_ref[...]
    for c0 in range(0, tn, MERGE_COLS):
        cs = slice(c0, c0 + MERGE_COLS)
        z_ref[:, cs] = _dot(h, wbf_ref[:, cs]).astype(z_ref.dtype)


def _in_proj(h1, w_in_t, n_cols, tm, tn):
    m, d = h1.shape
    rk = GLA_GATE_RANK
    return pl.pallas_call(
        functools.partial(_in_proj_kernel, n_aligned=MAIN_COLS // tn),
        grid=(n_cols // tn, m // tm),
        in_specs=[
            pl.BlockSpec((tm, d), lambda j, i: (i, 0)),
            pl.BlockSpec((tn, d), lambda j, i: (j, 0)),
            pl.BlockSpec((rk, d), lambda j, i: ((j + 1) * (tn // rk), 0)),
        ],
        out_specs=pl.BlockSpec((tm, tn), lambda j, i: (i, j)),
        out_shape=jax.ShapeDtypeStruct((m, n_cols), BF16),
        scratch_shapes=[pltpu.VMEM((d, tn), BF16)],
        compiler_params=_params(2, VMEM_LIMIT_LARGE),
        name="in_proj",
    )(h1, w_in_t, w_in_t)


def _cumsum_rows(x, length):
    row = lax.broadcasted_iota(jnp.int32, x.shape, 0) % length
    s = 1
    while s < length:
        x = x + jnp.where(row >= s, pltpu.roll(x, s, axis=0), 0.0)
        s *= 2
    return x


def _forget_lower_bound(lb_rows):
    mx = jnp.max(lb_rows, axis=0, keepdims=True)
    e = jnp.exp(lb_rows - mx)
    return e[0:1, :] / jnp.sum(e, axis=0, keepdims=True)


def _col_broadcast(row):
    return jnp.broadcast_to(row, (LANE, LANE)).T


def _gated_out(o, g, gain):
    return (_rms(o, gain) * jax.nn.silu(g)).astype(BF16)


SLAB = 256
CHUNKS_PER_SLAB = SLAB // CHUNK
ATT_BLOCK = SLAB
SUBLANES = 8
DEC_ROWS = -(-CHUNKS_PER_SLAB // SUBLANES) * SUBLANES
INTRA_PARTS = 1


def _slab_rows(g):
    return pl.ds(pl.multiple_of(g * SLAB, SLAB), SLAB)


def _per_chunk_rows(rows_1w):
    w = rows_1w[0].shape[1]
    return jnp.concatenate([jnp.broadcast_to(r, (CHUNK, w)) for r in rows_1w], axis=0)


def _intra_slab(slot, h0, q, k, logf, kdim, vdim, v_slab, qin_scr, kout_scr, dec_scr, o_scr):
    heads = q.shape[1] // kdim
    kl = slice(h0 * kdim, (h0 + heads) * kdim)
    cum = _cumsum_rows(logf, CHUNK)
    ref_rows = [cum[c * CHUNK + CHUNK // 2:c * CHUNK + CHUNK // 2 + 1, :]
                for c in range(CHUNKS_PER_SLAB)]
    last_rows = [cum[(c + 1) * CHUNK - 1:(c + 1) * CHUNK, :] for c in range(CHUNKS_PER_SLAB)]
    ref = _per_chunk_rows(ref_rows)
    qe = q * jnp.exp(cum - ref)
    ke = k * jnp.exp(ref - cum)
    qin_scr[slot, :, kl] = (qe * _per_chunk_rows([jnp.exp(r) for r in ref_rows])).astype(BF16)
    kout_scr[slot, :, kl] = (
        ke * _per_chunk_rows([jnp.exp(l - r) for l, r in zip(last_rows, ref_rows)])).astype(BF16)
    for c in range(CHUNKS_PER_SLAB):
        dec_scr[slot, c:c + 1, kl] = jnp.exp(last_rows[c])
    qe = qe.astype(BF16)
    ke = ke.astype(BF16)
    r_i = lax.broadcasted_iota(jnp.int32, (ATT_BLOCK, ATT_BLOCK), 0)
    c_i = lax.broadcasted_iota(jnp.int32, (ATT_BLOCK, ATT_BLOCK), 1)
    causal = (r_i // CHUNK == c_i // CHUNK) & (c_i <= r_i)
    for h in range(heads):
        ks = slice(h * kdim, (h + 1) * kdim)
        for r0 in range(0, SLAB, ATT_BLOCK):
            rs = slice(r0, r0 + ATT_BLOCK)
            att = lax.dot_general(qe[rs, ks], ke[rs, ks], NT, preferred_element_type=F32)
            o_scr[slot, rs, (h0 + h) * vdim:(h0 + h + 1) * vdim] = _dot(
                jnp.where(causal, att, 0.0).astype(BF16), v_slab[rs, h * vdim:(h + 1) * vdim])


def _state_chunk(g, slot, c, heads, kdim, vdim, v_ref, g_ref, gain, qin_scr, kout_scr, dec_scr,
                 o_scr, st_scr, o_ref):
    cr = slice(c * CHUNK, (c + 1) * CHUNK)
    rows = pl.ds(pl.multiple_of(g * SLAB + c * CHUNK, CHUNK), CHUNK)
    drow = dec_scr[slot, c:c + 1, :]
    for h in range(heads):
        ks = slice(h * kdim, (h + 1) * kdim)
        vs = slice(h * vdim, (h + 1) * vdim)
        st = st_scr[h]
        o = o_scr[slot, cr, vs] + _dot(qin_scr[slot, cr, ks], st.astype(BF16))
        kv = lax.dot_general(kout_scr[slot, cr, ks], v_ref[rows, vs], TN,
                             preferred_element_type=F32)
        dcol = _col_broadcast(drow[:, ks])
        if vdim != LANE:
            dcol = jnp.concatenate([dcol] * (vdim // LANE), axis=1)
        st_scr[h] = st * dcol + kv
        o_ref[rows, vs] = _gated_out(o, g_ref[rows, vs].astype(F32), gain)


def _run_pipelined(n_slabs, n_parts, intra, state, st_scr, s_ref):
    assert n_slabs % 2 == 0 and n_slabs >= 4

    def overlapped(g_intra, slot_intra, g_state, slot_state):
        for p in range(max(n_parts, CHUNKS_PER_SLAB)):
            if p < n_parts:
                intra(g_intra, slot_intra, p)
            if p < CHUNKS_PER_SLAB:
                state(g_state, slot_state, p)

    st_scr[...] = jnp.zeros_like(st_scr)
    for p in range(n_parts):
        intra(0, 0, p)

    def pair(i, carry):
        overlapped(2 * i + 1, 1, 2 * i, 0)
        overlapped(2 * i + 2, 0, 2 * i + 1, 1)
        return carry

    lax.fori_loop(0, n_slabs // 2 - 1, pair, 0)
    overlapped(n_slabs - 1, 1, n_slabs - 2, 0)
    for c in range(CHUNKS_PER_SLAB):
        state(n_slabs - 1, 1, c)
    for h in range(st_scr.shape[0]):
        s_ref[0, h] = st_scr[h]


def _hgrn_prompt_kernel(lb_ref, gain_ref, q_ref, f_ref, i_ref, g_ref, o_ref, s_ref,
                        qin_scr, kout_scr, dec_scr, o_scr, st_scr, *, hpb):
    lb = _forget_lower_bound(lb_ref[...])
    gain = gain_ref[...]

    hpp = hpb // INTRA_PARTS

    def intra(g, slot, part):
        rows = _slab_rows(g)
        ls = slice(part * hpp * HG_K, (part + 1) * hpp * HG_K)
        lbp = lb[:, ls]
        q = jax.nn.silu(q_ref[rows, ls].astype(F32)) * (HG_K ** -0.5)
        sig = jax.nn.sigmoid(f_ref[rows, ls].astype(F32))
        logf = jnp.log(lbp + (1.0 - lbp) * sig)
        k = (1.0 - lbp) * (1.0 - sig)
        _intra_slab(slot, part * hpp, q, k, logf, HG_K, HG_V, i_ref[rows, ls], qin_scr, kout_scr,
                    dec_scr, o_scr)

    def state(g, slot, c):
        _state_chunk(g, slot, c, hpb, HG_K, HG_V, i_ref, g_ref, gain, qin_scr, kout_scr, dec_scr,
                     o_scr, st_scr, o_ref)

    _run_pipelined(q_ref.shape[0] // SLAB, hpb // hpp, intra, state, st_scr, s_ref)


def _hgrn_prompt(z, hg_lb, gain, batch, seq, hpb):
    w = hpb * HG_K
    sec = (HG_HEADS * HG_K) // w
    zspec = lambda s: pl.BlockSpec((seq, w), lambda b, g, s=s: (b, s * sec + g))
    return pl.pallas_call(
        functools.partial(_hgrn_prompt_kernel, hpb=hpb),
        grid=(batch, HG_HEADS // hpb),
        in_specs=[
            pl.BlockSpec((hg_lb.shape[0], w), lambda b, g: (0, g)),
            pl.BlockSpec((1, HG_V), lambda b, g: (0, 0)),
            zspec(0), zspec(1), zspec(2), zspec(3),
        ],
        out_specs=[
            pl.BlockSpec((seq, w), lambda b, g: (b, g)),
            pl.BlockSpec((1, hpb, HG_K, HG_V), lambda b, g: (b, g, 0, 0)),
        ],
        out_shape=[jax.ShapeDtypeStruct((batch * seq, HG_HEADS * HG_V), BF16),
                   jax.ShapeDtypeStruct((batch, HG_HEADS, HG_K, HG_V), F32)],
        scratch_shapes=[pltpu.VMEM((2, SLAB, w), BF16), pltpu.VMEM((2, SLAB, w), BF16),
                        pltpu.VMEM((2, DEC_ROWS, w), F32), pltpu.VMEM((2, SLAB, w), F32),
                        pltpu.VMEM((hpb, HG_K, HG_V), F32)],
        compiler_params=_params(2),
        name="hgrn_prompt",
    )(hg_lb, gain, z, z, z, z)


def _gla_prompt_kernel(gain_ref, wgk_ref, bgk_ref, glr_ref, q_ref, k_ref, v_ref, g_ref,
                       o_ref, s_ref, qin_scr, kout_scr, dec_scr, o_scr, st_scr, *, hpb):
    gain = gain_ref[...]
    wgk = wgk_ref[...].astype(BF16)
    bgk = bgk_ref[...]

    hpp = hpb // INTRA_PARTS

    def intra(g, slot, part):
        rows = _slab_rows(g)
        kl = slice(part * hpp * GLA_K, (part + 1) * hpp * GLA_K)
        vl = slice(part * hpp * GLA_V, (part + 1) * hpp * GLA_V)
        gk = _dot(glr_ref[rows, :].astype(BF16), wgk[:, kl]) + bgk[:, kl]
        logf = jax.nn.log_sigmoid(gk) / GLA_GATE_NORM
        q = q_ref[rows, kl].astype(F32) * (GLA_K ** -0.5)
        _intra_slab(slot, part * hpp, q, k_ref[rows, kl].astype(F32), logf, GLA_K, GLA_V,
                    v_ref[rows, vl], qin_scr, kout_scr, dec_scr, o_scr)

    def state(g, slot, c):
        _state_chunk(g, slot, c, hpb, GLA_K, GLA_V, v_ref, g_ref, gain, qin_scr, kout_scr, dec_scr,
                     o_scr, st_scr, o_ref)

    _run_pipelined(q_ref.shape[0] // SLAB, hpb // hpp, intra, state, st_scr, s_ref)


def _gla_prompt(z, glr, wgk, bgk, gain, batch, seq, hpb):
    kw, vw = hpb * GLA_K, hpb * GLA_V
    q0 = (4 * HG_HEADS * HG_K) // kw
    k0 = q0 + (GLA_HEADS * GLA_K) // kw
    v0 = (4 * HG_HEADS * HG_K + 2 * GLA_HEADS * GLA_K) // vw
    g0 = v0 + (GLA_HEADS * GLA_V) // vw
    return pl.pallas_call(
        functools.partial(_gla_prompt_kernel, hpb=hpb),
        grid=(batch, GLA_HEADS // hpb),
        in_specs=[
            pl.BlockSpec((1, GLA_V), lambda b, g: (0, 0)),
            pl.BlockSpec((LANE, kw), lambda b, g: (0, g)),
            pl.BlockSpec((1, kw), lambda b, g: (0, g)),
            pl.BlockSpec((seq, LANE), lambda b, g: (b, 0)),
            pl.BlockSpec((seq, kw), lambda b, g: (b, q0 + g)),
            pl.BlockSpec((seq, kw), lambda b, g: (b, k0 + g)),
            pl.BlockSpec((seq, vw), lambda b, g: (b, v0 + g)),
            pl.BlockSpec((seq, vw), lambda b, g: (b, g0 + g)),
        ],
        out_specs=[
            pl.BlockSpec((seq, vw), lambda b, g: (b, g)),
            pl.BlockSpec((1, hpb, GLA_K, GLA_V), lambda b, g: (b, g, 0, 0)),
        ],
        out_shape=[jax.ShapeDtypeStruct((batch * seq, GLA_HEADS * GLA_V), BF16),
                   jax.ShapeDtypeStruct((batch, GLA_HEADS, GLA_K, GLA_V), F32)],
        scratch_shapes=[pltpu.VMEM((2, SLAB, kw), BF16), pltpu.VMEM((2, SLAB, kw), BF16),
                        pltpu.VMEM((2, DEC_ROWS, kw), F32), pltpu.VMEM((2, SLAB, vw), F32),
                        pltpu.VMEM((hpb, GLA_K, GLA_V), F32)],
        compiler_params=_params(2),
        name="gla_prompt",
    )(gain, wgk, bgk, glr, z, z, z, z)


def _seg_pick(x, seg, idx):
    n = x.shape[0]
    rowmod = lax.broadcasted_iota(jnp.int32, x.shape, 0) % seg
    out = x
    for m in range(seg):
        if m != idx:
            out = jnp.where(rowmod == m, pltpu.roll(x, (m - idx) % n, axis=0), out)
    return out


def _sample_branch(q, k, v_b, logf, g, gain, s_ref, ns_ref, o_ref, o_scr, *, heads, kdim, vdim,
                   seq, pairs):
    tile = 2 * seq
    cum = _cumsum_rows(logf, seq)
    ref = _seg_pick(cum, seq, seq // 2)
    last = _seg_pick(cum, seq, seq - 1)
    qe = (q * jnp.exp(cum - ref)).astype(BF16)
    ke = (k * jnp.exp(ref - cum)).astype(BF16)
    q_in = (q * jnp.exp(cum)).astype(BF16)
    k_out = k * jnp.exp(last - cum)
    dec = jnp.exp(last)
    r_i = lax.broadcasted_iota(jnp.int32, (tile, tile), 0)
    c_i = lax.broadcasted_iota(jnp.int32, (tile, tile), 1)
    amask = (r_i // seq == c_i // seq) & (c_i <= r_i)
    half = lax.broadcasted_iota(jnp.int32, (tile, 1), 0) // seq
    for p in range(pairs):
        rs = slice(p * tile, (p + 1) * tile)
        for h in range(heads):
            ks = slice(h * kdim, (h + 1) * kdim)
            vs = slice(h * vdim, (h + 1) * vdim)
            v_t = v_b[rs, vs]
            att = lax.dot_general(qe[rs, ks], ke[rs, ks], NT, preferred_element_type=F32)
            o = _dot(jnp.where(amask, att, 0.0).astype(BF16), v_t)
            for bi in range(2):
                b = 2 * p + bi
                s0 = s_ref[0, b, h]
                o = o + jnp.where(half == bi, _dot(q_in[rs, ks], s0.astype(BF16)), 0.0)
                km = jnp.where(half == bi, k_out[rs, ks], 0.0).astype(BF16)
                kv = lax.dot_general(km, v_t, TN, preferred_element_type=F32)
                r0 = p * tile + bi * seq
                dcol = _col_broadcast(dec[r0:r0 + 1, ks])
                if vdim != LANE:
                    dcol = jnp.concatenate([dcol] * (vdim // LANE), axis=1)
                ns_ref[0, b, h] = s0 * dcol + kv
            o_scr[rs, vs] = o
    for h in range(heads):
        vs = slice(h * vdim, (h + 1) * vdim)
        o_ref[:, vs] = _gated_out(o_scr[:, vs], g[:, vs].astype(F32), gain)


def _sample_kernel(lb_ref, hgain_ref, ggain_ref, wgk_ref, bgk_ref,
                   hq_ref, hf_ref, hi_ref, hg_ref, gq_ref, gk_ref, gv_ref, gg_ref, glr_ref,
                   sh_ref, sg_ref,
                   a_ref, b_ref, nsh_ref, nsg_ref, oh_scr, og_scr, *, seq, pairs):
    lb = _forget_lower_bound(lb_ref[...])
    sig = jax.nn.sigmoid(hf_ref[...].astype(F32))
    _sample_branch(
        jax.nn.silu(hq_ref[...].astype(F32)) * (HG_K ** -0.5),
        (1.0 - lb) * (1.0 - sig),
        hi_ref[...],
        jnp.log(lb + (1.0 - lb) * sig),
        hg_ref[...], hgain_ref[...], sh_ref, nsh_ref, a_ref, oh_scr,
        heads=HG_HEADS, kdim=HG_K, vdim=HG_V, seq=seq, pairs=pairs)
    gk = _dot(glr_ref[...].astype(BF16), wgk_ref[...].astype(BF16)) + bgk_ref[...]
    _sample_branch(
        gq_ref[...].astype(F32) * (GLA_K ** -0.5),
        gk_ref[...].astype(F32),
        gv_ref[...],
        jax.nn.log_sigmoid(gk) / GLA_GATE_NORM,
        gg_ref[...], ggain_ref[...], sg_ref, nsg_ref, b_ref, og_scr,
        heads=GLA_HEADS, kdim=GLA_K, vdim=GLA_V, seq=seq, pairs=pairs)


def _sample_scan(z, glr, state_hgrn, state_gla, hg_lb, hgain, ggain, wgk, bgk, row0, nb, seq, bb):
    rows = bb * seq
    rb0 = row0 // rows
    hw, kw, vw = HG_HEADS * HG_K, GLA_HEADS * GLA_K, GLA_HEADS * GLA_V
    gq0 = (4 * hw) // kw
    gv0 = (4 * hw + 2 * kw) // vw
    zs = lambda w, c: pl.BlockSpec((rows, w), lambda i, c=c: (rb0 + i, c))
    const = lambda shape: pl.BlockSpec(shape, lambda i: (0,) * len(shape))
    st = lambda h, k, v: pl.BlockSpec((1, bb, h, k, v), lambda i: (0, i, 0, 0, 0))
    return pl.pallas_call(
        functools.partial(_sample_kernel, seq=seq, pairs=bb // 2),
        grid=(nb // bb,),
        in_specs=[
            const(hg_lb.shape), const((1, HG_V)), const((1, GLA_V)), const((LANE, kw)),
            const((1, kw)),
            zs(hw, 0), zs(hw, 1), zs(hw, 2), zs(hw, 3),
            zs(kw, gq0), zs(kw, gq0 + 1), zs(vw, gv0), zs(vw, gv0 + 1),
            pl.BlockSpec((rows, LANE), lambda i: (rb0 + i, 0)),
            st(HG_HEADS, HG_K, HG_V), st(GLA_HEADS, GLA_K, GLA_V),
        ],
        out_specs=[
            pl.BlockSpec((rows, hw), lambda i: (i, 0)),
            pl.BlockSpec((rows, vw), lambda i: (i, 0)),
            st(HG_HEADS, HG_K, HG_V), st(GLA_HEADS, GLA_K, GLA_V),
        ],
        out_shape=[jax.ShapeDtypeStruct((nb * seq, hw), BF16),
                   jax.ShapeDtypeStruct((nb * seq, vw), BF16),
                   jax.ShapeDtypeStruct(state_hgrn.shape, state_hgrn.dtype),
                   jax.ShapeDtypeStruct(state_gla.shape, state_gla.dtype)],
        scratch_shapes=[pltpu.VMEM((rows, hw), F32), pltpu.VMEM((rows, vw), F32)],
        compiler_params=_params(1),
        name="sample_scan",
    )(hg_lb, hgain, ggain, wgk, bgk, z, z, z, z, z, z, z, z, glr, state_hgrn, state_gla)


def _merge_kernel(g0_ref, g1_ref, ap_ref, as_ref, bp_ref, bs_ref, wb0_ref, wb1_ref, *rest,
                  n_prompt_tiles, n_casts):
    cast_in, (o_ref, *cast_out), (wb_scr,) = (
        rest[:n_casts], rest[n_casts:2 * n_casts + 1], rest[2 * n_casts + 1:])
    tn = o_ref.shape[1]
    is_prompt = pl.program_id(1) < n_prompt_tiles

    @pl.when(pl.program_id(1) == 0)
    def _():
        wb_scr[0] = wb0_ref[0].astype(BF16)
        wb_scr[1] = wb1_ref[0].astype(BF16)

    a = jnp.where(is_prompt, ap_ref[...], as_ref[...])
    b = jnp.where(is_prompt, bp_ref[...], bs_ref[...])
    for c0 in range(0, tn, MERGE_COLS):
        cs = slice(c0, c0 + MERGE_COLS)
        m = jax.nn.sigmoid(g0_ref[:, cs].astype(F32)) * _dot(a, wb_scr[0, :, cs])
        m = m + jax.nn.sigmoid(g1_ref[:, cs].astype(F32)) * _dot(b, wb_scr[1, :, cs])
        o_ref[:, cs] = m.astype(o_ref.dtype)
    for src, dst in zip(cast_in, cast_out):
        dst[...] = src[...].astype(BF16)


def _merge(z, a_p, a_s, b_p, b_s, w_branch, to_cast, tm, tn):
    m = z.shape[0]
    d = w_branch.shape[2]
    bw = a_p.shape[1]
    npt = a_p.shape[0] // tm
    n_i = m // tm
    n_steps = (d // tn) * n_i
    cast_specs = []
    for w, slab in to_cast:
        n_slabs = w.shape[0] // slab
        assert w.shape[0] % slab == 0 and n_slabs <= n_steps
        cast_specs.append(pl.BlockSpec(
            (slab, w.shape[1]),
            lambda j, i, n_slabs=n_slabs: (jnp.minimum(j * n_i + i, n_slabs - 1), 0)))
    pspec = pl.BlockSpec((tm, bw), lambda j, i: (jnp.minimum(i, npt - 1), 0))
    sspec = pl.BlockSpec((tm, bw), lambda j, i: (jnp.maximum(i - npt, 0), 0))
    gate = lambda n: pl.BlockSpec((tm, tn), lambda j, i, n=n: (i, (MAIN_COLS + n * d) // tn + j))
    return pl.pallas_call(
        functools.partial(_merge_kernel, n_prompt_tiles=npt, n_casts=len(to_cast)),
        grid=(d // tn, n_i),
        in_specs=[
            gate(0), gate(1),
            pspec, sspec, pspec, sspec,
            pl.BlockSpec((1, bw, tn), lambda j, i: (0, 0, j)),
            pl.BlockSpec((1, bw, tn), lambda j, i: (1, 0, j)),
        ] + cast_specs,
        out_specs=[pl.BlockSpec((tm, tn), lambda j, i: (i, j))] + cast_specs,
        out_shape=[jax.ShapeDtypeStruct((m, d), BF16)]
        + [jax.ShapeDtypeStruct(w.shape, BF16) for w, _ in to_cast],
        scratch_shapes=[pltpu.VMEM((2, bw, tn), BF16)],
        compiler_params=_params(2),
        name="merge",
    )(z, z, a_p, a_s, b_p, b_s, w_branch, w_branch, *[w for w, _ in to_cast])


def _out_proj_kernel(m_ref, xp_ref, xs_ref, w_ref, ln_ref, x1_ref, h2_ref, *, n_prompt_tiles):
    is_prompt = pl.program_id(0) < n_prompt_tiles
    m = m_ref[...]
    tm, d = x1_ref.shape
    ss = jnp.zeros((tm, 1), F32)
    for c0 in range(0, d, EPILOGUE_COLS):
        cs = slice(c0, c0 + EPILOGUE_COLS)
        x1 = jnp.where(is_prompt, xp_ref[:, cs], xs_ref[:, cs]) + _dot(m, w_ref[:, cs])
        x1_ref[:, cs] = x1
        ss = ss + jnp.sum(x1 * x1, axis=-1, keepdims=True)
    inv = lax.rsqrt(ss * (1.0 / d) + EPS)
    for c0 in range(0, d, EPILOGUE_COLS):
        cs = slice(c0, c0 + EPILOGUE_COLS)
        h2_ref[:, cs] = (x1_ref[:, cs] * inv * ln_ref[:, cs]).astype(BF16)


def _out_proj(merged, xp, xs, w_out, ln2, tm):
    m, d = merged.shape
    npt = xp.shape[0] // tm
    return pl.pallas_call(
        functools.partial(_out_proj_kernel, n_prompt_tiles=npt),
        grid=(m // tm,),
        in_specs=[
            pl.BlockSpec((tm, d), lambda i: (i, 0)),
            pl.BlockSpec((tm, d), lambda i: (jnp.minimum(i, npt - 1), 0)),
            pl.BlockSpec((tm, d), lambda i: (jnp.maximum(i - npt, 0), 0)),
            pl.BlockSpec((d, d), lambda i: (0, 0), pipeline_mode=pl.Buffered(1)),
            pl.BlockSpec((1, d), lambda i: (0, 0)),
        ],
        out_specs=[pl.BlockSpec((tm, d), lambda i: (i, 0)),
                   pl.BlockSpec((tm, d), lambda i: (i, 0))],
        out_shape=[jax.ShapeDtypeStruct((m, d), F32), jax.ShapeDtypeStruct((m, d), BF16)],
        compiler_params=_params(1),
        name="out_proj",
    )(merged, xp, xs, w_out, ln2)


def _ffn_up_kernel(h_ref, wg_ref, wu_ref, o_ref, w_scr):
    @pl.when(pl.program_id(1) == 0)
    def _():
        w_scr[0] = wg_ref[...].astype(BF16)
        w_scr[1] = wu_ref[...].astype(BF16)

    h = h_ref[...]
    for c0 in range(0, o_ref.shape[1], MERGE_COLS):
        cs = slice(c0, c0 + MERGE_COLS)
        gate = _dot(h, w_scr[0, :, cs])
        o_ref[:, cs] = (jax.nn.silu(gate) * _dot(h, w_scr[1, :, cs])).astype(o_ref.dtype)


def _ffn_up(h2, w_gu, tm, tf):
    m, d = h2.shape
    dff = w_gu.shape[1] // 2
    nj = dff // tf
    return pl.pallas_call(
        _ffn_up_kernel,
        grid=(nj, m // tm),
        in_specs=[
            pl.BlockSpec((tm, d), lambda j, i: (i, 0)),
            pl.BlockSpec((d, tf), lambda j, i: (0, j)),
            pl.BlockSpec((d, tf), lambda j, i: (0, nj + j)),
        ],
        out_specs=pl.BlockSpec((tm, tf), lambda j, i: (i, j)),
        out_shape=jax.ShapeDtypeStruct((m, dff), BF16),
        scratch_shapes=[pltpu.VMEM((2, d, tf), BF16)],
        compiler_params=_params(2),
        name="ffn_up",
    )(h2, w_gu, w_gu)


def _ffn_down_kernel(a_ref, w_ref, x_ref, o_ref):
    a = a_ref[...]
    for c0 in range(0, o_ref.shape[1], MERGE_COLS):
        cs = slice(c0, c0 + MERGE_COLS)
        o_ref[:, cs] = x_ref[:, cs] + _dot(a, w_ref[:, cs])


def _ffn_down(act, w_down, x1, tm, tn):
    m, dff = act.shape
    d = w_down.shape[1]
    return pl.pallas_call(
        _ffn_down_kernel,
        grid=(d // tn, m // tm),
        in_specs=[
            pl.BlockSpec((tm, dff), lambda j, i: (i, 0)),
            pl.BlockSpec((dff, tn), lambda j, i: (0, j)),
            pl.BlockSpec((tm, tn), lambda j, i: (i, j)),
        ],
        out_specs=pl.BlockSpec((tm, tn), lambda j, i: (i, j)),
        out_shape=jax.ShapeDtypeStruct((m, d), F32),
        compiler_params=_params(2),
        name="ffn_down",
    )(act, w_down, x1)


def _final_kernel(x_ref, pp_ref, ps_ref, wpg_ref, wple_ref, ln3_ref, lnf_ref, yp_ref, ys_ref,
                  x3_scr, *, n_prompt_tiles):
    i = pl.program_id(0)
    is_prompt = i < n_prompt_tiles
    tm, d = x_ref.shape
    x2 = x_ref[...]
    h3 = (x2 * ln3_ref[...]).astype(BF16)
    inv3 = lax.rsqrt(jnp.mean(x2 * x2, axis=-1, keepdims=True) + EPS)
    p = jnp.where(is_prompt, pp_ref[...], ps_ref[...]).astype(BF16)
    ss = jnp.zeros((tm, 1), F32)
    for c0 in range(0, d, EPILOGUE_COLS):
        cs = slice(c0, c0 + EPILOGUE_COLS)
        gate = jax.nn.sigmoid(inv3 * _dot(h3, wpg_ref[:, cs]))
        x3 = x_ref[:, cs] + gate * _dot(p, wple_ref[:, cs])
        x3_scr[:, cs] = x3
        ss = ss + jnp.sum(x3 * x3, axis=-1, keepdims=True)
    inv = lax.rsqrt(ss * (1.0 / d) + EPS)

    def write(y_ref):
        for c0 in range(0, d, EPILOGUE_COLS):
            cs = slice(c0, c0 + EPILOGUE_COLS)
            y_ref[:, cs] = x3_scr[:, cs] * inv * lnf_ref[:, cs]

    @pl.when(is_prompt)
    def _():
        write(yp_ref)

    @pl.when(jnp.logical_not(is_prompt))
    def _():
        write(ys_ref)


def _final(x2, pp, ps, w_pg, w_ple, ln3, ln_f, tm):
    m, d = x2.shape
    npr, ns = pp.shape[0], ps.shape[0]
    pd = pp.shape[1]
    npt = npr // tm
    pidx = lambda i: (jnp.minimum(i, npt - 1), 0)
    sidx = lambda i: (jnp.maximum(i - npt, 0), 0)
    return pl.pallas_call(
        functools.partial(_final_kernel, n_prompt_tiles=npt),
        grid=(m // tm,),
        in_specs=[
            pl.BlockSpec((tm, d), lambda i: (i, 0)),
            pl.BlockSpec((tm, pd), pidx),
            pl.BlockSpec((tm, pd), sidx),
            pl.BlockSpec((d, d), lambda i: (0, 0), pipeline_mode=pl.Buffered(1)),
            pl.BlockSpec((pd, d), lambda i: (0, 0)),
            pl.BlockSpec((1, d), lambda i: (0, 0)),
            pl.BlockSpec((1, d), lambda i: (0, 0)),
        ],
        out_specs=[pl.BlockSpec((tm, d), pidx), pl.BlockSpec((tm, d), sidx)],
        out_shape=[jax.ShapeDtypeStruct((npr, d), F32), jax.ShapeDtypeStruct((ns, d), F32)],
        scratch_shapes=[pltpu.VMEM((tm, d), F32)],
        compiler_params=_params(1),
        name="final",
    )(x2, pp, ps, w_pg, w_ple, ln3, ln_f)


def kernel(x_prompt, x_sample, state_hgrn, state_gla, p_prompt, p_sample, hg_lb, ln1, w_in, hg_norm,
           gla_w_gk, gla_b_gk, gla_norm, w_branch, w_out, ln2, w_gu, w_down, ln3, w_ple, w_pg, ln_f):
    batch, seq, d = x_prompt.shape
    nb, dseq, _ = x_sample.shape
    depth = w_in.shape[0]
    assert depth == 1, "single-layer step"
    npr, ns = batch * seq, nb * dseq
    xp = x_prompt.reshape(npr, d)
    xs = x_sample.reshape(ns, d)
    row = lambda v: v.reshape(1, -1)

    w_in0 = jnp.swapaxes(w_in[0], 0, 1)
    wgk = jnp.pad(gla_w_gk[0], ((0, LANE - GLA_GATE_RANK), (0, 0)))
    bgk = row(gla_b_gk[0])

    h1, glr = _prep(xp, xs, row(ln1[0]), w_in0, tm=512)
    z = _in_proj(h1, w_in0, n_cols=MAIN_COLS + 2 * d, tm=2176, tn=1024)

    a_p, s_hp = _hgrn_prompt(z, hg_lb, row(hg_norm[0]), batch, seq, hpb=8)
    b_p, s_gp = _gla_prompt(z, glr, wgk, bgk, row(gla_norm[0]), batch, seq, hpb=4)
    a_s, b_s, s_hs, s_gs = _sample_scan(z, glr, state_hgrn, state_gla, hg_lb, row(hg_norm[0]),
                                        row(gla_norm[0]), wgk, bgk, row0=npr, nb=nb, seq=dseq, bb=8)

    merged, w_out_b, w_pg_b, w_ple_b, w_down_b = _merge(
        z, a_p, a_s, b_p, b_s, w_branch[0],
        to_cast=[(w_out[0], 64), (w_pg[0], 64), (w_ple[0], 32), (w_down[0], 176)], tm=512, tn=1024)
    x1, h2 = _out_proj(merged, xp, xs, w_out_b, row(ln2[0]), tm=512)
    act = _ffn_up(h2, w_gu[0], tm=2176, tf=512)
    x2 = _ffn_down(act, w_down_b, x1, tm=544, tn=1024)
    yp, ys = _final(x2, p_prompt[0].reshape(npr, -1), p_sample[0].reshape(ns, -1),
                    w_pg_b, w_ple_b, row(ln3[0]), row(ln_f), tm=512)

    return (yp.reshape(batch, seq, d), ys.reshape(nb, dseq, d),
            s_hp[None], s_gp[None], s_hs, s_gs)
```

```python
import functools

import jax
import jax.numpy as jnp
from jax import lax
from jax.experimental import pallas as pl
from jax.experimental.pallas import tpu as pltpu

F32 = jnp.float32
BF16 = jnp.bfloat16

EPS = 1e-6
LANE = 128
HG_HEADS, HG_K, HG_V = 8, 128, 128
GLA_HEADS, GLA_K, GLA_V = 4, 128, 256
GLA_GATE_RANK = 16
GLA_GATE_NORM = 16.0
CHUNK = 32
MAIN_COLS = 7168
VMEM_LIMIT = 56 * 1024 * 1024
VMEM_LIMIT_LARGE = 60 * 1024 * 1024
EPILOGUE_COLS = 256
MERGE_COLS = 256

NT = (((1,), (1,)), ((), ()))
TN = (((0,), (0,)), ((), ()))


def _params(n_axes, vmem=VMEM_LIMIT):
    return pltpu.CompilerParams(dimension_semantics=("arbitrary",) * n_axes,
                                vmem_limit_bytes=vmem)


def _dot(a, b):
    return jnp.dot(a, b, preferred_element_type=F32)


def _rms(x, g):
    return x * lax.rsqrt(jnp.mean(x * x, axis=-1, keepdims=True) + EPS) * g


def _prep_kernel(xp_ref, xs_ref, ln_ref, wglr_ref, h_ref, glr_ref, *, n_prompt_tiles):
    i = pl.program_id(0)

    d = wglr_ref.shape[1]
    wglr = jnp.concatenate([wglr_ref[...], jnp.zeros((LANE - GLA_GATE_RANK, d), F32)], axis=0)
    wglr = wglr.astype(BF16)

    def body(x):
        h = _rms(x, ln_ref[...]).astype(BF16)
        h_ref[...] = h
        glr_ref[...] = lax.dot_general(h, wglr, NT, preferred_element_type=F32)

    @pl.when(i < n_prompt_tiles)
    def _():
        body(xp_ref[...])

    @pl.when(i >= n_prompt_tiles)
    def _():
        body(xs_ref[...])


def _prep(xp, xs, ln1, w_in_t, tm):
    npr, d = xp.shape
    ns = xs.shape[0]
    npt, nst = npr // tm, ns // tm
    m = npr + ns
    return pl.pallas_call(
        functools.partial(_prep_kernel, n_prompt_tiles=npt),
        grid=(npt + nst,),
        in_specs=[
            pl.BlockSpec((tm, d), lambda i: (jnp.minimum(i, npt - 1), 0)),
            pl.BlockSpec((tm, d), lambda i: (jnp.maximum(i - npt, 0), 0)),
            pl.BlockSpec((1, d), lambda i: (0, 0)),
            pl.BlockSpec((GLA_GATE_RANK, d), lambda i: (MAIN_COLS // GLA_GATE_RANK, 0)),
        ],
        out_specs=[
            pl.BlockSpec((tm, d), lambda i: (i, 0)),
            pl.BlockSpec((tm, LANE), lambda i: (i, 0)),
        ],
        out_shape=[jax.ShapeDtypeStruct((m, d), BF16),
                   jax.ShapeDtypeStruct((m, LANE), F32)],
        compiler_params=_params(1),
        name="prep",
    )(xp, xs, ln1, w_in_t)


TRANSPOSE_ROWS = 256


def _stage_transposed(w_t, dst_ref, n=None):
    for r in range(0, w_t.shape[0], TRANSPOSE_ROWS):
        blk = w_t[r:r + TRANSPOSE_ROWS, :].T.astype(BF16)
        if n is None:
            dst_ref[:, r:r + TRANSPOSE_ROWS] = blk
        else:
            dst_ref[n, :, r:r + TRANSPOSE_ROWS] = blk


def _in_proj_kernel(h_ref, w_ref, wx_ref, z_ref, wbf_ref, *, n_aligned):
    j = pl.program_id(0)
    stage = pl.program_id(1) == 0
    tn = z_ref.shape[1]

    @pl.when(stage & (j < n_aligned))
    def _():
        _stage_transposed(w_ref[...], wbf_ref)

    @pl.when(stage & (j >= n_aligned))
    def _():
        cat = jnp.concatenate([w_ref[...], wx_ref[...]], axis=0)
        _stage_transposed(cat[GLA_GATE_RANK:GLA_GATE_RANK + tn, :], wbf_ref)

    h = h_ref[...]
    for c0 in range(0, tn, MERGE_COLS):
        cs = slice(c0, c0 + MERGE_COLS)
        z_ref[:, cs] = _dot(h, wbf_ref[:, cs]).astype(z_ref.dtype)


def _in_proj(h1, w_in_t, n_cols, tm, tn):
    m, d = h1.shape
    rk = GLA_GATE_RANK
    return pl.pallas_call(
        functools.partial(_in_proj_kernel, n_aligned=MAIN_COLS // tn),
        grid=(n_cols // tn, m // tm),
        in_specs=[
            pl.BlockSpec((tm, d), lambda j, i: (i, 0)),
            pl.BlockSpec((tn, d), lambda j, i: (j, 0)),
            pl.BlockSpec((rk, d), lambda j, i: ((j + 1) * (tn // rk), 0)),
        ],
        out_specs=pl.BlockSpec((tm, tn), lambda j, i: (i, j)),
        out_shape=jax.ShapeDtypeStruct((m, n_cols), BF16),
        scratch_shapes=[pltpu.VMEM((d, tn), BF16)],
        compiler_params=_params(2, VMEM_LIMIT_LARGE),
        name="in_proj",
    )(h1, w_in_t, w_in_t)


def _cumsum_rows(x, length):
    row = lax.broadcasted_iota(jnp.int32, x.shape, 0) % length
    s = 1
    while s < length:
        x = x + jnp.where(row >= s, pltpu.roll(x, s, axis=0), 0.0)
        s *= 2
    return x


def _forget_lower_bound(lb_rows):
    mx = jnp.max(lb_rows, axis=0, keepdims=True)
    e = jnp.exp(lb_rows - mx)
    return e[0:1, :] / jnp.sum(e, axis=0, keepdims=True)


def _col_broadcast(row):
    return jnp.broadcast_to(row, (LANE, LANE)).T


def _gated_out(o, g, gain):
    return (_rms(o, gain) * jax.nn.silu(g)).astype(BF16)


SLAB = 256
CHUNKS_PER_SLAB = SLAB // CHUNK
ATT_BLOCK = SLAB
SUBLANES = 8
DEC_ROWS = -(-CHUNKS_PER_SLAB // SUBLANES) * SUBLANES
INTRA_PARTS = 1


def _slab_rows(g):
    return pl.ds(pl.multiple_of(g * SLAB, SLAB), SLAB)


def _per_chunk_rows(rows_1w):
    w = rows_1w[0].shape[1]
    return jnp.concatenate([jnp.broadcast_to(r, (CHUNK, w)) for r in rows_1w], axis=0)


def _intra_slab(slot, h0, q, k, logf, kdim, vdim, v_slab, qin_scr, kout_scr, dec_scr, o_scr):
    heads = q.shape[1] // kdim
    kl = slice(h0 * kdim, (h0 + heads) * kdim)
    cum = _cumsum_rows(logf, CHUNK)
    ref_rows = [cum[c * CHUNK + CHUNK // 2:c * CHUNK + CHUNK // 2 + 1, :]
                for c in range(CHUNKS_PER_SLAB)]
    last_rows = [cum[(c + 1) * CHUNK - 1:(c + 1) * CHUNK, :] for c in range(CHUNKS_PER_SLAB)]
    ref = _per_chunk_rows(ref_rows)
    qe = q * jnp.exp(cum - ref)
    ke = k * jnp.exp(ref - cum)
    qin_scr[slot, :, kl] = (qe * _per_chunk_rows([jnp.exp(r) for r in ref_rows])).astype(BF16)
    kout_scr[slot, :, kl] = (
        ke * _per_chunk_rows([jnp.exp(l - r) for l, r in zip(last_rows, ref_rows)])).astype(BF16)
    for c in range(CHUNKS_PER_SLAB):
        dec_scr[slot, c:c + 1, kl] = jnp.exp(last_rows[c])
    qe = qe.astype(BF16)
    ke = ke.astype(BF16)
    r_i = lax.broadcasted_iota(jnp.int32, (ATT_BLOCK, ATT_BLOCK), 0)
    c_i = lax.broadcasted_iota(jnp.int32, (ATT_BLOCK, ATT_BLOCK), 1)
    causal = (r_i // CHUNK == c_i // CHUNK) & (c_i <= r_i)
    for h in range(heads):
        ks = slice(h * kdim, (h + 1) * kdim)
        for r0 in range(0, SLAB, ATT_BLOCK):
            rs = slice(r0, r0 + ATT_BLOCK)
            att = lax.dot_general(qe[rs, ks], ke[rs, ks], NT, preferred_element_type=F32)
            o_scr[slot, rs, (h0 + h) * vdim:(h0 + h + 1) * vdim] = _dot(
                jnp.where(causal, att, 0.0).astype(BF16), v_slab[rs, h * vdim:(h + 1) * vdim])


def _state_chunk(g, slot, c, heads, kdim, vdim, v_ref, g_ref, gain, qin_scr, kout_scr, dec_scr,
                 o_scr, st_scr, o_ref):
    cr = slice(c * CHUNK, (c + 1) * CHUNK)
    rows = pl.ds(pl.multiple_of(g * SLAB + c * CHUNK, CHUNK), CHUNK)
    drow = dec_scr[slot, c:c + 1, :]
    for h in range(heads):
        ks = slice(h * kdim, (h + 1) * kdim)
        vs = slice(h * vdim, (h + 1) * vdim)
        st = st_scr[h]
        o = o_scr[slot, cr, vs] + _dot(qin_scr[slot, cr, ks], st.astype(BF16))
        kv = lax.dot_general(kout_scr[slot, cr, ks], v_ref[rows, vs], TN,
                             preferred_element_type=F32)
        dcol = _col_broadcast(drow[:, ks])
        if vdim != LANE:
            dcol = jnp.concatenate([dcol] * (vdim // LANE), axis=1)
        st_scr[h] = st * dcol + kv
        o_ref[rows, vs] = _gated_out(o, g_ref[rows, vs].astype(F32), gain)


def _run_pipelined(n_slabs, n_parts, intra, state, st_scr, s_ref):
    assert n_slabs % 2 == 0 and n_slabs >= 4

    def overlapped(g_intra, slot_intra, g_state, slot_state):
        for p in range(max(n_parts, CHUNKS_PER_SLAB)):
            if p < n_parts:
                intra(g_intra, slot_intra, p)
            if p < CHUNKS_PER_SLAB:
                state(g_state, slot_state, p)

    st_scr[...] = jnp.zeros_like(st_scr)
    for p in range(n_parts):
        intra(0, 0, p)

    def pair(i, carry):
        overlapped(2 * i + 1, 1, 2 * i, 0)
        overlapped(2 * i + 2, 0, 2 * i + 1, 1)
        return carry

    lax.fori_loop(0, n_slabs // 2 - 1, pair, 0)
    overlapped(n_slabs - 1, 1, n_slabs - 2, 0)
    for c in range(CHUNKS_PER_SLAB):
        state(n_slabs - 1, 1, c)
    for h in range(st_scr.shape[0]):
        s_ref[0, h] = st_scr[h]


def _hgrn_prompt_kernel(lb_ref, gain_ref, q_ref, f_ref, i_ref, g_ref, o_ref, s_ref,
                        qin_scr, kout_scr, dec_scr, o_scr, st_scr, *, hpb):
    lb = _forget_lower_bound(lb_ref[...])
    gain = gain_ref[...]

    hpp = hpb // INTRA_PARTS

    def intra(g, slot, part):
        rows = _slab_rows(g)
        ls = slice(part * hpp * HG_K, (part + 1) * hpp * HG_K)
        lbp = lb[:, ls]
        q = jax.nn.silu(q_ref[rows, ls].astype(F32)) * (HG_K ** -0.5)
        sig = jax.nn.sigmoid(f_ref[rows, ls].astype(F32))
        logf = jnp.log(lbp + (1.0 - lbp) * sig)
        k = (1.0 - lbp) * (1.0 - sig)
        _intra_slab(slot, part * hpp, q, k, logf, HG_K, HG_V, i_ref[rows, ls], qin_scr, kout_scr,
                    dec_scr, o_scr)

    def state(g, slot, c):
        _state_chunk(g, slot, c, hpb, HG_K, HG_V, i_ref, g_ref, gain, qin_scr, kout_scr, dec_scr,
                     o_scr, st_scr, o_ref)

    _run_pipelined(q_ref.shape[0] // SLAB, hpb // hpp, intra, state, st_scr, s_ref)


def _hgrn_prompt(z, hg_lb, gain, batch, seq, hpb):
    w = hpb * HG_K
    sec = (HG_HEADS * HG_K) // w
    zspec = lambda s: pl.BlockSpec((seq, w), lambda b, g, s=s: (b, s * sec + g))
    return pl.pallas_call(
        functools.partial(_hgrn_prompt_kernel, hpb=hpb),
        grid=(batch, HG_HEADS // hpb),
        in_specs=[
            pl.BlockSpec((hg_lb.shape[0], w), lambda b, g: (0, g)),
            pl.BlockSpec((1, HG_V), lambda b, g: (0, 0)),
            zspec(0), zspec(1), zspec(2), zspec(3),
        ],
        out_specs=[
            pl.BlockSpec((seq, w), lambda b, g: (b, g)),
            pl.BlockSpec((1, hpb, HG_K, HG_V), lambda b, g: (b, g, 0, 0)),
        ],
        out_shape=[jax.ShapeDtypeStruct((batch * seq, HG_HEADS * HG_V), BF16),
                   jax.ShapeDtypeStruct((batch, HG_HEADS, HG_K, HG_V), F32)],
        scratch_shapes=[pltpu.VMEM((2, SLAB, w), BF16), pltpu.VMEM((2, SLAB, w), BF16),
                        pltpu.VMEM((2, DEC_ROWS, w), F32), pltpu.VMEM((2, SLAB, w), F32),
                        pltpu.VMEM((hpb, HG_K, HG_V), F32)],
        compiler_params=_params(2),
        name="hgrn_prompt",
    )(hg_lb, gain, z, z, z, z)


def _gla_prompt_kernel(gain_ref, wgk_ref, bgk_ref, glr_ref, q_ref, k_ref, v_ref, g_ref,
                       o_ref, s_ref, qin_scr, kout_scr, dec_scr, o_scr, st_scr, *, hpb):
    gain = gain_ref[...]
    wgk = wgk_ref[...].astype(BF16)
    bgk = bgk_ref[...]

    hpp = hpb // INTRA_PARTS

    def intra(g, slot, part):
        rows = _slab_rows(g)
        kl = slice(part * hpp * GLA_K, (part + 1) * hpp * GLA_K)
        vl = slice(part * hpp * GLA_V, (part + 1) * hpp * GLA_V)
        gk = _dot(glr_ref[rows, :].astype(BF16), wgk[:, kl]) + bgk[:, kl]
        logf = jax.nn.log_sigmoid(gk) / GLA_GATE_NORM
        q = q_ref[rows, kl].astype(F32) * (GLA_K ** -0.5)
        _intra_slab(slot, part * hpp, q, k_ref[rows, kl].astype(F32), logf, GLA_K, GLA_V,
                    v_ref[rows, vl], qin_scr, kout_scr, dec_scr, o_scr)

    def state(g, slot, c):
        _state_chunk(g, slot, c, hpb, GLA_K, GLA_V, v_ref, g_ref, gain, qin_scr, kout_scr, dec_scr,
                     o_scr, st_scr, o_ref)

    _run_pipelined(q_ref.shape[0] // SLAB, hpb // hpp, intra, state, st_scr, s_ref)


def _gla_prompt(z, glr, wgk, bgk, gain, batch, seq, hpb):
    kw, vw = hpb * GLA_K, hpb * GLA_V
    q0 = (4 * HG_HEADS * HG_K) // kw
    k0 = q0 + (GLA_HEADS * GLA_K) // kw
    v0 = (4 * HG_HEADS * HG_K + 2 * GLA_HEADS * GLA_K) // vw
    g0 = v0 + (GLA_HEADS * GLA_V) // vw
    return pl.pallas_call(
        functools.partial(_gla_prompt_kernel, hpb=hpb),
        grid=(batch, GLA_HEADS // hpb),
        in_specs=[
            pl.BlockSpec((1, GLA_V), lambda b, g: (0, 0)),
            pl.BlockSpec((LANE, kw), lambda b, g: (0, g)),
            pl.BlockSpec((1, kw), lambda b, g: (0, g)),
            pl.BlockSpec((seq, LANE), lambda b, g: (b, 0)),
            pl.BlockSpec((seq, kw), lambda b, g: (b, q0 + g)),
            pl.BlockSpec((seq, kw), lambda b, g: (b, k0 + g)),
            pl.BlockSpec((seq, vw), lambda b, g: (b, v0 + g)),
            pl.BlockSpec((seq, vw), lambda b, g: (b, g0 + g)),
        ],
        out_specs=[
            pl.BlockSpec((seq, vw), lambda b, g: (b, g)),
            pl.BlockSpec((1, hpb, GLA_K, GLA_V), lambda b, g: (b, g, 0, 0)),
        ],
        out_shape=[jax.ShapeDtypeStruct((batch * seq, GLA_HEADS * GLA_V), BF16),
                   jax.ShapeDtypeStruct((batch, GLA_HEADS, GLA_K, GLA_V), F32)],
        scratch_shapes=[pltpu.VMEM((2, SLAB, kw), BF16), pltpu.VMEM((2, SLAB, kw), BF16),
                        pltpu.VMEM((2, DEC_ROWS, kw), F32), pltpu.VMEM((2, SLAB, vw), F32),
                        pltpu.VMEM((hpb, GLA_K, GLA_V), F32)],
        compiler_params=_params(2),
        name="gla_prompt",
    )(gain, wgk, bgk, glr, z, z, z, z)


def _seg_pick(x, seg, idx):
    n = x.shape[0]
    rowmod = lax.broadcasted_iota(jnp.int32, x.shape, 0) % seg
    out = x
    for m in range(seg):
        if m != idx:
            out = jnp.where(rowmod == m, pltpu.roll(x, (m - idx) % n, axis=0), out)
    return out


def _sample_branch(q, k, v_b, logf, g, gain, s_ref, ns_ref, o_ref, o_scr, *, heads, kdim, vdim,
                   seq, pairs):
    tile = 2 * seq
    cum = _cumsum_rows(logf, seq)
    ref = _seg_pick(cum, seq, seq // 2)
    last = _seg_pick(cum, seq, seq - 1)
    qe = (q * jnp.exp(cum - ref)).astype(BF16)
    ke = (k * jnp.exp(ref - cum)).astype(BF16)
    q_in = (q * jnp.exp(cum)).astype(BF16)
    k_out = k * jnp.exp(last - cum)
    dec = jnp.exp(last)
    r_i = lax.broadcasted_iota(jnp.int32, (tile, tile), 0)
    c_i = lax.broadcasted_iota(jnp.int32, (tile, tile), 1)
    amask = (r_i // seq == c_i // seq) & (c_i <= r_i)
    half = lax.broadcasted_iota(jnp.int32, (tile, 1), 0) // seq
    for p in range(pairs):
        rs = slice(p * tile, (p + 1) * tile)
        for h in range(heads):
            ks = slice(h * kdim, (h + 1) * kdim)
            vs = slice(h * vdim, (h + 1) * vdim)
            v_t = v_b[rs, vs]
            att = lax.dot_general(qe[rs, ks], ke[rs, ks], NT, preferred_element_type=F32)
            o = _dot(jnp.where(amask, att, 0.0).astype(BF16), v_t)
            for bi in range(2):
                b = 2 * p + bi
                s0 = s_ref[0, b, h]
                o = o + jnp.where(half == bi, _dot(q_in[rs, ks], s0.astype(BF16)), 0.0)
                km = jnp.where(half == bi, k_out[rs, ks], 0.0).astype(BF16)
                kv = lax.dot_general(km, v_t, TN, preferred_element_type=F32)
                r0 = p * tile + bi * seq
                dcol = _col_broadcast(dec[r0:r0 + 1, ks])
                if vdim != LANE:
                    dcol = jnp.concatenate([dcol] * (vdim // LANE), axis=1)
                ns_ref[0, b, h] = s0 * dcol + kv
            o_scr[rs, vs] = o
    for h in range(heads):
        vs = slice(h * vdim, (h + 1) * vdim)
        o_ref[:, vs] = _gated_out(o_scr[:, vs], g[:, vs].astype(F32), gain)


def _sample_kernel(lb_ref, hgain_ref, ggain_ref, wgk_ref, bgk_ref,
                   hq_ref, hf_ref, hi_ref, hg_ref, gq_ref, gk_ref, gv_ref, gg_ref, glr_ref,
                   sh_ref, sg_ref,
                   a_ref, b_ref, nsh_ref, nsg_ref, oh_scr, og_scr, *, seq, pairs):
    lb = _forget_lower_bound(lb_ref[...])
    sig = jax.nn.sigmoid(hf_ref[...].astype(F32))
    _sample_branch(
        jax.nn.silu(hq_ref[...].astype(F32)) * (HG_K ** -0.5),
        (1.0 - lb) * (1.0 - sig),
        hi_ref[...],
        jnp.log(lb + (1.0 - lb) * sig),
        hg_ref[...], hgain_ref[...], sh_ref, nsh_ref, a_ref, oh_scr,
        heads=HG_HEADS, kdim=HG_K, vdim=HG_V, seq=seq, pairs=pairs)
    gk = _dot(glr_ref[...].astype(BF16), wgk_ref[...].astype(BF16)) + bgk_ref[...]
    _sample_branch(
        gq_ref[...].astype(F32) * (GLA_K ** -0.5),
        gk_ref[...].astype(F32),
        gv_ref[...],
        jax.nn.log_sigmoid(gk) / GLA_GATE_NORM,
        gg_ref[...], ggain_ref[...], sg_ref, nsg_ref, b_ref, og_scr,
        heads=GLA_HEADS, kdim=GLA_K, vdim=GLA_V, seq=seq, pairs=pairs)


def _sample_scan(z, glr, state_hgrn, state_gla, hg_lb, hgain, ggain, wgk, bgk, row0, nb, seq, bb):
    rows = bb * seq
    rb0 = row0 // rows
    hw, kw, vw = HG_HEADS * HG_K, GLA_HEADS * GLA_K, GLA_HEADS * GLA_V
    gq0 = (4 * hw) // kw
    gv0 = (4 * hw + 2 * kw) // vw
    zs = lambda w, c: pl.BlockSpec((rows, w), lambda i, c=c: (rb0 + i, c))
    const = lambda shape: pl.BlockSpec(shape, lambda i: (0,) * len(shape))
    st = lambda h, k, v: pl.BlockSpec((1, bb, h, k, v), lambda i: (0, i, 0, 0, 0))
    return pl.pallas_call(
        functools.partial(_sample_kernel, seq=seq, pairs=bb // 2),
        grid=(nb // bb,),
        in_specs=[
            const(hg_lb.shape), const((1, HG_V)), const((1, GLA_V)), const((LANE, kw)),
            const((1, kw)),
            zs(hw, 0), zs(hw, 1), zs(hw, 2), zs(hw, 3),
            zs(kw, gq0), zs(kw, gq0 + 1), zs(vw, gv0), zs(vw, gv0 + 1),
            pl.BlockSpec((rows, LANE), lambda i: (rb0 + i, 0)),
            st(HG_HEADS, HG_K, HG_V), st(GLA_HEADS, GLA_K, GLA_V),
        ],
        out_specs=[
            pl.BlockSpec((rows, hw), lambda i: (i, 0)),
            pl.BlockSpec((rows, vw), lambda i: (i, 0)),
            st(HG_HEADS, HG_K, HG_V), st(GLA_HEADS, GLA_K, GLA_V),
        ],
        out_shape=[jax.ShapeDtypeStruct((nb * seq, hw), BF16),
                   jax.ShapeDtypeStruct((nb * seq, vw), BF16),
                   jax.ShapeDtypeStruct(state_hgrn.shape, state_hgrn.dtype),
                   jax.ShapeDtypeStruct(state_gla.shape, state_gla.dtype)],
        scratch_shapes=[pltpu.VMEM((rows, hw), F32), pltpu.VMEM((rows, vw), F32)],
        compiler_params=_params(1),
        name="sample_scan",
    )(hg_lb, hgain, ggain, wgk, bgk, z, z, z, z, z, z, z, z, glr, state_hgrn, state_gla)


def _cast_specs(to_cast, n_j, n_i):
    specs = []
    for w, slab in to_cast:
        n_slabs = w.shape[0] // slab
        assert w.shape[0] % slab == 0 and n_slabs <= n_j * n_i
        specs.append(pl.BlockSpec(
            (slab, w.shape[1]),
            lambda j, i, n_slabs=n_slabs: (jnp.minimum(j * n_i + i, n_slabs - 1), 0)))
    return specs


def _merge_kernel(g0_ref, g1_ref, ap_ref, as_ref, bp_ref, bs_ref, wb0_ref, wb1_ref, *rest,
                  n_prompt_tiles, n_casts):
    cast_in, (o_ref, *cast_out), (wb_scr,) = (
        rest[:n_casts], rest[n_casts:2 * n_casts + 1], rest[2 * n_casts + 1:])
    tn = o_ref.shape[1]
    is_prompt = pl.program_id(1) < n_prompt_tiles

    @pl.when(pl.program_id(1) == 0)
    def _():
        wb_scr[0] = wb0_ref[0].astype(BF16)
        wb_scr[1] = wb1_ref[0].astype(BF16)

    a = jnp.where(is_prompt, ap_ref[...], as_ref[...])
    b = jnp.where(is_prompt, bp_ref[...], bs_ref[...])
    for c0 in range(0, tn, MERGE_COLS):
        cs = slice(c0, c0 + MERGE_COLS)
        m = jax.nn.sigmoid(g0_ref[:, cs].astype(F32)) * _dot(a, wb_scr[0, :, cs])
        m = m + jax.nn.sigmoid(g1_ref[:, cs].astype(F32)) * _dot(b, wb_scr[1, :, cs])
        o_ref[:, cs] = m.astype(o_ref.dtype)
    for src, dst in zip(cast_in, cast_out):
        dst[...] = src[...].astype(BF16)


def _merge(z, a_p, a_s, b_p, b_s, w_branch, to_cast, tm, tn):
    m = z.shape[0]
    d = w_branch.shape[2]
    bw = a_p.shape[1]
    npt = a_p.shape[0] // tm
    n_i = m // tm
    cast_specs = _cast_specs(to_cast, d // tn, n_i)
    pspec =pl.BlockSpec((tm, bw), lambda j, i: (jnp.minimum(i, npt - 1), 0))
    sspec = pl.BlockSpec((tm, bw), lambda j, i: (jnp.maximum(i - npt, 0), 0))
    gate = lambda n: pl.BlockSpec((tm, tn), lambda j, i, n=n: (i, (MAIN_COLS + n * d) // tn + j))
    return pl.pallas_call(
        functools.partial(_merge_kernel, n_prompt_tiles=npt, n_casts=len(to_cast)),
        grid=(d // tn, n_i),
        in_specs=[
            gate(0), gate(1),
            pspec, sspec, pspec, sspec,
            pl.BlockSpec((1, bw, tn), lambda j, i: (0, 0, j)),
            pl.BlockSpec((1, bw, tn), lambda j, i: (1, 0, j)),
        ] + cast_specs,
        out_specs=[pl.BlockSpec((tm, tn), lambda j, i: (i, j))] + cast_specs,
        out_shape=[jax.ShapeDtypeStruct((m, d), BF16)]
        + [jax.ShapeDtypeStruct(w.shape, BF16) for w, _ in to_cast],
        scratch_shapes=[pltpu.VMEM((2, bw, tn), BF16)],
        compiler_params=_params(2),
        name="merge",
    )(z, z, a_p, a_s, b_p, b_s, w_branch, w_branch, *[w for w, _ in to_cast])


def _out_proj_kernel(m_ref, xp_ref, xs_ref, w_ref, ln_ref, x1_ref, h2_ref, *, n_prompt_tiles):
    is_prompt = pl.program_id(0) < n_prompt_tiles
    m = m_ref[...]
    tm, d = x1_ref.shape
    ss = jnp.zeros((tm, 1), F32)
    for c0 in range(0, d, EPILOGUE_COLS):
        cs = slice(c0, c0 + EPILOGUE_COLS)
        x1 = jnp.where(is_prompt, xp_ref[:, cs], xs_ref[:, cs]) + _dot(m, w_ref[:, cs])
        x1_ref[:, cs] = x1
        ss = ss + jnp.sum(x1 * x1, axis=-1, keepdims=True)
    inv = lax.rsqrt(ss * (1.0 / d) + EPS)
    for c0 in range(0, d, EPILOGUE_COLS):
        cs = slice(c0, c0 + EPILOGUE_COLS)
        h2_ref[:, cs] = (x1_ref[:, cs] * inv * ln_ref[:, cs]).astype(BF16)


def _out_proj(merged, xp, xs, w_out, ln2, tm):
    m, d = merged.shape
    npt = xp.shape[0] // tm
    return pl.pallas_call(
        functools.partial(_out_proj_kernel, n_prompt_tiles=npt),
        grid=(m // tm,),
        in_specs=[
            pl.BlockSpec((tm, d), lambda i: (i, 0)),
            pl.BlockSpec((tm, d), lambda i: (jnp.minimum(i, npt - 1), 0)),
            pl.BlockSpec((tm, d), lambda i: (jnp.maximum(i - npt, 0), 0)),
            pl.BlockSpec((d, d), lambda i: (0, 0), pipeline_mode=pl.Buffered(1)),
            pl.BlockSpec((1, d), lambda i: (0, 0)),
        ],
        out_specs=[pl.BlockSpec((tm, d), lambda i: (i, 0)),
                   pl.BlockSpec((tm, d), lambda i: (i, 0))],
        out_shape=[jax.ShapeDtypeStruct((m, d), F32), jax.ShapeDtypeStruct((m, d), BF16)],
        compiler_params=_params(1),
        name="out_proj",
    )(merged, xp, xs, w_out, ln2)


def _ffn_up_kernel(h_ref, wg_ref, wu_ref, *rest, n_casts):
    cast_in, (o_ref, *cast_out), (w_scr,) = (
        rest[:n_casts], rest[n_casts:2 * n_casts + 1], rest[2 * n_casts + 1:])

    @pl.when(pl.program_id(1) == 0)
    def _():
        w_scr[0] = wg_ref[...].astype(BF16)
        w_scr[1] = wu_ref[...].astype(BF16)

    h = h_ref[...]
    for c0 in range(0, o_ref.shape[1], MERGE_COLS):
        cs = slice(c0, c0 + MERGE_COLS)
        gate = _dot(h, w_scr[0, :, cs])
        o_ref[:, cs] = (jax.nn.silu(gate) * _dot(h, w_scr[1, :, cs])).astype(o_ref.dtype)
    for src, dst in zip(cast_in, cast_out):
        dst[...] = src[...].astype(BF16)


def _ffn_up(h2, w_gu, to_cast, tm, tf):
    m, d = h2.shape
    dff = w_gu.shape[1] // 2
    nj = dff // tf
    cast_specs = _cast_specs(to_cast, nj, m // tm)
    return pl.pallas_call(
        functools.partial(_ffn_up_kernel, n_casts=len(to_cast)),
        grid=(nj, m // tm),
        in_specs=[
            pl.BlockSpec((tm, d), lambda j, i: (i, 0)),
            pl.BlockSpec((d, tf), lambda j, i: (0, j)),
            pl.BlockSpec((d, tf), lambda j, i: (0, nj + j)),
        ] + cast_specs,
        out_specs=[pl.BlockSpec((tm, tf), lambda j, i: (i, j))] + cast_specs,
        out_shape=[jax.ShapeDtypeStruct((m, dff), BF16)]
        + [jax.ShapeDtypeStruct(w.shape, BF16) for w, _ in to_cast],
        scratch_shapes=[pltpu.VMEM((2, d, tf), BF16)],
        compiler_params=_params(2, VMEM_LIMIT_LARGE),
        name="ffn_up",
    )(h2, w_gu, w_gu, *[w for w, _ in to_cast])


def _ffn_down_kernel(a_ref, w_ref, x_ref, o_ref):
    a = a_ref[...]
    for c0 in range(0, o_ref.shape[1], MERGE_COLS):
        cs = slice(c0, c0 + MERGE_COLS)
        o_ref[:, cs] = x_ref[:, cs] + _dot(a, w_ref[:, cs])


def _ffn_down(act, w_down, x1, tm, tn):
    m, dff = act.shape
    d = w_down.shape[1]
    return pl.pallas_call(
        _ffn_down_kernel,
        grid=(d // tn, m // tm),
        in_specs=[
            pl.BlockSpec((tm, dff), lambda j, i: (i, 0)),
            pl.BlockSpec((dff, tn), lambda j, i: (0, j)),
            pl.BlockSpec((tm, tn), lambda j, i: (i, j)),
        ],
        out_specs=pl.BlockSpec((tm, tn), lambda j, i: (i, j)),
        out_shape=jax.ShapeDtypeStruct((m, d), F32),
        compiler_params=_params(2),
        name="ffn_down",
    )(act, w_down, x1)


def _final_kernel(x_ref, pp_ref, ps_ref, wpg_ref, wple_ref, ln3_ref, lnf_ref, yp_ref, ys_ref,
                  x3_scr, *, n_prompt_tiles):
    i = pl.program_id(0)
    is_prompt = i < n_prompt_tiles
    tm, d = x_ref.shape
    x2 = x_ref[...]
    h3 = (x2 * ln3_ref[...]).astype(BF16)
    inv3 = lax.rsqrt(jnp.mean(x2 * x2, axis=-1, keepdims=True) + EPS)
    p = jnp.where(is_prompt, pp_ref[...], ps_ref[...]).astype(BF16)
    ss = jnp.zeros((tm, 1), F32)
    for c0 in range(0, d, EPILOGUE_COLS):
        cs = slice(c0, c0 + EPILOGUE_COLS)
        gate = jax.nn.sigmoid(inv3 * _dot(h3, wpg_ref[:, cs]))
        x3 = x_ref[:, cs] + gate * _dot(p, wple_ref[:, cs])
        x3_scr[:, cs] = x3
        ss = ss + jnp.sum(x3 * x3, axis=-1, keepdims=True)
    inv = lax.rsqrt(ss * (1.0 / d) + EPS)

    def write(y_ref):
        for c0 in range(0, d, EPILOGUE_COLS):
            cs = slice(c0, c0 + EPILOGUE_COLS)
            y_ref[:, cs] = x3_scr[:, cs] * inv * lnf_ref[:, cs]

    @pl.when(is_prompt)
    def _():
        write(yp_ref)

    @pl.when(jnp.logical_not(is_prompt))
    def _():
        write(ys_ref)


def _final(x2, pp, ps, w_pg, w_ple, ln3, ln_f, tm):
    m, d = x2.shape
    npr, ns = pp.shape[0], ps.shape[0]
    pd = pp.shape[1]
    npt = npr // tm
    pidx = lambda i: (jnp.minimum(i, npt - 1), 0)
    sidx = lambda i: (jnp.maximum(i - npt, 0), 0)
    return pl.pallas_call(
        functools.partial(_final_kernel, n_prompt_tiles=npt),
        grid=(m // tm,),
        in_specs=[
            pl.BlockSpec((tm, d), lambda i: (i, 0)),
            pl.BlockSpec((tm, pd), pidx),
            pl.BlockSpec((tm, pd), sidx),
            pl.BlockSpec((d, d), lambda i: (0, 0), pipeline_mode=pl.Buffered(1)),
            pl.BlockSpec((pd, d), lambda i: (0, 0)),
            pl.BlockSpec((1, d), lambda i: (0, 0)),
            pl.BlockSpec((1, d), lambda i: (0, 0)),
        ],
        out_specs=[pl.BlockSpec((tm, d), pidx), pl.BlockSpec((tm, d), sidx)],
        out_shape=[jax.ShapeDtypeStruct((npr, d), F32), jax.ShapeDtypeStruct((ns, d), F32)],
        scratch_shapes=[pltpu.VMEM((tm, d), F32)],
        compiler_params=_params(1),
        name="final",
    )(x2, pp, ps, w_pg, w_ple, ln3, ln_f)


def kernel(x_prompt, x_sample, state_hgrn, state_gla, p_prompt, p_sample, hg_lb, ln1, w_in, hg_norm,
           gla_w_gk, gla_b_gk, gla_norm, w_branch, w_out, ln2, w_gu, w_down, ln3, w_ple, w_pg, ln_f):
    batch, seq, d = x_prompt.shape
    nb, dseq, _ = x_sample.shape
    depth = w_in.shape[0]
    assert depth == 1, "single-layer step"
    npr, ns = batch * seq, nb * dseq
    xp = x_prompt.reshape(npr, d)
    xs = x_sample.reshape(ns, d)
    row = lambda v: v.reshape(1, -1)

    w_in0 = jnp.swapaxes(w_in[0], 0, 1)
    wgk = jnp.pad(gla_w_gk[0], ((0, LANE - GLA_GATE_RANK), (0, 0)))
    bgk = row(gla_b_gk[0])

    h1, glr = _prep(xp, xs, row(ln1[0]), w_in0, tm=512)
    z = _in_proj(h1, w_in0, n_cols=MAIN_COLS + 2 * d, tm=2176, tn=1024)

    a_p, s_hp = _hgrn_prompt(z, hg_lb, row(hg_norm[0]), batch, seq, hpb=8)
    b_p, s_gp = _gla_prompt(z, glr, wgk, bgk, row(gla_norm[0]), batch, seq, hpb=4)
    a_s, b_s, s_hs, s_gs = _sample_scan(z, glr, state_hgrn, state_gla, hg_lb, row(hg_norm[0]),
                                        row(gla_norm[0]), wgk, bgk, row0=npr, nb=nb, seq=dseq, bb=8)

    merged, w_out_b = _merge(z, a_p, a_s, b_p, b_s, w_branch[0], to_cast=[(w_out[0], 64)],
                             tm=512, tn=1024)
    x1, h2 = _out_proj(merged, xp, xs, w_out_b, row(ln2[0]), tm=512)
    act, w_down_b, w_pg_b, w_ple_b = _ffn_up(
        h2, w_gu[0], to_cast=[(w_down[0], 128), (w_pg[0], 64), (w_ple[0], 32)], tm=2176, tf=512)
    x2 = _ffn_down(act, w_down_b, x1, tm=544, tn=1024)
    yp, ys = _final(x2, p_prompt[0].reshape(npr, -1), p_sample[0].reshape(ns, -1),
                    w_pg_b, w_ple_b, row(ln3[0]), row(ln_f), tm=512)

    return (yp.reshape(batch, seq, d), ys.reshape(nb, dseq, d),
            s_hp[None], s_gp[None], s_hs, s_gs)
```

```python
import functools

import jax
import jax.numpy as jnp
from jax import lax
from jax.experimental import pallas as pl
from jax.experimental.pallas import tpu as pltpu

F32 = jnp.float32
BF16 = jnp.bfloat16

EPS = 1e-6
LANE = 128
HG_HEADS, HG_K, HG_V = 8, 128, 128
GLA_HEADS, GLA_K, GLA_V = 4, 128, 256
GLA_GATE_RANK = 16
GLA_GATE_NORM = 16.0
CHUNK = 32
MAIN_COLS = 7168
VMEM_LIMIT = 56 * 1024 * 1024
VMEM_LIMIT_LARGE = 60 * 1024 * 1024
EPILOGUE_COLS = 256
MERGE_COLS = 256

NT = (((1,), (1,)), ((), ()))
TN = (((0,), (0,)), ((), ()))


def _params(n_axes, vmem=VMEM_LIMIT):
    return pltpu.CompilerParams(dimension_semantics=("arbitrary",) * n_axes,
                                vmem_limit_bytes=vmem)


def _dot(a, b):
    return jnp.dot(a, b, preferred_element_type=F32)


def _rms(x, g):
    return x * lax.rsqrt(jnp.mean(x * x, axis=-1, keepdims=True) + EPS) * g


def _prep_kernel(xp_ref, xs_ref, ln_ref, wglr_ref, h_ref, glr_ref, xs2d_ref, *, n_prompt_tiles):
    i = pl.program_id(0)

    d = wglr_ref.shape[1]
    wglr = jnp.concatenate([wglr_ref[...], jnp.zeros((LANE - GLA_GATE_RANK, d), F32)], axis=0)
    wglr = wglr.astype(BF16)

    def body(x):
        h = _rms(x, ln_ref[...]).astype(BF16)
        h_ref[...] = h
        glr_ref[...] = lax.dot_general(h, wglr, NT, preferred_element_type=F32)

    @pl.when(i < n_prompt_tiles)
    def _():
        body(xp_ref[...])

    @pl.when(i >= n_prompt_tiles)
    def _():
        x = xs_ref[...].reshape(xs2d_ref.shape)
        xs2d_ref[...] = x
        body(x)


def _prep(xp, xs3, ln1, w_in_t, tm):
    npr, d = xp.shape
    nb, dseq, _ = xs3.shape
    ns = nb * dseq
    assert ns == tm
    npt = npr // tm
    m = npr + ns
    return pl.pallas_call(
        functools.partial(_prep_kernel, n_prompt_tiles=npt),
        grid=(npt + 1,),
        in_specs=[
            pl.BlockSpec((tm, d), lambda i: (jnp.minimum(i, npt - 1), 0)),
            pl.BlockSpec((nb, dseq, d), lambda i: (0, 0, 0)),
            pl.BlockSpec((1, d), lambda i: (0, 0)),
            pl.BlockSpec((GLA_GATE_RANK, d), lambda i: (MAIN_COLS // GLA_GATE_RANK, 0)),
        ],
        out_specs=[
            pl.BlockSpec((tm, d), lambda i: (i, 0)),
            pl.BlockSpec((tm, LANE), lambda i: (i, 0)),
            pl.BlockSpec((tm, d), lambda i: (0, 0)),
        ],
        out_shape=[jax.ShapeDtypeStruct((m, d), BF16),
                   jax.ShapeDtypeStruct((m, LANE), F32),
                   jax.ShapeDtypeStruct((ns, d), F32)],
        compiler_params=_params(1),
        name="prep",
    )(xp, xs3, ln1, w_in_t)


TRANSPOSE_ROWS = 256


def _stage_transposed(w_t, dst_ref, n=None):
    for r in range(0, w_t.shape[0], TRANSPOSE_ROWS):
        blk = w_t[r:r + TRANSPOSE_ROWS, :].T.astype(BF16)
        if n is None:
            dst_ref[:, r:r + TRANSPOSE_ROWS] = blk
        else:
            dst_ref[n, :, r:r + TRANSPOSE_ROWS] = blk


def _in_proj_kernel(h_ref, w_ref, wx_ref, z_ref, wbf_ref, *, n_aligned):
    j = pl.program_id(0)
    stage = pl.program_id(1) == 0
    tn = z_ref.shape[1]

    @pl.when(stage & (j < n_aligned))
    def _():
        _stage_transposed(w_ref[...], wbf_ref)

    @pl.when(stage & (j >= n_aligned))
    def _():
        cat = jnp.concatenate([w_ref[...], wx_ref[...]], axis=0)
        _stage_transposed(cat[GLA_GATE_RANK:GLA_GATE_RANK + tn, :], wbf_ref)

    h = h_ref[...]
    for c0 in range(0, tn, MERGE_COLS):
        cs = slice(c0, c0 + MERGE_COLS)
        z_ref[:, cs] = _dot(h, wbf_ref[:, cs]).astype(z_ref.dtype)


def _in_proj(h1, w_in_t, n_cols, tm, tn):
    m, d = h1.shape
    rk = GLA_GATE_RANK
    return pl.pallas_call(
        functools.partial(_in_proj_kernel, n_aligned=MAIN_COLS // tn),
        grid=(n_cols // tn, m // tm),
        in_specs=[
            pl.BlockSpec((tm, d), lambda j, i: (i, 0)),
            pl.BlockSpec((tn, d), lambda j, i: (j, 0)),
            pl.BlockSpec((rk, d), lambda j, i: ((j + 1) * (tn // rk), 0)),
        ],
        out_specs=pl.BlockSpec((tm, tn), lambda j, i: (i, j)),
        out_shape=jax.ShapeDtypeStruct((m, n_cols), BF16),
        scratch_shapes=[pltpu.VMEM((d, tn), BF16)],
        compiler_params=_params(2, VMEM_LIMIT_LARGE),
        name="in_proj",
    )(h1, w_in_t, w_in_t)


def _cumsum_rows(x, length):
    row = lax.broadcasted_iota(jnp.int32, x.shape, 0) % length
    s = 1
    while s < length:
        x = x + jnp.where(row >= s, pltpu.roll(x, s, axis=0), 0.0)
        s *= 2
    return x


def _forget_lower_bound(lb_rows):
    mx = jnp.max(lb_rows, axis=0, keepdims=True)
    e = jnp.exp(lb_rows - mx)
    return e[0:1, :] / jnp.sum(e, axis=0, keepdims=True)


def _col_broadcast(row):
    return jnp.broadcast_to(row, (LANE, LANE)).T


def _gated_out(o, g, gain):
    return (_rms(o, gain) * jax.nn.silu(g)).astype(BF16)


SLAB = 256
CHUNKS_PER_SLAB = SLAB // CHUNK
SUBLANES = 8
DEC_ROWS = -(-CHUNKS_PER_SLAB // SUBLANES) * SUBLANES


def _slab_rows(g):
    return pl.ds(pl.multiple_of(g * SLAB, SLAB), SLAB)


def _per_chunk_rows(rows_1w):
    w = rows_1w[0].shape[1]
    return jnp.concatenate([jnp.broadcast_to(r, (CHUNK, w)) for r in rows_1w], axis=0)


def _intra_slab(slot, q, k, logf, kdim, vdim, v_slab, qin_scr, kout_scr, dec_scr, o_scr):
    heads = q.shape[1] // kdim
    cum = _cumsum_rows(logf, CHUNK)
    ref_rows = [cum[c * CHUNK + CHUNK // 2:c * CHUNK + CHUNK // 2 + 1, :]
                for c in range(CHUNKS_PER_SLAB)]
    last_rows = [cum[(c + 1) * CHUNK - 1:(c + 1) * CHUNK, :] for c in range(CHUNKS_PER_SLAB)]
    ref = _per_chunk_rows(ref_rows)
    qe = q * jnp.exp(cum - ref)
    ke = k * jnp.exp(ref - cum)
    qin_scr[slot] = (qe * _per_chunk_rows([jnp.exp(r) for r in ref_rows])).astype(BF16)
    kout_scr[slot] = (
        ke * _per_chunk_rows([jnp.exp(l - r) for l, r in zip(last_rows, ref_rows)])).astype(BF16)
    for c in range(CHUNKS_PER_SLAB):
        dec_scr[slot, c:c + 1, :] = jnp.exp(last_rows[c])
    qe = qe.astype(BF16)
    ke = ke.astype(BF16)
    r_i = lax.broadcasted_iota(jnp.int32, (SLAB, SLAB), 0)
    c_i = lax.broadcasted_iota(jnp.int32, (SLAB, SLAB), 1)
    causal = (r_i // CHUNK == c_i // CHUNK) & (c_i <= r_i)
    for h in range(heads):
        ks = slice(h * kdim, (h + 1) * kdim)
        vs = slice(h * vdim, (h + 1) * vdim)
        att = lax.dot_general(qe[:, ks], ke[:, ks], NT, preferred_element_type=F32)
        o_scr[slot, :, vs] = _dot(jnp.where(causal, att, 0.0).astype(BF16), v_slab[:, vs])


def _state_chunk(g, slot, c, heads, kdim, vdim, v_ref, g_ref, gain, qin_scr, kout_scr, dec_scr,
                 o_scr, st_scr, o_ref):
    cr = slice(c * CHUNK, (c + 1) * CHUNK)
    rows = pl.ds(pl.multiple_of(g * SLAB + c * CHUNK, CHUNK), CHUNK)
    drow = dec_scr[slot, c:c + 1, :]
    for h in range(heads):
        ks = slice(h * kdim, (h + 1) * kdim)
        vs = slice(h * vdim, (h + 1) * vdim)
        st = st_scr[h]
        o = o_scr[slot, cr, vs] + _dot(qin_scr[slot, cr, ks], st.astype(BF16))
        kv = lax.dot_general(kout_scr[slot, cr, ks], v_ref[rows, vs], TN,
                             preferred_element_type=F32)
        dcol = _col_broadcast(drow[:, ks])
        if vdim != LANE:
            dcol = jnp.concatenate([dcol] * (vdim // LANE), axis=1)
        st_scr[h] = st * dcol + kv
        o_ref[rows, vs] = _gated_out(o, g_ref[rows, vs].astype(F32), gain)


def _run_pipelined(n_slabs, intra, state, st_scr, s_ref):
    assert n_slabs % 2 == 0 and n_slabs >= 4

    def overlapped(g_intra, slot_intra, g_state, slot_state):
        intra(g_intra, slot_intra)
        for c in range(CHUNKS_PER_SLAB):
            state(g_state, slot_state, c)

    st_scr[...] = jnp.zeros_like(st_scr)
    intra(0, 0)

    def pair(i, carry):
        overlapped(2 * i + 1, 1, 2 * i, 0)
        overlapped(2 * i + 2, 0, 2 * i + 1, 1)
        return carry

    lax.fori_loop(0, n_slabs // 2 - 1, pair, 0)
    overlapped(n_slabs - 1, 1, n_slabs - 2, 0)
    for c in range(CHUNKS_PER_SLAB):
        state(n_slabs - 1, 1, c)
    for h in range(st_scr.shape[0]):
        s_ref[0, h] = st_scr[h]


def _hgrn_prompt_kernel(lb_ref, gain_ref, q_ref, f_ref, i_ref, g_ref, o_ref, s_ref,
                        qin_scr, kout_scr, dec_scr, o_scr, st_scr, *, hpb):
    lb = _forget_lower_bound(lb_ref[...])
    gain = gain_ref[...]

    def intra(g, slot):
        rows = _slab_rows(g)
        q = jax.nn.silu(q_ref[rows, :].astype(F32)) * (HG_K ** -0.5)
        sig = jax.nn.sigmoid(f_ref[rows, :].astype(F32))
        logf = jnp.log(lb + (1.0 - lb) * sig)
        k = (1.0 - lb) * (1.0 - sig)
        _intra_slab(slot, q, k, logf, HG_K, HG_V, i_ref[rows, :], qin_scr, kout_scr, dec_scr,
                    o_scr)

    def state(g, slot, c):
        _state_chunk(g, slot, c, hpb, HG_K, HG_V, i_ref, g_ref, gain, qin_scr, kout_scr, dec_scr,
                     o_scr, st_scr, o_ref)

    _run_pipelined(q_ref.shape[0] // SLAB, intra, state, st_scr, s_ref)


def _hgrn_prompt(z, hg_lb, gain, batch, seq, hpb):
    w = hpb * HG_K
    sec = (HG_HEADS * HG_K) // w
    zspec = lambda s: pl.BlockSpec((seq, w), lambda b, g, s=s: (b, s * sec + g))
    return pl.pallas_call(
        functools.partial(_hgrn_prompt_kernel, hpb=hpb),
        grid=(batch, HG_HEADS // hpb),
        in_specs=[
            pl.BlockSpec((hg_lb.shape[0], w), lambda b, g: (0, g)),
            pl.BlockSpec((1, HG_V), lambda b, g: (0, 0)),
            zspec(0), zspec(1), zspec(2), zspec(3),
        ],
        out_specs=[
            pl.BlockSpec((seq, w), lambda b, g: (b, g)),
            pl.BlockSpec((1, hpb, HG_K, HG_V), lambda b, g: (b, g, 0, 0)),
        ],
        out_shape=[jax.ShapeDtypeStruct((batch * seq, HG_HEADS * HG_V), BF16),
                   jax.ShapeDtypeStruct((batch, HG_HEADS, HG_K, HG_V), F32)],
        scratch_shapes=[pltpu.VMEM((2, SLAB, w), BF16), pltpu.VMEM((2, SLAB, w), BF16),
                        pltpu.VMEM((2, DEC_ROWS, w), F32), pltpu.VMEM((2, SLAB, w), F32),
                        pltpu.VMEM((hpb, HG_K, HG_V), F32)],
        compiler_params=_params(2),
        name="hgrn_prompt",
    )(hg_lb, gain, z, z, z, z)


def _gla_prompt_kernel(gain_ref, wgk_ref, bgk_ref, glr_ref, q_ref, k_ref, v_ref, g_ref,
                       o_ref, s_ref, qin_scr, kout_scr, dec_scr, o_scr, st_scr, *, hpb):
    gain = gain_ref[...]
    wgk = wgk_ref[...].astype(BF16)
    bgk = bgk_ref[...]

    def intra(g, slot):
        rows = _slab_rows(g)
        gk = _dot(glr_ref[rows, :].astype(BF16), wgk) + bgk
        logf = jax.nn.log_sigmoid(gk) / GLA_GATE_NORM
        q = q_ref[rows, :].astype(F32) * (GLA_K ** -0.5)
        _intra_slab(slot, q, k_ref[rows, :].astype(F32), logf, GLA_K, GLA_V, v_ref[rows, :],
                    qin_scr, kout_scr, dec_scr, o_scr)

    def state(g, slot, c):
        _state_chunk(g, slot, c, hpb, GLA_K, GLA_V, v_ref, g_ref, gain, qin_scr, kout_scr, dec_scr,
                     o_scr, st_scr, o_ref)

    _run_pipelined(q_ref.shape[0] // SLAB, intra, state, st_scr, s_ref)


def _gla_prompt(z, glr, wgk, bgk, gain, batch, seq, hpb):
    kw, vw = hpb * GLA_K, hpb * GLA_V
    q0 = (4 * HG_HEADS * HG_K) // kw
    k0 = q0 + (GLA_HEADS * GLA_K) // kw
    v0 = (4 * HG_HEADS * HG_K + 2 * GLA_HEADS * GLA_K) // vw
    g0 = v0 + (GLA_HEADS * GLA_V) // vw
    return pl.pallas_call(
        functools.partial(_gla_prompt_kernel, hpb=hpb),
        grid=(batch, GLA_HEADS // hpb),
        in_specs=[
            pl.BlockSpec((1, GLA_V), lambda b, g: (0, 0)),
            pl.BlockSpec((LANE, kw), lambda b, g: (0, g)),
            pl.BlockSpec((1, kw), lambda b, g: (0, g)),
            pl.BlockSpec((seq, LANE), lambda b, g: (b, 0)),
            pl.BlockSpec((seq, kw), lambda b, g: (b, q0 + g)),
            pl.BlockSpec((seq, kw), lambda b, g: (b, k0 + g)),
            pl.BlockSpec((seq, vw), lambda b, g: (b, v0 + g)),
            pl.BlockSpec((seq, vw), lambda b, g: (b, g0 + g)),
        ],
        out_specs=[
            pl.BlockSpec((seq, vw), lambda b, g: (b, g)),
            pl.BlockSpec((1, hpb, GLA_K, GLA_V), lambda b, g: (b, g, 0, 0)),
        ],
        out_shape=[jax.ShapeDtypeStruct((batch * seq, GLA_HEADS * GLA_V), BF16),
                   jax.ShapeDtypeStruct((batch, GLA_HEADS, GLA_K, GLA_V), F32)],
        scratch_shapes=[pltpu.VMEM((2, SLAB, kw), BF16), pltpu.VMEM((2, SLAB, kw), BF16),
                        pltpu.VMEM((2, DEC_ROWS, kw), F32), pltpu.VMEM((2, SLAB, vw), F32),
                        pltpu.VMEM((hpb, GLA_K, GLA_V), F32)],
        compiler_params=_params(2),
        name="gla_prompt",
    )(gain, wgk, bgk, glr, z, z, z, z)


def _seg_pick(x, seg, idx):
    n = x.shape[0]
    rowmod = lax.broadcasted_iota(jnp.int32, x.shape, 0) % seg
    out = x
    for m in range(seg):
        if m != idx:
            out = jnp.where(rowmod == m, pltpu.roll(x, (m - idx) % n, axis=0), out)
    return out


def _sample_branch(q, k, v_b, logf, g, gain, s_ref, ns_ref, o_ref, o_scr, *, heads, kdim, vdim,
                   seq, pairs):
    tile = 2 * seq
    cum = _cumsum_rows(logf, seq)
    ref = _seg_pick(cum, seq, seq // 2)
    last = _seg_pick(cum, seq, seq - 1)
    qe = (q * jnp.exp(cum - ref)).astype(BF16)
    ke = (k * jnp.exp(ref - cum)).astype(BF16)
    q_in = (q * jnp.exp(cum)).astype(BF16)
    k_out = k * jnp.exp(last - cum)
    dec = jnp.exp(last)
    r_i = lax.broadcasted_iota(jnp.int32, (tile, tile), 0)
    c_i = lax.broadcasted_iota(jnp.int32, (tile, tile), 1)
    amask = (r_i // seq == c_i // seq) & (c_i <= r_i)
    half = lax.broadcasted_iota(jnp.int32, (tile, 1), 0) // seq
    for p in range(pairs):
        rs = slice(p * tile, (p + 1) * tile)
        for h in range(heads):
            ks = slice(h * kdim, (h + 1) * kdim)
            vs = slice(h * vdim, (h + 1) * vdim)
            v_t = v_b[rs, vs]
            att = lax.dot_general(qe[rs, ks], ke[rs, ks], NT, preferred_element_type=F32)
            o = _dot(jnp.where(amask, att, 0.0).astype(BF16), v_t)
            for bi in range(2):
                b = 2 * p + bi
                s0 = s_ref[0, b, h]
                o = o + jnp.where(half == bi, _dot(q_in[rs, ks], s0.astype(BF16)), 0.0)
                km = jnp.where(half == bi, k_out[rs, ks], 0.0).astype(BF16)
                kv = lax.dot_general(km, v_t, TN, preferred_element_type=F32)
                r0 = p * tile + bi * seq
                dcol = _col_broadcast(dec[r0:r0 + 1, ks])
                if vdim != LANE:
                    dcol = jnp.concatenate([dcol] * (vdim // LANE), axis=1)
                ns_ref[0, b, h] = s0 * dcol + kv
            o_scr[rs, vs] = o
    for h in range(heads):
        vs = slice(h * vdim, (h + 1) * vdim)
        o_ref[:, vs] = _gated_out(o_scr[:, vs], g[:, vs].astype(F32), gain)


def _sample_kernel(lb_ref, hgain_ref, ggain_ref, wgk_ref, bgk_ref,
                   hq_ref, hf_ref, hi_ref, hg_ref, gq_ref, gk_ref, gv_ref, gg_ref, glr_ref,
                   sh_ref, sg_ref,
                   a_ref, b_ref, nsh_ref, nsg_ref, oh_scr, og_scr, *, seq, pairs):
    lb = _forget_lower_bound(lb_ref[...])
    sig = jax.nn.sigmoid(hf_ref[...].astype(F32))
    _sample_branch(
        jax.nn.silu(hq_ref[...].astype(F32)) * (HG_K ** -0.5),
        (1.0 - lb) * (1.0 - sig),
        hi_ref[...],
        jnp.log(lb + (1.0 - lb) * sig),
        hg_ref[...], hgain_ref[...], sh_ref, nsh_ref, a_ref, oh_scr,
        heads=HG_HEADS, kdim=HG_K, vdim=HG_V, seq=seq, pairs=pairs)
    gk = _dot(glr_ref[...].astype(BF16), wgk_ref[...].astype(BF16)) + bgk_ref[...]
    _sample_branch(
        gq_ref[...].astype(F32) * (GLA_K ** -0.5),
        gk_ref[...].astype(F32),
        gv_ref[...],
        jax.nn.log_sigmoid(gk) / GLA_GATE_NORM,
        gg_ref[...], ggain_ref[...], sg_ref, nsg_ref, b_ref, og_scr,
        heads=GLA_HEADS, kdim=GLA_K, vdim=GLA_V, seq=seq, pairs=pairs)


def _sample_scan(z, glr, state_hgrn, state_gla, hg_lb, hgain, ggain, wgk, bgk, row0, nb, seq, bb):
    rows = bb * seq
    rb0 = row0 // rows
    hw, kw, vw = HG_HEADS * HG_K, GLA_HEADS * GLA_K, GLA_HEADS * GLA_V
    gq0 = (4 * hw) // kw
    gv0 = (4 * hw + 2 * kw) // vw
    zs = lambda w, c: pl.BlockSpec((rows, w), lambda i, c=c: (rb0 + i, c))
    const = lambda shape: pl.BlockSpec(shape, lambda i: (0,) * len(shape))
    st = lambda h, k, v: pl.BlockSpec((1, bb, h, k, v), lambda i: (0, i, 0, 0, 0))
    return pl.pallas_call(
        functools.partial(_sample_kernel, seq=seq, pairs=bb // 2),
        grid=(nb // bb,),
        in_specs=[
            const(hg_lb.shape), const((1, HG_V)), const((1, GLA_V)), const((LANE, kw)),
            const((1, kw)),
            zs(hw, 0), zs(hw, 1), zs(hw, 2), zs(hw, 3),
            zs(kw, gq0), zs(kw, gq0 + 1), zs(vw, gv0), zs(vw, gv0 + 1),
            pl.BlockSpec((rows, LANE), lambda i: (rb0 + i, 0)),
            st(HG_HEADS, HG_K, HG_V), st(GLA_HEADS, GLA_K, GLA_V),
        ],
        out_specs=[
            pl.BlockSpec((rows, hw), lambda i: (i, 0)),
            pl.BlockSpec((rows, vw), lambda i: (i, 0)),
            st(HG_HEADS, HG_K, HG_V), st(GLA_HEADS, GLA_K, GLA_V),
        ],
        out_shape=[jax.ShapeDtypeStruct((nb * seq, hw), BF16),
                   jax.ShapeDtypeStruct((nb * seq, vw), BF16),
                   jax.ShapeDtypeStruct(state_hgrn.shape, state_hgrn.dtype),
                   jax.ShapeDtypeStruct(state_gla.shape, state_gla.dtype)],
        scratch_shapes=[pltpu.VMEM((rows, hw), F32), pltpu.VMEM((rows, vw), F32)],
        compiler_params=_params(1),
        name="sample_scan",
    )(hg_lb, hgain, ggain, wgk, bgk, z, z, z, z, z, z, z, z, glr, state_hgrn, state_gla)


def _cast_specs(to_cast, n_j, n_i):
    specs = []
    for w, slab in to_cast:
        n_slabs = w.shape[0] // slab
        assert w.shape[0] % slab == 0 and n_slabs <= n_j * n_i
        specs.append(pl.BlockSpec(
            (slab, w.shape[1]),
            lambda j, i, n_slabs=n_slabs: (jnp.minimum(j * n_i + i, n_slabs - 1), 0)))
    return specs


def _merge_kernel(g0_ref, g1_ref, ap_ref, as_ref, bp_ref, bs_ref, wb0_ref, wb1_ref, *rest,
                  n_prompt_tiles, n_casts):
    cast_in, (o_ref, *cast_out), (wb_scr,) = (
        rest[:n_casts], rest[n_casts:2 * n_casts + 1], rest[2 * n_casts + 1:])
    tn = o_ref.shape[1]
    is_prompt = pl.program_id(1) < n_prompt_tiles

    @pl.when(pl.program_id(1) == 0)
    def _():
        wb_scr[0] = wb0_ref[0].astype(BF16)
        wb_scr[1] = wb1_ref[0].astype(BF16)

    a = jnp.where(is_prompt, ap_ref[...], as_ref[...])
    b = jnp.where(is_prompt, bp_ref[...], bs_ref[...])
    for c0 in range(0, tn, MERGE_COLS):
        cs = slice(c0, c0 + MERGE_COLS)
        m = jax.nn.sigmoid(g0_ref[:, cs].astype(F32)) * _dot(a, wb_scr[0, :, cs])
        m = m + jax.nn.sigmoid(g1_ref[:, cs].astype(F32)) * _dot(b, wb_scr[1, :, cs])
        o_ref[:, cs] = m.astype(o_ref.dtype)
    for src, dst in zip(cast_in, cast_out):
        dst[...] = src[...].astype(BF16)


def _merge(z, a_p, a_s, b_p, b_s, w_branch, to_cast, tm, tn):
    m = z.shape[0]
    d = w_branch.shape[2]
    bw = a_p.shape[1]
    npt = a_p.shape[0] // tm
    n_i = m // tm
    cast_specs = _cast_specs(to_cast, d // tn, n_i)
    pspec =pl.BlockSpec((tm, bw), lambda j, i: (jnp.minimum(i, npt - 1), 0))
    sspec = pl.BlockSpec((tm, bw), lambda j, i: (jnp.maximum(i - npt, 0), 0))
    gate = lambda n: pl.BlockSpec((tm, tn), lambda j, i, n=n: (i, (MAIN_COLS + n * d) // tn + j))
    return pl.pallas_call(
        functools.partial(_merge_kernel, n_prompt_tiles=npt, n_casts=len(to_cast)),
        grid=(d // tn, n_i),
        in_specs=[
            gate(0), gate(1),
            pspec, sspec, pspec, sspec,
            pl.BlockSpec((1, bw, tn), lambda j, i: (0, 0, j)),
            pl.BlockSpec((1, bw, tn), lambda j, i: (1, 0, j)),
        ] + cast_specs,
        out_specs=[pl.BlockSpec((tm, tn), lambda j, i: (i, j))] + cast_specs,
        out_shape=[jax.ShapeDtypeStruct((m, d), BF16)]
        + [jax.ShapeDtypeStruct(w.shape, BF16) for w, _ in to_cast],
        scratch_shapes=[pltpu.VMEM((2, bw, tn), BF16)],
        compiler_params=_params(2),
        name="merge",
    )(z, z, a_p, a_s, b_p, b_s, w_branch, w_branch, *[w for w, _ in to_cast])


def _out_proj_kernel(m_ref, xp_ref, xs_ref, w_ref, ln_ref, x1_ref, h2_ref, *, n_prompt_tiles):
    is_prompt = pl.program_id(0) < n_prompt_tiles
    m = m_ref[...]
    tm, d = x1_ref.shape
    ss = jnp.zeros((tm, 1), F32)
    for c0 in range(0, d, EPILOGUE_COLS):
        cs = slice(c0, c0 + EPILOGUE_COLS)
        x1 = jnp.where(is_prompt, xp_ref[:, cs], xs_ref[:, cs]) + _dot(m, w_ref[:, cs])
        x1_ref[:, cs] = x1
        ss = ss + jnp.sum(x1 * x1, axis=-1, keepdims=True)
    inv = lax.rsqrt(ss * (1.0 / d) + EPS)
    for c0 in range(0, d, EPILOGUE_COLS):
        cs = slice(c0, c0 + EPILOGUE_COLS)
        h2_ref[:, cs] = (x1_ref[:, cs] * inv * ln_ref[:, cs]).astype(BF16)


def _out_proj(merged, xp, xs, w_out, ln2, tm):
    m, d = merged.shape
    npt = xp.shape[0] // tm
    return pl.pallas_call(
        functools.partial(_out_proj_kernel, n_prompt_tiles=npt),
        grid=(m // tm,),
        in_specs=[
            pl.BlockSpec((tm, d), lambda i: (i, 0)),
            pl.BlockSpec((tm, d), lambda i: (jnp.minimum(i, npt - 1), 0)),
            pl.BlockSpec((tm, d), lambda i: (jnp.maximum(i - npt, 0), 0)),
            pl.BlockSpec((d, d), lambda i: (0, 0), pipeline_mode=pl.Buffered(1)),
            pl.BlockSpec((1, d), lambda i: (0, 0)),
        ],
        out_specs=[pl.BlockSpec((tm, d), lambda i: (i, 0)),
                   pl.BlockSpec((tm, d), lambda i: (i, 0))],
        out_shape=[jax.ShapeDtypeStruct((m, d), F32), jax.ShapeDtypeStruct((m, d), BF16)],
        compiler_params=_params(1),
        name="out_proj",
    )(merged, xp, xs, w_out, ln2)


def _ffn_up_kernel(h_ref, wg_ref, wu_ref, *rest, n_casts):
    cast_in, (o_ref, *cast_out), (w_scr,) = (
        rest[:n_casts], rest[n_casts:2 * n_casts + 1], rest[2 * n_casts + 1:])

    @pl.when(pl.program_id(1) == 0)
    def _():
        w_scr[0] = wg_ref[...].astype(BF16)
        w_scr[1] = wu_ref[...].astype(BF16)

    h = h_ref[...]
    for c0 in range(0, o_ref.shape[1], MERGE_COLS):
        cs = slice(c0, c0 + MERGE_COLS)
        gate = _dot(h, w_scr[0, :, cs])
        o_ref[:, cs] = (jax.nn.silu(gate) * _dot(h, w_scr[1, :, cs])).astype(o_ref.dtype)
    for src, dst in zip(cast_in, cast_out):
        dst[...] = src[...].astype(BF16)


def _ffn_up(h2, w_gu, to_cast, tm, tf):
    m, d = h2.shape
    dff = w_gu.shape[1] // 2
    nj = dff // tf
    cast_specs = _cast_specs(to_cast, nj, m // tm)
    return pl.pallas_call(
        functools.partial(_ffn_up_kernel, n_casts=len(to_cast)),
        grid=(nj, m // tm),
        in_specs=[
            pl.BlockSpec((tm, d), lambda j, i: (i, 0)),
            pl.BlockSpec((d, tf), lambda j, i: (0, j)),
            pl.BlockSpec((d, tf), lambda j, i: (0, nj + j)),
        ] + cast_specs,
        out_specs=[pl.BlockSpec((tm, tf), lambda j, i: (i, j))] + cast_specs,
        out_shape=[jax.ShapeDtypeStruct((m, dff), BF16)]
        + [jax.ShapeDtypeStruct(w.shape, BF16) for w, _ in to_cast],
        scratch_shapes=[pltpu.VMEM((2, d, tf), BF16)],
        compiler_params=_params(2, VMEM_LIMIT_LARGE),
        name="ffn_up",
    )(h2, w_gu, w_gu, *[w for w, _ in to_cast])


def _ffn_down_kernel(a_ref, w_ref, x_ref, o_ref):
    a = a_ref[...]
    for c0 in range(0, o_ref.shape[1], MERGE_COLS):
        cs = slice(c0, c0 + MERGE_COLS)
        o_ref[:, cs] = x_ref[:, cs] + _dot(a, w_ref[:, cs])


def _ffn_down(act, w_down, x1, tm, tn):
    m, dff = act.shape
    d = w_down.shape[1]
    return pl.pallas_call(
        _ffn_down_kernel,
        grid=(d // tn, m // tm),
        in_specs=[
            pl.BlockSpec((tm, dff), lambda j, i: (i, 0)),
            pl.BlockSpec((dff, tn), lambda j, i: (0, j)),
            pl.BlockSpec((tm, tn), lambda j, i: (i, j)),
        ],
        out_specs=pl.BlockSpec((tm, tn), lambda j, i: (i, j)),
        out_shape=jax.ShapeDtypeStruct((m, d), F32),
        compiler_params=_params(2),
        name="ffn_down",
    )(act, w_down, x1)


def _final_kernel(x_ref, pp_ref, ps_ref, wpg_ref, wple_ref, ln3_ref, lnf_ref, yp_ref, ys_ref,
                  x3_scr, *, n_prompt_tiles):
    i = pl.program_id(0)
    is_prompt = i < n_prompt_tiles
    tm, d = x_ref.shape
    x2 = x_ref[...]
    h3 = (x2 * ln3_ref[...]).astype(BF16)
    inv3 = lax.rsqrt(jnp.mean(x2 * x2, axis=-1, keepdims=True) + EPS)
    p = jnp.where(is_prompt, pp_ref[...], ps_ref[...]).astype(BF16)
    ss = jnp.zeros((tm, 1), F32)
    for c0 in range(0, d, EPILOGUE_COLS):
        cs = slice(c0, c0 + EPILOGUE_COLS)
        gate = jax.nn.sigmoid(inv3 * _dot(h3, wpg_ref[:, cs]))
        x3 = x_ref[:, cs] + gate * _dot(p, wple_ref[:, cs])
        x3_scr[:, cs] = x3
        ss = ss + jnp.sum(x3 * x3, axis=-1, keepdims=True)
    inv = lax.rsqrt(ss * (1.0 / d) + EPS)

    def write(y_ref):
        for c0 in range(0, d, EPILOGUE_COLS):
            cs = slice(c0, c0 + EPILOGUE_COLS)
            y = x3_scr[:, cs] * inv * lnf_ref[:, cs]
            if len(y_ref.shape) == 3:
                y_ref[:, :, cs] = y.reshape(y_ref.shape[0], y_ref.shape[1], EPILOGUE_COLS)
            else:
                y_ref[:, cs] = y

    @pl.when(is_prompt)
    def _():
        write(yp_ref)

    @pl.when(jnp.logical_not(is_prompt))
    def _():
        write(ys_ref)


def _final(x2, pp, ps, w_pg, w_ple, ln3, ln_f, sample_shape, tm):
    m, d = x2.shape
    npr, ns = pp.shape[0], ps.shape[0]
    pd = pp.shape[1]
    npt = npr // tm
    nb, dseq = sample_shape
    assert nb * dseq == ns == tm
    pidx = lambda i: (jnp.minimum(i, npt - 1), 0)
    sidx = lambda i: (jnp.maximum(i - npt, 0), 0)
    return pl.pallas_call(
        functools.partial(_final_kernel, n_prompt_tiles=npt),
        grid=(m // tm,),
        in_specs=[
            pl.BlockSpec((tm, d), lambda i: (i, 0)),
            pl.BlockSpec((tm, pd), pidx),
            pl.BlockSpec((tm, pd), sidx),
            pl.BlockSpec((d, d), lambda i: (0, 0), pipeline_mode=pl.Buffered(1)),
            pl.BlockSpec((pd, d), lambda i: (0, 0)),
            pl.BlockSpec((1, d), lambda i: (0, 0)),
            pl.BlockSpec((1, d), lambda i: (0, 0)),
        ],
        out_specs=[pl.BlockSpec((tm, d), pidx),
                   pl.BlockSpec((nb, dseq, d), lambda i: (0, 0, 0))],
        out_shape=[jax.ShapeDtypeStruct((npr, d), F32), jax.ShapeDtypeStruct((nb, dseq, d), F32)],
        scratch_shapes=[pltpu.VMEM((tm, d), F32)],
        compiler_params=_params(1),
        name="final",
    )(x2, pp, ps, w_pg, w_ple, ln3, ln_f)


def kernel(x_prompt, x_sample, state_hgrn, state_gla, p_prompt, p_sample, hg_lb, ln1, w_in, hg_norm,
           gla_w_gk, gla_b_gk, gla_norm, w_branch, w_out, ln2, w_gu, w_down, ln3, w_ple, w_pg, ln_f):
    batch, seq, d = x_prompt.shape
    nb, dseq, _ = x_sample.shape
    depth = w_in.shape[0]
    assert depth == 1, "single-layer step"
    npr, ns = batch * seq, nb * dseq
    xp = x_prompt.reshape(npr, d)
    row = lambda v: v.reshape(1, -1)

    w_in0 = jnp.swapaxes(w_in[0], 0, 1)
    wgk = jnp.pad(gla_w_gk[0], ((0, LANE - GLA_GATE_RANK), (0, 0)))
    bgk = row(gla_b_gk[0])

    m = npr + ns
    tm_split, tm_quarter, tm_sixteenth = ns, m // 4, m // 16
    assert npr % tm_split == 0 and m % 64 == 0

    h1, glr, xs = _prep(xp, x_sample, row(ln1[0]), w_in0, tm=tm_split)
    z = _in_proj(h1, w_in0, n_cols=MAIN_COLS + 2 * d, tm=tm_quarter, tn=1024)

    a_p, s_hp = _hgrn_prompt(z, hg_lb, row(hg_norm[0]), batch, seq, hpb=HG_HEADS)
    b_p, s_gp = _gla_prompt(z, glr, wgk, bgk, row(gla_norm[0]), batch, seq, hpb=GLA_HEADS)
    a_s, b_s, s_hs, s_gs = _sample_scan(z, glr, state_hgrn, state_gla, hg_lb, row(hg_norm[0]),
                                        row(gla_norm[0]), wgk, bgk, row0=npr, nb=nb, seq=dseq, bb=8)

    merged, w_out_b = _merge(z, a_p, a_s, b_p, b_s, w_branch[0], to_cast=[(w_out[0], 64)],
                             tm=tm_split, tn=1024)
    x1, h2 = _out_proj(merged, xp, xs, w_out_b, row(ln2[0]), tm=tm_split)
    act, w_down_b, w_pg_b, w_ple_b = _ffn_up(
        h2, w_gu[0], to_cast=[(w_down[0], 128), (w_pg[0], 64), (w_ple[0], 32)],
        tm=tm_quarter, tf=512)
    x2 = _ffn_down(act, w_down_b, x1, tm=tm_sixteenth, tn=1024)
    yp, ys = _final(x2, p_prompt[0].reshape(npr, -1), p_sample[0].reshape(ns, -1),
                    w_pg_b, w_ple_b, row(ln3[0]), row(ln_f), sample_shape=(nb, dseq), tm=tm_split)

    return (yp.reshape(batch, seq, d), ys,
            s_hp[None], s_gp[None], s_hs, s_gs)
```

```python
import functools

import jax
import jax.numpy as jnp
from jax import lax
from jax.experimental import pallas as pl
from jax.experimental.pallas import tpu as pltpu

F32 = jnp.float32
BF16 = jnp.bfloat16

EPS = 1e-6
LANE = 128
HG_HEADS, HG_K, HG_V = 8, 128, 128
GLA_HEADS, GLA_K, GLA_V = 4, 128, 256
GLA_GATE_RANK = 16
GLA_GATE_NORM = 16.0
CHUNK = 32
MAIN_COLS = 7168
VMEM_LIMIT = 56 * 1024 * 1024
VMEM_LIMIT_LARGE = 60 * 1024 * 1024
EPILOGUE_COLS = 256
MERGE_COLS = 256

NT = (((1,), (1,)), ((), ()))
TN = (((0,), (0,)), ((), ()))


def _params(n_axes, vmem=VMEM_LIMIT):
    return pltpu.CompilerParams(dimension_semantics=("arbitrary",) * n_axes,
                                vmem_limit_bytes=vmem)


def _dot(a, b):
    return jnp.dot(a, b, preferred_element_type=F32)


def _rms(x, g):
    return x * lax.rsqrt(jnp.mean(x * x, axis=-1, keepdims=True) + EPS) * g


def _prep_kernel(xp_ref, xs_ref, ln_ref, wglr_ref, h_ref, glr_ref, xs2d_ref, *, n_prompt_tiles):
    i = pl.program_id(0)

    d = wglr_ref.shape[1]
    wglr = jnp.concatenate([wglr_ref[...], jnp.zeros((LANE - GLA_GATE_RANK, d), F32)], axis=0)
    wglr = wglr.astype(BF16)

    def body(x):
        h = _rms(x, ln_ref[...]).astype(BF16)
        h_ref[...] = h
        glr_ref[...] = lax.dot_general(h, wglr, NT, preferred_element_type=F32)

    @pl.when(i < n_prompt_tiles)
    def _():
        body(xp_ref[...])

    @pl.when(i >= n_prompt_tiles)
    def _():
        x = xs_ref[...].reshape(xs2d_ref.shape)
        xs2d_ref[...] = x
        body(x)


def _prep(xp, xs3, ln1, w_in_t, tm):
    npr, d = xp.shape
    nb, dseq, _ = xs3.shape
    ns = nb * dseq
    assert ns == tm
    npt = npr // tm
    m = npr + ns
    return pl.pallas_call(
        functools.partial(_prep_kernel, n_prompt_tiles=npt),
        grid=(npt + 1,),
        in_specs=[
            pl.BlockSpec((tm, d), lambda i: (jnp.minimum(i, npt - 1), 0)),
            pl.BlockSpec((nb, dseq, d), lambda i: (0, 0, 0)),
            pl.BlockSpec((1, d), lambda i: (0, 0)),
            pl.BlockSpec((GLA_GATE_RANK, d), lambda i: (MAIN_COLS // GLA_GATE_RANK, 0)),
        ],
        out_specs=[
            pl.BlockSpec((tm, d), lambda i: (i, 0)),
            pl.BlockSpec((tm, LANE), lambda i: (i, 0)),
            pl.BlockSpec((tm, d), lambda i: (0, 0)),
        ],
        out_shape=[jax.ShapeDtypeStruct((m, d), BF16),
                   jax.ShapeDtypeStruct((m, LANE), F32),
                   jax.ShapeDtypeStruct((ns, d), F32)],
        compiler_params=_params(1),
        name="prep",
    )(xp, xs3, ln1, w_in_t)


TRANSPOSE_ROWS = 256


def _stage_transposed(w_t, dst_ref, n=None):
    for r in range(0, w_t.shape[0], TRANSPOSE_ROWS):
        blk = w_t[r:r + TRANSPOSE_ROWS, :].T.astype(BF16)
        if n is None:
            dst_ref[:, r:r + TRANSPOSE_ROWS] = blk
        else:
            dst_ref[n, :, r:r + TRANSPOSE_ROWS] = blk


def _in_proj_kernel(h_ref, w_ref, wx_ref, z_ref, wbf_ref, *, n_aligned):
    j = pl.program_id(0)
    stage = pl.program_id(1) == 0
    tn = z_ref.shape[1]

    @pl.when(stage & (j < n_aligned))
    def _():
        _stage_transposed(w_ref[...], wbf_ref)

    @pl.when(stage & (j >= n_aligned))
    def _():
        cat = jnp.concatenate([w_ref[...], wx_ref[...]], axis=0)
        _stage_transposed(cat[GLA_GATE_RANK:GLA_GATE_RANK + tn, :], wbf_ref)

    h = h_ref[...]
    for c0 in range(0, tn, MERGE_COLS):
        cs = slice(c0, c0 + MERGE_COLS)
        z_ref[:, cs] = _dot(h, wbf_ref[:, cs]).astype(z_ref.dtype)


def _in_proj(h1, w_in_t, n_cols, tm, tn):
    m, d = h1.shape
    rk = GLA_GATE_RANK
    return pl.pallas_call(
        functools.partial(_in_proj_kernel, n_aligned=MAIN_COLS // tn),
        grid=(n_cols // tn, m // tm),
        in_specs=[
            pl.BlockSpec((tm, d), lambda j, i: (i, 0)),
            pl.BlockSpec((tn, d), lambda j, i: (j, 0)),
            pl.BlockSpec((rk, d), lambda j, i: ((j + 1) * (tn // rk), 0)),
        ],
        out_specs=pl.BlockSpec((tm, tn), lambda j, i: (i, j)),
        out_shape=jax.ShapeDtypeStruct((m, n_cols), BF16),
        scratch_shapes=[pltpu.VMEM((d, tn), BF16)],
        compiler_params=_params(2, VMEM_LIMIT_LARGE),
        name="in_proj",
    )(h1, w_in_t, w_in_t)


def _cumsum_rows(x, length):
    row = lax.broadcasted_iota(jnp.int32, x.shape, 0) % length
    s = 1
    while s < length:
        x = x + jnp.where(row >= s, pltpu.roll(x, s, axis=0), 0.0)
        s *= 2
    return x


def _forget_lower_bound(lb_rows):
    mx = jnp.max(lb_rows, axis=0, keepdims=True)
    e = jnp.exp(lb_rows - mx)
    return e[0:1, :] / jnp.sum(e, axis=0, keepdims=True)


def _col_broadcast(row):
    return jnp.broadcast_to(row, (LANE, LANE)).T


def _gated_out(o, g, gain):
    return (_rms(o, gain) * jax.nn.silu(g)).astype(BF16)


SLAB = 256
CHUNKS_PER_SLAB = SLAB // CHUNK
SUBLANES = 8
DEC_ROWS = -(-CHUNKS_PER_SLAB // SUBLANES) * SUBLANES


def _slab_rows(g):
    return pl.ds(pl.multiple_of(g * SLAB, SLAB), SLAB)


def _per_chunk_rows(rows_1w):
    w = rows_1w[0].shape[1]
    return jnp.concatenate([jnp.broadcast_to(r, (CHUNK, w)) for r in rows_1w], axis=0)


def _intra_slab(slot, q, k, logf, kdim, vdim, v_slab, qin_scr, kout_scr, dec_scr, o_scr):
    heads = q.shape[1] // kdim
    cum = _cumsum_rows(logf, CHUNK)
    ref_rows = [cum[c * CHUNK + CHUNK // 2:c * CHUNK + CHUNK // 2 + 1, :]
                for c in range(CHUNKS_PER_SLAB)]
    last_rows = [cum[(c + 1) * CHUNK - 1:(c + 1) * CHUNK, :] for c in range(CHUNKS_PER_SLAB)]
    ref = _per_chunk_rows(ref_rows)
    qe = q * jnp.exp(cum - ref)
    ke = k * jnp.exp(ref - cum)
    qin_scr[slot] = (q * jnp.exp(cum)).astype(BF16)
    kout_scr[slot] = (k * jnp.exp(_per_chunk_rows(last_rows) - cum)).astype(BF16)
    for c in range(CHUNKS_PER_SLAB):
        dec_scr[slot, c:c + 1, :] = jnp.exp(last_rows[c])
    qe = qe.astype(BF16)
    ke = ke.astype(BF16)
    r_i = lax.broadcasted_iota(jnp.int32, (SLAB, SLAB), 0)
    c_i = lax.broadcasted_iota(jnp.int32, (SLAB, SLAB), 1)
    causal = (r_i // CHUNK == c_i // CHUNK) & (c_i <= r_i)
    for h in range(heads):
        ks = slice(h * kdim, (h + 1) * kdim)
        vs = slice(h * vdim, (h + 1) * vdim)
        att = lax.dot_general(qe[:, ks], ke[:, ks], NT, preferred_element_type=F32)
        o_scr[slot, :, vs] = _dot(jnp.where(causal, att, 0.0).astype(BF16), v_slab[:, vs])


def _state_chunk(g, slot, c, heads, kdim, vdim, v_ref, g_ref, gain, qin_scr, kout_scr, dec_scr,
                 o_scr, st_scr, o_ref):
    cr = slice(c * CHUNK, (c + 1) * CHUNK)
    rows = pl.ds(pl.multiple_of(g * SLAB + c * CHUNK, CHUNK), CHUNK)
    drow = dec_scr[slot, c:c + 1, :]
    for h in range(heads):
        ks = slice(h * kdim, (h + 1) * kdim)
        vs = slice(h * vdim, (h + 1) * vdim)
        st = st_scr[h]
        o = o_scr[slot, cr, vs] + _dot(qin_scr[slot, cr, ks], st.astype(BF16))
        kv = lax.dot_general(kout_scr[slot, cr, ks], v_ref[rows, vs], TN,
                             preferred_element_type=F32)
        dcol = _col_broadcast(drow[:, ks])
        if vdim != LANE:
            dcol = jnp.concatenate([dcol] * (vdim // LANE), axis=1)
        st_scr[h] = st * dcol + kv
        o_ref[rows, vs] = _gated_out(o, g_ref[rows, vs].astype(F32), gain)


def _run_pipelined(n_slabs, intra, state, st_scr, s_ref):
    assert n_slabs % 2 == 0 and n_slabs >= 4

    def overlapped(g_intra, slot_intra, g_state, slot_state):
        intra(g_intra, slot_intra)
        for c in range(CHUNKS_PER_SLAB):
            state(g_state, slot_state, c)

    st_scr[...] = jnp.zeros_like(st_scr)
    intra(0, 0)

    def pair(i, carry):
        overlapped(2 * i + 1, 1, 2 * i, 0)
        overlapped(2 * i + 2, 0, 2 * i + 1, 1)
        return carry

    lax.fori_loop(0, n_slabs // 2 - 1, pair, 0)
    overlapped(n_slabs - 1, 1, n_slabs - 2, 0)
    for c in range(CHUNKS_PER_SLAB):
        state(n_slabs - 1, 1, c)
    for h in range(st_scr.shape[0]):
        s_ref[0, h] = st_scr[h]


def _hgrn_prompt_kernel(lb_ref, gain_ref, q_ref, f_ref, i_ref, g_ref, o_ref, s_ref,
                        qin_scr, kout_scr, dec_scr, o_scr, st_scr, *, hpb):
    lb = _forget_lower_bound(lb_ref[...])
    gain = gain_ref[...]

    def intra(g, slot):
        rows = _slab_rows(g)
        q = jax.nn.silu(q_ref[rows, :].astype(F32)) * (HG_K ** -0.5)
        sig = jax.nn.sigmoid(f_ref[rows, :].astype(F32))
        logf = jnp.log(lb + (1.0 - lb) * sig)
        k = (1.0 - lb) * (1.0 - sig)
        _intra_slab(slot, q, k, logf, HG_K, HG_V, i_ref[rows, :], qin_scr, kout_scr, dec_scr,
                    o_scr)

    def state(g, slot, c):
        _state_chunk(g, slot, c, hpb, HG_K, HG_V, i_ref, g_ref, gain, qin_scr, kout_scr, dec_scr,
                     o_scr, st_scr, o_ref)

    _run_pipelined(q_ref.shape[0] // SLAB, intra, state, st_scr, s_ref)


def _hgrn_prompt(z, hg_lb, gain, batch, seq, hpb):
    w = hpb * HG_K
    sec = (HG_HEADS * HG_K) // w
    zspec = lambda s: pl.BlockSpec((seq, w), lambda b, g, s=s: (b, s * sec + g))
    return pl.pallas_call(
        functools.partial(_hgrn_prompt_kernel, hpb=hpb),
        grid=(batch, HG_HEADS // hpb),
        in_specs=[
            pl.BlockSpec((hg_lb.shape[0], w), lambda b, g: (0, g)),
            pl.BlockSpec((1, HG_V), lambda b, g: (0, 0)),
            zspec(0), zspec(1), zspec(2), zspec(3),
        ],
        out_specs=[
            pl.BlockSpec((seq, w), lambda b, g: (b, g)),
            pl.BlockSpec((1, hpb, HG_K, HG_V), lambda b, g: (b, g, 0, 0)),
        ],
        out_shape=[jax.ShapeDtypeStruct((batch * seq, HG_HEADS * HG_V), BF16),
                   jax.ShapeDtypeStruct((batch, HG_HEADS, HG_K, HG_V), F32)],
        scratch_shapes=[pltpu.VMEM((2, SLAB, w), BF16), pltpu.VMEM((2, SLAB, w), BF16),
                        pltpu.VMEM((2, DEC_ROWS, w), F32), pltpu.VMEM((2, SLAB, w), F32),
                        pltpu.VMEM((hpb, HG_K, HG_V), F32)],
        compiler_params=_params(2),
        name="hgrn_prompt",
    )(hg_lb, gain, z, z, z, z)


def _gla_prompt_kernel(gain_ref, wgk_ref, bgk_ref, glr_ref, q_ref, k_ref, v_ref, g_ref,
                       o_ref, s_ref, qin_scr, kout_scr, dec_scr, o_scr, st_scr, *, hpb):
    gain = gain_ref[...]
    wgk = wgk_ref[...].astype(BF16)
    bgk = bgk_ref[...]

    def intra(g, slot):
        rows = _slab_rows(g)
        gk = _dot(glr_ref[rows, :].astype(BF16), wgk) + bgk
        logf = jax.nn.log_sigmoid(gk) / GLA_GATE_NORM
        q = q_ref[rows, :].astype(F32) * (GLA_K ** -0.5)
        _intra_slab(slot, q, k_ref[rows, :].astype(F32), logf, GLA_K, GLA_V, v_ref[rows, :],
                    qin_scr, kout_scr, dec_scr, o_scr)

    def state(g, slot, c):
        _state_chunk(g, slot, c, hpb, GLA_K, GLA_V, v_ref, g_ref, gain, qin_scr, kout_scr, dec_scr,
                     o_scr, st_scr, o_ref)

    _run_pipelined(q_ref.shape[0] // SLAB, intra, state, st_scr, s_ref)


def _gla_prompt(z, glr, wgk, bgk, gain, batch, seq, hpb):
    kw, vw = hpb * GLA_K, hpb * GLA_V
    q0 = (4 * HG_HEADS * HG_K) // kw
    k0 = q0 + (GLA_HEADS * GLA_K) // kw
    v0 = (4 * HG_HEADS * HG_K + 2 * GLA_HEADS * GLA_K) // vw
    g0 = v0 + (GLA_HEADS * GLA_V) // vw
    return pl.pallas_call(
        functools.partial(_gla_prompt_kernel, hpb=hpb),
        grid=(batch, GLA_HEADS // hpb),
        in_specs=[
            pl.BlockSpec((1, GLA_V), lambda b, g: (0, 0)),
            pl.BlockSpec((LANE, kw), lambda b, g: (0, g)),
            pl.BlockSpec((1, kw), lambda b, g: (0, g)),
            pl.BlockSpec((seq, LANE), lambda b, g: (b, 0)),
            pl.BlockSpec((seq, kw), lambda b, g: (b, q0 + g)),
            pl.BlockSpec((seq, kw), lambda b, g: (b, k0 + g)),
            pl.BlockSpec((seq, vw), lambda b, g: (b, v0 + g)),
            pl.BlockSpec((seq, vw), lambda b, g: (b, g0 + g)),
        ],
        out_specs=[
            pl.BlockSpec((seq, vw), lambda b, g: (b, g)),
            pl.BlockSpec((1, hpb, GLA_K, GLA_V), lambda b, g: (b, g, 0, 0)),
        ],
        out_shape=[jax.ShapeDtypeStruct((batch * seq, GLA_HEADS * GLA_V), BF16),
                   jax.ShapeDtypeStruct((batch, GLA_HEADS, GLA_K, GLA_V), F32)],
        scratch_shapes=[pltpu.VMEM((2, SLAB, kw), BF16), pltpu.VMEM((2, SLAB, kw), BF16),
                        pltpu.VMEM((2, DEC_ROWS, kw), F32), pltpu.VMEM((2, SLAB, vw), F32),
                        pltpu.VMEM((hpb, GLA_K, GLA_V), F32)],
        compiler_params=_params(2),
        name="gla_prompt",
    )(gain, wgk, bgk, glr, z, z, z, z)


def _seg_pick(x, seg, idx):
    n = x.shape[0]
    rowmod = lax.broadcasted_iota(jnp.int32, x.shape, 0) % seg
    out = x
    for m in range(seg):
        if m != idx:
            out = jnp.where(rowmod == m, pltpu.roll(x, (m - idx) % n, axis=0), out)
    return out


def _sample_branch(q, k, v_b, logf, g, gain, s_ref, ns_ref, o_ref, o_scr, *, heads, kdim, vdim,
                   seq, pairs):
    tile = 2 * seq
    cum = _cumsum_rows(logf, seq)
    ref = _seg_pick(cum, seq, seq // 2)
    last = _seg_pick(cum, seq, seq - 1)
    qe = (q * jnp.exp(cum - ref)).astype(BF16)
    ke = (k * jnp.exp(ref - cum)).astype(BF16)
    q_in = (q * jnp.exp(cum)).astype(BF16)
    k_out = k * jnp.exp(last - cum)
    dec = jnp.exp(last)
    r_i = lax.broadcasted_iota(jnp.int32, (tile, tile), 0)
    c_i = lax.broadcasted_iota(jnp.int32, (tile, tile), 1)
    amask = (r_i // seq == c_i // seq) & (c_i <= r_i)
    half = lax.broadcasted_iota(jnp.int32, (tile, 1), 0) // seq
    for p in range(pairs):
        rs = slice(p * tile, (p + 1) * tile)
        for h in range(heads):
            ks = slice(h * kdim, (h + 1) * kdim)
            vs = slice(h * vdim, (h + 1) * vdim)
            v_t = v_b[rs, vs]
            att = lax.dot_general(qe[rs, ks], ke[rs, ks], NT, preferred_element_type=F32)
            o = _dot(jnp.where(amask, att, 0.0).astype(BF16), v_t)
            for bi in range(2):
                b = 2 * p + bi
                s0 = s_ref[0, b, h]
                o = o + jnp.where(half == bi, _dot(q_in[rs, ks], s0.astype(BF16)), 0.0)
                km = jnp.where(half == bi, k_out[rs, ks], 0.0).astype(BF16)
                kv = lax.dot_general(km, v_t, TN, preferred_element_type=F32)
                r0 = p * tile + bi * seq
                dcol = _col_broadcast(dec[r0:r0 + 1, ks])
                if vdim != LANE:
                    dcol = jnp.concatenate([dcol] * (vdim // LANE), axis=1)
                ns_ref[0, b, h] = s0 * dcol + kv
            o_scr[rs, vs] = o
    for h in range(heads):
        vs = slice(h * vdim, (h + 1) * vdim)
        o_ref[:, vs] = _gated_out(o_scr[:, vs], g[:, vs].astype(F32), gain)


def _sample_kernel(lb_ref, hgain_ref, ggain_ref, wgk_ref, bgk_ref,
                   hq_ref, hf_ref, hi_ref, hg_ref, gq_ref, gk_ref, gv_ref, gg_ref, glr_ref,
                   sh_ref, sg_ref,
                   a_ref, b_ref, nsh_ref, nsg_ref, oh_scr, og_scr, *, seq, pairs):
    lb = _forget_lower_bound(lb_ref[...])
    sig = jax.nn.sigmoid(hf_ref[...].astype(F32))
    _sample_branch(
        jax.nn.silu(hq_ref[...].astype(F32)) * (HG_K ** -0.5),
        (1.0 - lb) * (1.0 - sig),
        hi_ref[...],
        jnp.log(lb + (1.0 - lb) * sig),
        hg_ref[...], hgain_ref[...], sh_ref, nsh_ref, a_ref, oh_scr,
        heads=HG_HEADS, kdim=HG_K, vdim=HG_V, seq=seq, pairs=pairs)
    gk = _dot(glr_ref[...].astype(BF16), wgk_ref[...].astype(BF16)) + bgk_ref[...]
    _sample_branch(
        gq_ref[...].astype(F32) * (GLA_K ** -0.5),
        gk_ref[...].astype(F32),
        gv_ref[...],
        jax.nn.log_sigmoid(gk) / GLA_GATE_NORM,
        gg_ref[...], ggain_ref[...], sg_ref, nsg_ref, b_ref, og_scr,
        heads=GLA_HEADS, kdim=GLA_K, vdim=GLA_V, seq=seq, pairs=pairs)


def _sample_scan(z, glr, state_hgrn, state_gla, hg_lb, hgain, ggain, wgk, bgk, row0, nb, seq, bb):
    rows = bb * seq
    rb0 = row0 // rows
    hw, kw, vw = HG_HEADS * HG_K, GLA_HEADS * GLA_K, GLA_HEADS * GLA_V
    gq0 = (4 * hw) // kw
    gv0 = (4 * hw + 2 * kw) // vw
    zs = lambda w, c: pl.BlockSpec((rows, w), lambda i, c=c: (rb0 + i, c))
    const = lambda shape: pl.BlockSpec(shape, lambda i: (0,) * len(shape))
    st = lambda h, k, v: pl.BlockSpec((1, bb, h, k, v), lambda i: (0, i, 0, 0, 0))
    return pl.pallas_call(
        functools.partial(_sample_kernel, seq=seq, pairs=bb // 2),
        grid=(nb // bb,),
        in_specs=[
            const(hg_lb.shape), const((1, HG_V)), const((1, GLA_V)), const((LANE, kw)),
            const((1, kw)),
            zs(hw, 0), zs(hw, 1), zs(hw, 2), zs(hw, 3),
            zs(kw, gq0), zs(kw, gq0 + 1), zs(vw, gv0), zs(vw, gv0 + 1),
            pl.BlockSpec((rows, LANE), lambda i: (rb0 + i, 0)),
            st(HG_HEADS, HG_K, HG_V), st(GLA_HEADS, GLA_K, GLA_V),
        ],
        out_specs=[
            pl.BlockSpec((rows, hw), lambda i: (i, 0)),
            pl.BlockSpec((rows, vw), lambda i: (i, 0)),
            st(HG_HEADS, HG_K, HG_V), st(GLA_HEADS, GLA_K, GLA_V),
        ],
        out_shape=[jax.ShapeDtypeStruct((nb * seq, hw), BF16),
                   jax.ShapeDtypeStruct((nb * seq, vw), BF16),
                   jax.ShapeDtypeStruct(state_hgrn.shape, state_hgrn.dtype),
                   jax.ShapeDtypeStruct(state_gla.shape, state_gla.dtype)],
        scratch_shapes=[pltpu.VMEM((rows, hw), F32), pltpu.VMEM((rows, vw), F32)],
        compiler_params=_params(1),
        name="sample_scan",
    )(hg_lb, hgain, ggain, wgk, bgk, z, z, z, z, z, z, z, z, glr, state_hgrn, state_gla)


def _cast_specs(to_cast, n_j, n_i):
    specs = []
    for w, slab in to_cast:
        n_slabs = w.shape[0] // slab
        assert w.shape[0] % slab == 0 and n_slabs <= n_j * n_i
        specs.append(pl.BlockSpec(
            (slab, w.shape[1]),
            lambda j, i, n_slabs=n_slabs: (jnp.minimum(j * n_i + i, n_slabs - 1), 0)))
    return specs


def _merge_kernel(g0_ref, g1_ref, ap_ref, as_ref, bp_ref, bs_ref, wb0_ref, wb1_ref, *rest,
                  n_prompt_tiles, n_casts):
    cast_in, (o_ref, *cast_out), (wb_scr,) = (
        rest[:n_casts], rest[n_casts:2 * n_casts + 1], rest[2 * n_casts + 1:])
    tn = o_ref.shape[1]
    is_prompt = pl.program_id(1) < n_prompt_tiles

    @pl.when(pl.program_id(1) == 0)
    def _():
        wb_scr[0] = wb0_ref[0].astype(BF16)
        wb_scr[1] = wb1_ref[0].astype(BF16)

    a = jnp.where(is_prompt, ap_ref[...], as_ref[...])
    b = jnp.where(is_prompt, bp_ref[...], bs_ref[...])
    for c0 in range(0, tn, MERGE_COLS):
        cs = slice(c0, c0 + MERGE_COLS)
        m = jax.nn.sigmoid(g0_ref[:, cs].astype(F32)) * _dot(a, wb_scr[0, :, cs])
        m = m + jax.nn.sigmoid(g1_ref[:, cs].astype(F32)) * _dot(b, wb_scr[1, :, cs])
        o_ref[:, cs] = m.astype(o_ref.dtype)
    for src, dst in zip(cast_in, cast_out):
        dst[...] = src[...].astype(BF16)


def _merge(z, a_p, a_s, b_p, b_s, w_branch, to_cast, tm, tn):
    m = z.shape[0]
    d = w_branch.shape[2]
    bw = a_p.shape[1]
    npt = a_p.shape[0] // tm
    n_i = m // tm
    cast_specs = _cast_specs(to_cast, d // tn, n_i)
    pspec =pl.BlockSpec((tm, bw), lambda j, i: (jnp.minimum(i, npt - 1), 0))
    sspec = pl.BlockSpec((tm, bw), lambda j, i: (jnp.maximum(i - npt, 0), 0))
    gate = lambda n: pl.BlockSpec((tm, tn), lambda j, i, n=n: (i, (MAIN_COLS + n * d) // tn + j))
    return pl.pallas_call(
        functools.partial(_merge_kernel, n_prompt_tiles=npt, n_casts=len(to_cast)),
        grid=(d // tn, n_i),
        in_specs=[
            gate(0), gate(1),
            pspec, sspec, pspec, sspec,
            pl.BlockSpec((1, bw, tn), lambda j, i: (0, 0, j)),
            pl.BlockSpec((1, bw, tn), lambda j, i: (1, 0, j)),
        ] + cast_specs,
        out_specs=[pl.BlockSpec((tm, tn), lambda j, i: (i, j))] + cast_specs,
        out_shape=[jax.ShapeDtypeStruct((m, d), BF16)]
        + [jax.ShapeDtypeStruct(w.shape, BF16) for w, _ in to_cast],
        scratch_shapes=[pltpu.VMEM((2, bw, tn), BF16)],
        compiler_params=_params(2),
        name="merge",
    )(z, z, a_p, a_s, b_p, b_s, w_branch, w_branch, *[w for w, _ in to_cast])


def _out_proj_kernel(m_ref, xp_ref, xs_ref, w_ref, ln_ref, x1_ref, h2_ref, *, n_prompt_tiles):
    is_prompt = pl.program_id(0) < n_prompt_tiles
    m = m_ref[...]
    tm, d = x1_ref.shape
    ss = jnp.zeros((tm, 1), F32)
    for c0 in range(0, d, EPILOGUE_COLS):
        cs = slice(c0, c0 + EPILOGUE_COLS)
        x1 = jnp.where(is_prompt, xp_ref[:, cs], xs_ref[:, cs]) + _dot(m, w_ref[:, cs])
        x1_ref[:, cs] = x1
        ss = ss + jnp.sum(x1 * x1, axis=-1, keepdims=True)
    inv = lax.rsqrt(ss * (1.0 / d) + EPS)
    for c0 in range(0, d, EPILOGUE_COLS):
        cs = slice(c0, c0 + EPILOGUE_COLS)
        h2_ref[:, cs] = (x1_ref[:, cs] * inv * ln_ref[:, cs]).astype(BF16)


def _out_proj(merged, xp, xs, w_out, ln2, tm):
    m, d = merged.shape
    npt = xp.shape[0] // tm
    return pl.pallas_call(
        functools.partial(_out_proj_kernel, n_prompt_tiles=npt),
        grid=(m // tm,),
        in_specs=[
            pl.BlockSpec((tm, d), lambda i: (i, 0)),
            pl.BlockSpec((tm, d), lambda i: (jnp.minimum(i, npt - 1), 0)),
            pl.BlockSpec((tm, d), lambda i: (jnp.maximum(i - npt, 0), 0)),
            pl.BlockSpec((d, d), lambda i: (0, 0), pipeline_mode=pl.Buffered(1)),
            pl.BlockSpec((1, d), lambda i: (0, 0)),
        ],
        out_specs=[pl.BlockSpec((tm, d), lambda i: (i, 0)),
                   pl.BlockSpec((tm, d), lambda i: (i, 0))],
        out_shape=[jax.ShapeDtypeStruct((m, d), F32), jax.ShapeDtypeStruct((m, d), BF16)],
        compiler_params=_params(1),
        name="out_proj",
    )(merged, xp, xs, w_out, ln2)


def _ffn_up_kernel(h_ref, wg_ref, wu_ref, *rest, n_casts):
    cast_in, (o_ref, *cast_out), (w_scr,) = (
        rest[:n_casts], rest[n_casts:2 * n_casts + 1], rest[2 * n_casts + 1:])

    @pl.when(pl.program_id(1) == 0)
    def _():
        w_scr[0] = wg_ref[...].astype(BF16)
        w_scr[1] = wu_ref[...].astype(BF16)

    h = h_ref[...]
    for c0 in range(0, o_ref.shape[1], MERGE_COLS):
        cs = slice(c0, c0 + MERGE_COLS)
        gate = _dot(h, w_scr[0, :, cs])
        o_ref[:, cs] = (jax.nn.silu(gate) * _dot(h, w_scr[1, :, cs])).astype(o_ref.dtype)
    for src, dst in zip(cast_in, cast_out):
        dst[...] = src[...].astype(BF16)


def _ffn_up(h2, w_gu, to_cast, tm, tf):
    m, d = h2.shape
    dff = w_gu.shape[1] // 2
    nj = dff // tf
    cast_specs = _cast_specs(to_cast, nj, m // tm)
    return pl.pallas_call(
        functools.partial(_ffn_up_kernel, n_casts=len(to_cast)),
        grid=(nj, m // tm),
        in_specs=[
            pl.BlockSpec((tm, d), lambda j, i: (i, 0)),
            pl.BlockSpec((d, tf), lambda j, i: (0, j)),
            pl.BlockSpec((d, tf), lambda j, i: (0, nj + j)),
        ] + cast_specs,
        out_specs=[pl.BlockSpec((tm, tf), lambda j, i: (i, j))] + cast_specs,
        out_shape=[jax.ShapeDtypeStruct((m, dff), BF16)]
        + [jax.ShapeDtypeStruct(w.shape, BF16) for w, _ in to_cast],
        scratch_shapes=[pltpu.VMEM((2, d, tf), BF16)],
        compiler_params=_params(2, VMEM_LIMIT_LARGE),
        name="ffn_up",
    )(h2, w_gu, w_gu, *[w for w, _ in to_cast])


def _ffn_down_kernel(a_ref, w_ref, x_ref, o_ref):
    a = a_ref[...]
    for c0 in range(0, o_ref.shape[1], MERGE_COLS):
        cs = slice(c0, c0 + MERGE_COLS)
        o_ref[:, cs] = x_ref[:, cs] + _dot(a, w_ref[:, cs])


def _ffn_down(act, w_down, x1, tm, tn):
    m, dff = act.shape
    d = w_down.shape[1]
    return pl.pallas_call(
        _ffn_down_kernel,
        grid=(d // tn, m // tm),
        in_specs=[
            pl.BlockSpec((tm, dff), lambda j, i: (i, 0)),
            pl.BlockSpec((dff, tn), lambda j, i: (0, j)),
            pl.BlockSpec((tm, tn), lambda j, i: (i, j)),
        ],
        out_specs=pl.BlockSpec((tm, tn), lambda j, i: (i, j)),
        out_shape=jax.ShapeDtypeStruct((m, d), F32),
        compiler_params=_params(2),
        name="ffn_down",
    )(act, w_down, x1)


def _final_kernel(x_ref, pp_ref, ps_ref, wpg_ref, wple_ref, ln3_ref, lnf_ref, yp_ref, ys_ref,
                  x3_scr, *, n_prompt_tiles):
    i = pl.program_id(0)
    is_prompt = i < n_prompt_tiles
    tm, d = x_ref.shape
    x2 = x_ref[...]
    h3 = (x2 * ln3_ref[...]).astype(BF16)
    inv3 = lax.rsqrt(jnp.mean(x2 * x2, axis=-1, keepdims=True) + EPS)
    p = jnp.where(is_prompt, pp_ref[...], ps_ref[...]).astype(BF16)
    ss = jnp.zeros((tm, 1), F32)
    for c0 in range(0, d, EPILOGUE_COLS):
        cs = slice(c0, c0 + EPILOGUE_COLS)
        gate = jax.nn.sigmoid(inv3 * _dot(h3, wpg_ref[:, cs]))
        x3 = x_ref[:, cs] + gate * _dot(p, wple_ref[:, cs])
        x3_scr[:, cs] = x3
        ss = ss + jnp.sum(x3 * x3, axis=-1, keepdims=True)
    inv = lax.rsqrt(ss * (1.0 / d) + EPS)

    def write(y_ref):
        for c0 in range(0, d, EPILOGUE_COLS):
            cs = slice(c0, c0 + EPILOGUE_COLS)
            y = x3_scr[:, cs] * inv * lnf_ref[:, cs]
            if len(y_ref.shape) == 3:
                y_ref[:, :, cs] = y.reshape(y_ref.shape[0], y_ref.shape[1], EPILOGUE_COLS)
            else:
                y_ref[:, cs] = y

    @pl.when(is_prompt)
    def _():
        write(yp_ref)

    @pl.when(jnp.logical_not(is_prompt))
    def _():
        write(ys_ref)


def _final(x2, pp, ps, w_pg, w_ple, ln3, ln_f, sample_shape, tm):
    m, d = x2.shape
    npr, ns = pp.shape[0], ps.shape[0]
    pd = pp.shape[1]
    npt = npr // tm
    nb, dseq = sample_shape
    assert nb * dseq == ns == tm
    pidx = lambda i: (jnp.minimum(i, npt - 1), 0)
    sidx = lambda i: (jnp.maximum(i - npt, 0), 0)
    return pl.pallas_call(
        functools.partial(_final_kernel, n_prompt_tiles=npt),
        grid=(m // tm,),
        in_specs=[
            pl.BlockSpec((tm, d), lambda i: (i, 0)),
            pl.BlockSpec((tm, pd), pidx),
            pl.BlockSpec((tm, pd), sidx),
            pl.BlockSpec((d, d), lambda i: (0, 0), pipeline_mode=pl.Buffered(1)),
            pl.BlockSpec((pd, d), lambda i: (0, 0)),
            pl.BlockSpec((1, d), lambda i: (0, 0)),
            pl.BlockSpec((1, d), lambda i: (0, 0)),
        ],
        out_specs=[pl.BlockSpec((tm, d), pidx),
                   pl.BlockSpec((nb, dseq, d), lambda i: (0, 0, 0))],
        out_shape=[jax.ShapeDtypeStruct((npr, d), F32), jax.ShapeDtypeStruct((nb, dseq, d), F32)],
        scratch_shapes=[pltpu.VMEM((tm, d), F32)],
        compiler_params=_params(1),
        name="final",
    )(x2, pp, ps, w_pg, w_ple, ln3, ln_f)


def kernel(x_prompt, x_sample, state_hgrn, state_gla, p_prompt, p_sample, hg_lb, ln1, w_in, hg_norm,
           gla_w_gk, gla_b_gk, gla_norm, w_branch, w_out, ln2, w_gu, w_down, ln3, w_ple, w_pg, ln_f):
    batch, seq, d = x_prompt.shape
    nb, dseq, _ = x_sample.shape
    depth = w_in.shape[0]
    assert depth == 1, "single-layer step"
    npr, ns = batch * seq, nb * dseq
    xp = x_prompt.reshape(npr, d)
    row = lambda v: v.reshape(1, -1)

    w_in0 = jnp.swapaxes(w_in[0], 0, 1)
    wgk = jnp.pad(gla_w_gk[0], ((0, LANE - GLA_GATE_RANK), (0, 0)))
    bgk = row(gla_b_gk[0])

    m = npr + ns
    tm_split, tm_quarter, tm_sixteenth = ns, m // 4, m // 16
    assert npr % tm_split == 0 and m % 64 == 0

    h1, glr, xs = _prep(xp, x_sample, row(ln1[0]), w_in0, tm=tm_split)
    z = _in_proj(h1, w_in0, n_cols=MAIN_COLS + 2 * d, tm=tm_quarter, tn=1024)

    a_p, s_hp = _hgrn_prompt(z, hg_lb, row(hg_norm[0]), batch, seq, hpb=HG_HEADS)
    b_p, s_gp = _gla_prompt(z, glr, wgk, bgk, row(gla_norm[0]), batch, seq, hpb=GLA_HEADS)
    a_s, b_s, s_hs, s_gs = _sample_scan(z, glr, state_hgrn, state_gla, hg_lb, row(hg_norm[0]),
                                        row(gla_norm[0]), wgk, bgk, row0=npr, nb=nb, seq=dseq, bb=8)

    merged, w_out_b = _merge(z, a_p, a_s, b_p, b_s, w_branch[0], to_cast=[(w_out[0], 64)],
                             tm=tm_split, tn=1024)
    x1, h2 = _out_proj(merged, xp, xs, w_out_b, row(ln2[0]), tm=tm_split)
    act, w_down_b, w_pg_b, w_ple_b = _ffn_up(
        h2, w_gu[0], to_cast=[(w_down[0], 128), (w_pg[0], 64), (w_ple[0], 32)],
        tm=tm_quarter, tf=512)
    x2 = _ffn_down(act, w_down_b, x1, tm=tm_sixteenth, tn=1024)
    yp, ys = _final(x2, p_prompt[0].reshape(npr, -1), p_sample[0].reshape(ns, -1),
                    w_pg_b, w_ple_b, row(ln3[0]), row(ln_f), sample_shape=(nb, dseq), tm=tm_split)

    return (yp.reshape(batch, seq, d), ys,
            s_hp[None], s_gp[None], s_hs, s_gs)
```

```python
import functools

import jax
import jax.numpy as jnp
from jax import lax
from jax.experimental import pallas as pl
from jax.experimental.pallas import tpu as pltpu

F32 = jnp.float32
BF16 = jnp.bfloat16

EPS = 1e-6
LANE = 128
HG_HEADS, HG_K, HG_V = 8, 128, 128
GLA_HEADS, GLA_K, GLA_V = 4, 128, 256
GLA_GATE_RANK = 16
GLA_GATE_NORM = 16.0
CHUNK = 32
MAIN_COLS = 7168
VMEM_LIMIT = 56 * 1024 * 1024
VMEM_LIMIT_LARGE = 60 * 1024 * 1024
COL_BLOCK = 256
TN_WIDE = 1024
TF = 512

NT = (((1,), (1,)), ((), ()))
TN = (((0,), (0,)), ((), ()))


def _params(n_axes, vmem=VMEM_LIMIT):
    return pltpu.CompilerParams(dimension_semantics=("arbitrary",) * n_axes,
                                vmem_limit_bytes=vmem)


def _dot(a, b):
    return jnp.dot(a, b, preferred_element_type=F32)


def _rms(x, g):
    return x * lax.rsqrt(jnp.mean(x * x, axis=-1, keepdims=True) + EPS) * g


def _prep_kernel(xp_ref, xs_ref, ln_ref, wglr_ref, h_ref, glr_ref, xs2d_ref, *, n_prompt_tiles):
    i = pl.program_id(0)

    d = wglr_ref.shape[1]
    wglr = jnp.concatenate([wglr_ref[...], jnp.zeros((LANE - GLA_GATE_RANK, d), F32)], axis=0)
    wglr = wglr.astype(BF16)

    def body(x):
        h = _rms(x, ln_ref[...]).astype(BF16)
        h_ref[...] = h
        glr_ref[...] = lax.dot_general(h, wglr, NT, preferred_element_type=F32)

    @pl.when(i < n_prompt_tiles)
    def _():
        body(xp_ref[...])

    @pl.when(i >= n_prompt_tiles)
    def _():
        x = xs_ref[...].reshape(xs2d_ref.shape)
        xs2d_ref[...] = x
        body(x)


def _prep(xp, xs3, ln1, w_in_t, tm):
    npr, d = xp.shape
    nb, dseq, _ = xs3.shape
    ns = nb * dseq
    assert ns == tm
    npt = npr // tm
    m = npr + ns
    return pl.pallas_call(
        functools.partial(_prep_kernel, n_prompt_tiles=npt),
        grid=(npt + 1,),
        in_specs=[
            pl.BlockSpec((tm, d), lambda i: (jnp.minimum(i, npt - 1), 0)),
            pl.BlockSpec((nb, dseq, d), lambda i: (0, 0, 0)),
            pl.BlockSpec((1, d), lambda i: (0, 0)),
            pl.BlockSpec((GLA_GATE_RANK, d), lambda i: (MAIN_COLS // GLA_GATE_RANK, 0)),
        ],
        out_specs=[
            pl.BlockSpec((tm, d), lambda i: (i, 0)),
            pl.BlockSpec((tm, LANE), lambda i: (i, 0)),
            pl.BlockSpec((tm, d), lambda i: (0, 0)),
        ],
        out_shape=[jax.ShapeDtypeStruct((m, d), BF16),
                   jax.ShapeDtypeStruct((m, LANE), F32),
                   jax.ShapeDtypeStruct((ns, d), F32)],
        compiler_params=_params(1),
        name="prep",
    )(xp, xs3, ln1, w_in_t)


TRANSPOSE_ROWS = 256


def _stage_transposed(w_t, dst_ref, n=None):
    for r in range(0, w_t.shape[0], TRANSPOSE_ROWS):
        blk = w_t[r:r + TRANSPOSE_ROWS, :].T.astype(BF16)
        if n is None:
            dst_ref[:, r:r + TRANSPOSE_ROWS] = blk
        else:
            dst_ref[n, :, r:r + TRANSPOSE_ROWS] = blk


def _in_proj_kernel(h_ref, w_ref, wx_ref, z_ref, wbf_ref, *, n_aligned):
    j = pl.program_id(0)
    stage = pl.program_id(1) == 0
    tn = z_ref.shape[1]

    @pl.when(stage & (j < n_aligned))
    def _():
        _stage_transposed(w_ref[...], wbf_ref)

    @pl.when(stage & (j >= n_aligned))
    def _():
        cat = jnp.concatenate([w_ref[...], wx_ref[...]], axis=0)
        _stage_transposed(cat[GLA_GATE_RANK:GLA_GATE_RANK + tn, :], wbf_ref)

    h = h_ref[...]
    for c0 in range(0, tn, COL_BLOCK):
        cs = slice(c0, c0 + COL_BLOCK)
        z_ref[:, cs] = _dot(h, wbf_ref[:, cs]).astype(z_ref.dtype)


def _in_proj(h1, w_in_t, n_cols, tm, tn):
    m, d = h1.shape
    rk = GLA_GATE_RANK
    return pl.pallas_call(
        functools.partial(_in_proj_kernel, n_aligned=MAIN_COLS // tn),
        grid=(n_cols // tn, m // tm),
        in_specs=[
            pl.BlockSpec((tm, d), lambda j, i: (i, 0)),
            pl.BlockSpec((tn, d), lambda j, i: (j, 0)),
            pl.BlockSpec((rk, d), lambda j, i: ((j + 1) * (tn // rk), 0)),
        ],
        out_specs=pl.BlockSpec((tm, tn), lambda j, i: (i, j)),
        out_shape=jax.ShapeDtypeStruct((m, n_cols), BF16),
        scratch_shapes=[pltpu.VMEM((d, tn), BF16)],
        compiler_params=_params(2, VMEM_LIMIT_LARGE),
        name="in_proj",
    )(h1, w_in_t, w_in_t)


def _cumsum_rows(x, length):
    row = lax.broadcasted_iota(jnp.int32, x.shape, 0) % length
    s = 1
    while s < length:
        x = x + jnp.where(row >= s, pltpu.roll(x, s, axis=0), 0.0)
        s *= 2
    return x


def _forget_lower_bound(lb_rows):
    mx = jnp.max(lb_rows, axis=0, keepdims=True)
    e = jnp.exp(lb_rows - mx)
    return e[0:1, :] / jnp.sum(e, axis=0, keepdims=True)


def _col_broadcast(row):
    return jnp.broadcast_to(row, (LANE, LANE)).T


def _gated_out(o, g, gain):
    return (_rms(o, gain) * jax.nn.silu(g)).astype(BF16)


SLAB = 256
CHUNKS_PER_SLAB = SLAB // CHUNK
SUBLANES = 8
DEC_ROWS = -(-CHUNKS_PER_SLAB // SUBLANES) * SUBLANES


def _slab_rows(g):
    return pl.ds(pl.multiple_of(g * SLAB, SLAB), SLAB)


def _per_chunk_rows(rows_1w):
    w = rows_1w[0].shape[1]
    return jnp.concatenate([jnp.broadcast_to(r, (CHUNK, w)) for r in rows_1w], axis=0)


def _intra_slab(slot, q, k, logf, kdim, vdim, v_slab, qin_scr, kout_scr, dec_scr, o_scr):
    heads = q.shape[1] // kdim
    cum = _cumsum_rows(logf, CHUNK)
    ref_rows = [cum[c * CHUNK + CHUNK // 2:c * CHUNK + CHUNK // 2 + 1, :]
                for c in range(CHUNKS_PER_SLAB)]
    last_rows = [cum[(c + 1) * CHUNK - 1:(c + 1) * CHUNK, :] for c in range(CHUNKS_PER_SLAB)]
    ref = _per_chunk_rows(ref_rows)
    qe = q * jnp.exp(cum - ref)
    ke = k * jnp.exp(ref - cum)
    qin_scr[slot] = (q * jnp.exp(cum)).astype(BF16)
    kout_scr[slot] = (k * jnp.exp(_per_chunk_rows(last_rows) - cum)).astype(BF16)
    for c in range(CHUNKS_PER_SLAB):
        dec_scr[slot, c:c + 1, :] = jnp.exp(last_rows[c])
    qe = qe.astype(BF16)
    ke = ke.astype(BF16)
    r_i = lax.broadcasted_iota(jnp.int32, (SLAB, SLAB), 0)
    c_i = lax.broadcasted_iota(jnp.int32, (SLAB, SLAB), 1)
    causal = (r_i // CHUNK == c_i // CHUNK) & (c_i <= r_i)
    for h in range(heads):
        ks = slice(h * kdim, (h + 1) * kdim)
        vs = slice(h * vdim, (h + 1) * vdim)
        att = lax.dot_general(qe[:, ks], ke[:, ks], NT, preferred_element_type=F32)
        o_scr[slot, :, vs] = _dot(jnp.where(causal, att, 0.0).astype(BF16), v_slab[:, vs])


def _state_chunk(g, slot, c, heads, kdim, vdim, v_ref, g_ref, gain, qin_scr, kout_scr, dec_scr,
                 o_scr, st_scr, o_ref):
    cr = slice(c * CHUNK, (c + 1) * CHUNK)
    rows = pl.ds(pl.multiple_of(g * SLAB + c * CHUNK, CHUNK), CHUNK)
    drow = dec_scr[slot, c:c + 1, :]
    for h in range(heads):
        ks = slice(h * kdim, (h + 1) * kdim)
        vs = slice(h * vdim, (h + 1) * vdim)
        st = st_scr[h]
        o = o_scr[slot, cr, vs] + _dot(qin_scr[slot, cr, ks], st.astype(BF16))
        kv = lax.dot_general(kout_scr[slot, cr, ks], v_ref[rows, vs], TN,
                             preferred_element_type=F32)
        dcol = _col_broadcast(drow[:, ks])
        if vdim != LANE:
            dcol = jnp.concatenate([dcol] * (vdim // LANE), axis=1)
        st_scr[h] = st * dcol + kv
        o_ref[rows, vs] = _gated_out(o, g_ref[rows, vs].astype(F32), gain)


def _run_pipelined(n_slabs, intra, state, st_scr, s_ref):
    assert n_slabs % 2 == 0 and n_slabs >= 4

    def overlapped(g_intra, slot_intra, g_state, slot_state):
        intra(g_intra, slot_intra)
        for c in range(CHUNKS_PER_SLAB):
            state(g_state, slot_state, c)

    st_scr[...] = jnp.zeros_like(st_scr)
    intra(0, 0)

    def pair(i, carry):
        overlapped(2 * i + 1, 1, 2 * i, 0)
        overlapped(2 * i + 2, 0, 2 * i + 1, 1)
        return carry

    lax.fori_loop(0, n_slabs // 2 - 1, pair, 0)
    overlapped(n_slabs - 1, 1, n_slabs - 2, 0)
    for c in range(CHUNKS_PER_SLAB):
        state(n_slabs - 1, 1, c)
    for h in range(st_scr.shape[0]):
        s_ref[0, h] = st_scr[h]


def _hgrn_prompt_kernel(lb_ref, gain_ref, q_ref, f_ref, i_ref, g_ref, o_ref, s_ref,
                        qin_scr, kout_scr, dec_scr, o_scr, st_scr, *, hpb):
    lb = _forget_lower_bound(lb_ref[...])
    gain = gain_ref[...]

    def intra(g, slot):
        rows = _slab_rows(g)
        q = jax.nn.silu(q_ref[rows, :].astype(F32)) * (HG_K ** -0.5)
        sig = jax.nn.sigmoid(f_ref[rows, :].astype(F32))
        logf = jnp.log(lb + (1.0 - lb) * sig)
        k = (1.0 - lb) * (1.0 - sig)
        _intra_slab(slot, q, k, logf, HG_K, HG_V, i_ref[rows, :], qin_scr, kout_scr, dec_scr,
                    o_scr)

    def state(g, slot, c):
        _state_chunk(g, slot, c, hpb, HG_K, HG_V, i_ref, g_ref, gain, qin_scr, kout_scr, dec_scr,
                     o_scr, st_scr, o_ref)

    _run_pipelined(q_ref.shape[0] // SLAB, intra, state, st_scr, s_ref)


def _hgrn_prompt(z, hg_lb, gain, batch, seq, hpb):
    w = hpb * HG_K
    sec = (HG_HEADS * HG_K) // w
    zspec = lambda s: pl.BlockSpec((seq, w), lambda b, g, s=s: (b, s * sec + g))
    return pl.pallas_call(
        functools.partial(_hgrn_prompt_kernel, hpb=hpb),
        grid=(batch, HG_HEADS // hpb),
        in_specs=[
            pl.BlockSpec((hg_lb.shape[0], w), lambda b, g: (0, g)),
            pl.BlockSpec((1, HG_V), lambda b, g: (0, 0)),
            zspec(0), zspec(1), zspec(2), zspec(3),
        ],
        out_specs=[
            pl.BlockSpec((seq, w), lambda b, g: (b, g)),
            pl.BlockSpec((1, hpb, HG_K, HG_V), lambda b, g: (b, g, 0, 0)),
        ],
        out_shape=[jax.ShapeDtypeStruct((batch * seq, HG_HEADS * HG_V), BF16),
                   jax.ShapeDtypeStruct((batch, HG_HEADS, HG_K, HG_V), F32)],
        scratch_shapes=[pltpu.VMEM((2, SLAB, w), BF16), pltpu.VMEM((2, SLAB, w), BF16),
                        pltpu.VMEM((2, DEC_ROWS, w), F32), pltpu.VMEM((2, SLAB, w), F32),
                        pltpu.VMEM((hpb, HG_K, HG_V), F32)],
        compiler_params=_params(2),
        name="hgrn_prompt",
    )(hg_lb, gain, z, z, z, z)


def _gla_prompt_kernel(gain_ref, wgk_ref, bgk_ref, glr_ref, q_ref, k_ref, v_ref, g_ref,
                       o_ref, s_ref, qin_scr, kout_scr, dec_scr, o_scr, st_scr, *, hpb):
    gain = gain_ref[...]
    wgk = wgk_ref[...].astype(BF16)
    bgk = bgk_ref[...]

    def intra(g, slot):
        rows = _slab_rows(g)
        gk = _dot(glr_ref[rows, :].astype(BF16), wgk) + bgk
        logf = jax.nn.log_sigmoid(gk) / GLA_GATE_NORM
        q = q_ref[rows, :].astype(F32) * (GLA_K ** -0.5)
        _intra_slab(slot, q, k_ref[rows, :].astype(F32), logf, GLA_K, GLA_V, v_ref[rows, :],
                    qin_scr, kout_scr, dec_scr, o_scr)

    def state(g, slot, c):
        _state_chunk(g, slot, c, hpb, GLA_K, GLA_V, v_ref, g_ref, gain, qin_scr, kout_scr, dec_scr,
                     o_scr, st_scr, o_ref)

    _run_pipelined(q_ref.shape[0] // SLAB, intra, state, st_scr, s_ref)


def _gla_prompt(z, glr, wgk, bgk, gain, batch, seq, hpb):
    kw, vw = hpb * GLA_K, hpb * GLA_V
    q0 = (4 * HG_HEADS * HG_K) // kw
    k0 = q0 + (GLA_HEADS * GLA_K) // kw
    v0 = (4 * HG_HEADS * HG_K + 2 * GLA_HEADS * GLA_K) // vw
    g0 = v0 + (GLA_HEADS * GLA_V) // vw
    return pl.pallas_call(
        functools.partial(_gla_prompt_kernel, hpb=hpb),
        grid=(batch, GLA_HEADS // hpb),
        in_specs=[
            pl.BlockSpec((1, GLA_V), lambda b, g: (0, 0)),
            pl.BlockSpec((LANE, kw), lambda b, g: (0, g)),
            pl.BlockSpec((1, kw), lambda b, g: (0, g)),
            pl.BlockSpec((seq, LANE), lambda b, g: (b, 0)),
            pl.BlockSpec((seq, kw), lambda b, g: (b, q0 + g)),
            pl.BlockSpec((seq, kw), lambda b, g: (b, k0 + g)),
            pl.BlockSpec((seq, vw), lambda b, g: (b, v0 + g)),
            pl.BlockSpec((seq, vw), lambda b, g: (b, g0 + g)),
        ],
        out_specs=[
            pl.BlockSpec((seq, vw), lambda b, g: (b, g)),
            pl.BlockSpec((1, hpb, GLA_K, GLA_V), lambda b, g: (b, g, 0, 0)),
        ],
        out_shape=[jax.ShapeDtypeStruct((batch * seq, GLA_HEADS * GLA_V), BF16),
                   jax.ShapeDtypeStruct((batch, GLA_HEADS, GLA_K, GLA_V), F32)],
        scratch_shapes=[pltpu.VMEM((2, SLAB, kw), BF16), pltpu.VMEM((2, SLAB, kw), BF16),
                        pltpu.VMEM((2, DEC_ROWS, kw), F32), pltpu.VMEM((2, SLAB, vw), F32),
                        pltpu.VMEM((hpb, GLA_K, GLA_V), F32)],
        compiler_params=_params(2),
        name="gla_prompt",
    )(gain, wgk, bgk, glr, z, z, z, z)


def _seg_pick(x, seg, idx):
    n = x.shape[0]
    rowmod = lax.broadcasted_iota(jnp.int32, x.shape, 0) % seg
    out = x
    for m in range(seg):
        if m != idx:
            out = jnp.where(rowmod == m, pltpu.roll(x, (m - idx) % n, axis=0), out)
    return out


def _sample_branch(q, k, v_b, logf, g, gain, s_ref, ns_ref, o_ref, o_scr, *, heads, kdim, vdim,
                   seq, pairs):
    tile = 2 * seq
    cum = _cumsum_rows(logf, seq)
    ref = _seg_pick(cum, seq, seq // 2)
    last = _seg_pick(cum, seq, seq - 1)
    qe = (q * jnp.exp(cum - ref)).astype(BF16)
    ke = (k * jnp.exp(ref - cum)).astype(BF16)
    q_in = (q * jnp.exp(cum)).astype(BF16)
    k_out = k * jnp.exp(last - cum)
    dec = jnp.exp(last)
    r_i = lax.broadcasted_iota(jnp.int32, (tile, tile), 0)
    c_i = lax.broadcasted_iota(jnp.int32, (tile, tile), 1)
    amask = (r_i // seq == c_i // seq) & (c_i <= r_i)
    half = lax.broadcasted_iota(jnp.int32, (tile, 1), 0) // seq
    for p in range(pairs):
        rs = slice(p * tile, (p + 1) * tile)
        for h in range(heads):
            ks = slice(h * kdim, (h + 1) * kdim)
            vs = slice(h * vdim, (h + 1) * vdim)
            v_t = v_b[rs, vs]
            att = lax.dot_general(qe[rs, ks], ke[rs, ks], NT, preferred_element_type=F32)
            o = _dot(jnp.where(amask, att, 0.0).astype(BF16), v_t)
            for bi in range(2):
                b = 2 * p + bi
                s0 = s_ref[0, b, h]
                o = o + jnp.where(half == bi, _dot(q_in[rs, ks], s0.astype(BF16)), 0.0)
                km = jnp.where(half == bi, k_out[rs, ks], 0.0).astype(BF16)
                kv = lax.dot_general(km, v_t, TN, preferred_element_type=F32)
                r0 = p * tile + bi * seq
                dcol = _col_broadcast(dec[r0:r0 + 1, ks])
                if vdim != LANE:
                    dcol = jnp.concatenate([dcol] * (vdim // LANE), axis=1)
                ns_ref[0, b, h] = s0 * dcol + kv
            o_scr[rs, vs] = o
    for h in range(heads):
        vs = slice(h * vdim, (h + 1) * vdim)
        o_ref[:, vs] = _gated_out(o_scr[:, vs], g[:, vs].astype(F32), gain)


def _sample_kernel(lb_ref, hgain_ref, ggain_ref, wgk_ref, bgk_ref,
                   hq_ref, hf_ref, hi_ref, hg_ref, gq_ref, gk_ref, gv_ref, gg_ref, glr_ref,
                   sh_ref, sg_ref,
                   a_ref, b_ref, nsh_ref, nsg_ref, oh_scr, og_scr, *, seq, pairs):
    lb = _forget_lower_bound(lb_ref[...])
    sig = jax.nn.sigmoid(hf_ref[...].astype(F32))
    _sample_branch(
        jax.nn.silu(hq_ref[...].astype(F32)) * (HG_K ** -0.5),
        (1.0 - lb) * (1.0 - sig),
        hi_ref[...],
        jnp.log(lb + (1.0 - lb) * sig),
        hg_ref[...], hgain_ref[...], sh_ref, nsh_ref, a_ref, oh_scr,
        heads=HG_HEADS, kdim=HG_K, vdim=HG_V, seq=seq, pairs=pairs)
    gk = _dot(glr_ref[...].astype(BF16), wgk_ref[...].astype(BF16)) + bgk_ref[...]
    _sample_branch(
        gq_ref[...].astype(F32) * (GLA_K ** -0.5),
        gk_ref[...].astype(F32),
        gv_ref[...],
        jax.nn.log_sigmoid(gk) / GLA_GATE_NORM,
        gg_ref[...], ggain_ref[...], sg_ref, nsg_ref, b_ref, og_scr,
        heads=GLA_HEADS, kdim=GLA_K, vdim=GLA_V, seq=seq, pairs=pairs)


def _sample_scan(z, glr, state_hgrn, state_gla, hg_lb, hgain, ggain, wgk, bgk, row0, nb, seq, bb):
    rows = bb * seq
    rb0 = row0 // rows
    hw, kw, vw = HG_HEADS * HG_K, GLA_HEADS * GLA_K, GLA_HEADS * GLA_V
    gq0 = (4 * hw) // kw
    gv0 = (4 * hw + 2 * kw) // vw
    zs = lambda w, c: pl.BlockSpec((rows, w), lambda i, c=c: (rb0 + i, c))
    const = lambda shape: pl.BlockSpec(shape, lambda i: (0,) * len(shape))
    st = lambda h, k, v: pl.BlockSpec((1, bb, h, k, v), lambda i: (0, i, 0, 0, 0))
    return pl.pallas_call(
        functools.partial(_sample_kernel, seq=seq, pairs=bb // 2),
        grid=(nb // bb,),
        in_specs=[
            const(hg_lb.shape), const((1, HG_V)), const((1, GLA_V)), const((LANE, kw)),
            const((1, kw)),
            zs(hw, 0), zs(hw, 1), zs(hw, 2), zs(hw, 3),
            zs(kw, gq0), zs(kw, gq0 + 1), zs(vw, gv0), zs(vw, gv0 + 1),
            pl.BlockSpec((rows, LANE), lambda i: (rb0 + i, 0)),
            st(HG_HEADS, HG_K, HG_V), st(GLA_HEADS, GLA_K, GLA_V),
        ],
        out_specs=[
            pl.BlockSpec((rows, hw), lambda i: (i, 0)),
            pl.BlockSpec((rows, vw), lambda i: (i, 0)),
            st(HG_HEADS, HG_K, HG_V), st(GLA_HEADS, GLA_K, GLA_V),
        ],
        out_shape=[jax.ShapeDtypeStruct((nb * seq, hw), BF16),
                   jax.ShapeDtypeStruct((nb * seq, vw), BF16),
                   jax.ShapeDtypeStruct(state_hgrn.shape, state_hgrn.dtype),
                   jax.ShapeDtypeStruct(state_gla.shape, state_gla.dtype)],
        scratch_shapes=[pltpu.VMEM((rows, hw), F32), pltpu.VMEM((rows, vw), F32)],
        compiler_params=_params(1),
        name="sample_scan",
    )(hg_lb, hgain, ggain, wgk, bgk, z, z, z, z, z, z, z, z, glr, state_hgrn, state_gla)


def _cast_specs(to_cast, n_j, n_i):
    specs = []
    for w, slab in to_cast:
        n_slabs = w.shape[0] // slab
        assert w.shape[0] % slab == 0 and n_slabs <= n_j * n_i
        specs.append(pl.BlockSpec(
            (slab, w.shape[1]),
            lambda j, i, n_slabs=n_slabs: (jnp.minimum(j * n_i + i, n_slabs - 1), 0)))
    return specs


def _merge_kernel(g0_ref, g1_ref, ap_ref, as_ref, bp_ref, bs_ref, wb0_ref, wb1_ref, *rest,
                  n_prompt_tiles, n_casts):
    cast_in, (o_ref, *cast_out), (wb_scr,) = (
        rest[:n_casts], rest[n_casts:2 * n_casts + 1], rest[2 * n_casts + 1:])
    tn = o_ref.shape[1]
    is_prompt = pl.program_id(1) < n_prompt_tiles

    @pl.when(pl.program_id(1) == 0)
    def _():
        wb_scr[0] = wb0_ref[0].astype(BF16)
        wb_scr[1] = wb1_ref[0].astype(BF16)

    a = jnp.where(is_prompt, ap_ref[...], as_ref[...])
    b = jnp.where(is_prompt, bp_ref[...], bs_ref[...])
    for c0 in range(0, tn, COL_BLOCK):
        cs = slice(c0, c0 + COL_BLOCK)
        m = jax.nn.sigmoid(g0_ref[:, cs].astype(F32)) * _dot(a, wb_scr[0, :, cs])
        m = m + jax.nn.sigmoid(g1_ref[:, cs].astype(F32)) * _dot(b, wb_scr[1, :, cs])
        o_ref[:, cs] = m.astype(o_ref.dtype)
    for src, dst in zip(cast_in, cast_out):
        dst[...] = src[...].astype(BF16)


def _merge(z, a_p, a_s, b_p, b_s, w_branch, to_cast, tm, tn):
    m = z.shape[0]
    d = w_branch.shape[2]
    bw = a_p.shape[1]
    npt = a_p.shape[0] // tm
    n_i = m // tm
    cast_specs = _cast_specs(to_cast, d // tn, n_i)
    pspec =pl.BlockSpec((tm, bw), lambda j, i: (jnp.minimum(i, npt - 1), 0))
    sspec = pl.BlockSpec((tm, bw), lambda j, i: (jnp.maximum(i - npt, 0), 0))
    gate = lambda n: pl.BlockSpec((tm, tn), lambda j, i, n=n: (i, (MAIN_COLS + n * d) // tn + j))
    return pl.pallas_call(
        functools.partial(_merge_kernel, n_prompt_tiles=npt, n_casts=len(to_cast)),
        grid=(d // tn, n_i),
        in_specs=[
            gate(0), gate(1),
            pspec, sspec, pspec, sspec,
            pl.BlockSpec((1, bw, tn), lambda j, i: (0, 0, j)),
            pl.BlockSpec((1, bw, tn), lambda j, i: (1, 0, j)),
        ] + cast_specs,
        out_specs=[pl.BlockSpec((tm, tn), lambda j, i: (i, j))] + cast_specs,
        out_shape=[jax.ShapeDtypeStruct((m, d), BF16)]
        + [jax.ShapeDtypeStruct(w.shape, BF16) for w, _ in to_cast],
        scratch_shapes=[pltpu.VMEM((2, bw, tn), BF16)],
        compiler_params=_params(2),
        name="merge",
    )(z, z, a_p, a_s, b_p, b_s, w_branch, w_branch, *[w for w, _ in to_cast])


def _out_proj_kernel(m_ref, xp_ref, xs_ref, w_ref, ln_ref, x1_ref, h2_ref, *, n_prompt_tiles):
    is_prompt = pl.program_id(0) < n_prompt_tiles
    m = m_ref[...]
    tm, d = x1_ref.shape
    ss = jnp.zeros((tm, 1), F32)
    for c0 in range(0, d, COL_BLOCK):
        cs = slice(c0, c0 + COL_BLOCK)
        x1 = jnp.where(is_prompt, xp_ref[:, cs], xs_ref[:, cs]) + _dot(m, w_ref[:, cs])
        x1_ref[:, cs] = x1
        ss = ss + jnp.sum(x1 * x1, axis=-1, keepdims=True)
    inv = lax.rsqrt(ss * (1.0 / d) + EPS)
    for c0 in range(0, d, COL_BLOCK):
        cs = slice(c0, c0 + COL_BLOCK)
        h2_ref[:, cs] = (x1_ref[:, cs] * inv * ln_ref[:, cs]).astype(BF16)


def _out_proj(merged, xp, xs, w_out, ln2, tm):
    m, d = merged.shape
    npt = xp.shape[0] // tm
    return pl.pallas_call(
        functools.partial(_out_proj_kernel, n_prompt_tiles=npt),
        grid=(m // tm,),
        in_specs=[
            pl.BlockSpec((tm, d), lambda i: (i, 0)),
            pl.BlockSpec((tm, d), lambda i: (jnp.minimum(i, npt - 1), 0)),
            pl.BlockSpec((tm, d), lambda i: (jnp.maximum(i - npt, 0), 0)),
            pl.BlockSpec((d, d), lambda i: (0, 0), pipeline_mode=pl.Buffered(1)),
            pl.BlockSpec((1, d), lambda i: (0, 0)),
        ],
        out_specs=[pl.BlockSpec((tm, d), lambda i: (i, 0)),
                   pl.BlockSpec((tm, d), lambda i: (i, 0))],
        out_shape=[jax.ShapeDtypeStruct((m, d), F32), jax.ShapeDtypeStruct((m, d), BF16)],
        compiler_params=_params(1),
        name="out_proj",
    )(merged, xp, xs, w_out, ln2)


def _ffn_up_kernel(h_ref, wg_ref, wu_ref, *rest, n_casts):
    cast_in, (o_ref, *cast_out), (w_scr,) = (
        rest[:n_casts], rest[n_casts:2 * n_casts + 1], rest[2 * n_casts + 1:])

    @pl.when(pl.program_id(1) == 0)
    def _():
        w_scr[0] = wg_ref[...].astype(BF16)
        w_scr[1] = wu_ref[...].astype(BF16)

    h = h_ref[...]
    for c0 in range(0, o_ref.shape[1], COL_BLOCK):
        cs = slice(c0, c0 + COL_BLOCK)
        gate = _dot(h, w_scr[0, :, cs])
        o_ref[:, cs] = (jax.nn.silu(gate) * _dot(h, w_scr[1, :, cs])).astype(o_ref.dtype)
    for src, dst in zip(cast_in, cast_out):
        dst[...] = src[...].astype(BF16)


def _ffn_up(h2, w_gu, to_cast, tm, tf):
    m, d = h2.shape
    dff = w_gu.shape[1] // 2
    nj = dff // tf
    cast_specs = _cast_specs(to_cast, nj, m // tm)
    return pl.pallas_call(
        functools.partial(_ffn_up_kernel, n_casts=len(to_cast)),
        grid=(nj, m // tm),
        in_specs=[
            pl.BlockSpec((tm, d), lambda j, i: (i, 0)),
            pl.BlockSpec((d, tf), lambda j, i: (0, j)),
            pl.BlockSpec((d, tf), lambda j, i: (0, nj + j)),
        ] + cast_specs,
        out_specs=[pl.BlockSpec((tm, tf), lambda j, i: (i, j))] + cast_specs,
        out_shape=[jax.ShapeDtypeStruct((m, dff), BF16)]
        + [jax.ShapeDtypeStruct(w.shape, BF16) for w, _ in to_cast],
        scratch_shapes=[pltpu.VMEM((2, d, tf), BF16)],
        compiler_params=_params(2, VMEM_LIMIT_LARGE),
        name="ffn_up",
    )(h2, w_gu, w_gu, *[w for w, _ in to_cast])


def _ffn_down_kernel(a_ref, w_ref, x_ref, o_ref):
    a = a_ref[...]
    for c0 in range(0, o_ref.shape[1], COL_BLOCK):
        cs = slice(c0, c0 + COL_BLOCK)
        o_ref[:, cs] = x_ref[:, cs] + _dot(a, w_ref[:, cs])


def _ffn_down(act, w_down, x1, tm, tn):
    m, dff = act.shape
    d = w_down.shape[1]
    return pl.pallas_call(
        _ffn_down_kernel,
        grid=(d // tn, m // tm),
        in_specs=[
            pl.BlockSpec((tm, dff), lambda j, i: (i, 0)),
            pl.BlockSpec((dff, tn), lambda j, i: (0, j)),
            pl.BlockSpec((tm, tn), lambda j, i: (i, j)),
        ],
        out_specs=pl.BlockSpec((tm, tn), lambda j, i: (i, j)),
        out_shape=jax.ShapeDtypeStruct((m, d), F32),
        compiler_params=_params(2),
        name="ffn_down",
    )(act, w_down, x1)


def _final_kernel(x_ref, pp_ref, ps_ref, wpg_ref, wple_ref, ln3_ref, lnf_ref, yp_ref, ys_ref,
                  x3_scr, *, n_prompt_tiles):
    i = pl.program_id(0)
    is_prompt = i < n_prompt_tiles
    tm, d = x_ref.shape
    x2 = x_ref[...]
    h3 = (x2 * ln3_ref[...]).astype(BF16)
    inv3 = lax.rsqrt(jnp.mean(x2 * x2, axis=-1, keepdims=True) + EPS)
    p = jnp.where(is_prompt, pp_ref[...], ps_ref[...]).astype(BF16)
    ss = jnp.zeros((tm, 1), F32)
    for c0 in range(0, d, COL_BLOCK):
        cs = slice(c0, c0 + COL_BLOCK)
        gate = jax.nn.sigmoid(inv3 * _dot(h3, wpg_ref[:, cs]))
        x3 = x_ref[:, cs] + gate * _dot(p, wple_ref[:, cs])
        x3_scr[:, cs] = x3
        ss = ss + jnp.sum(x3 * x3, axis=-1, keepdims=True)
    inv = lax.rsqrt(ss * (1.0 / d) + EPS)

    def write(y_ref):
        for c0 in range(0, d, COL_BLOCK):
            cs = slice(c0, c0 + COL_BLOCK)
            y = x3_scr[:, cs] * inv * lnf_ref[:, cs]
            if len(y_ref.shape) == 3:
                y_ref[:, :, cs] = y.reshape(y_ref.shape[0], y_ref.shape[1], COL_BLOCK)
            else:
                y_ref[:, cs] = y

    @pl.when(is_prompt)
    def _():
        write(yp_ref)

    @pl.when(jnp.logical_not(is_prompt))
    def _():
        write(ys_ref)


def _final(x2, pp, ps, w_pg, w_ple, ln3, ln_f, sample_shape, tm):
    m, d = x2.shape
    npr, ns = pp.shape[0], ps.shape[0]
    pd = pp.shape[1]
    npt = npr // tm
    nb, dseq = sample_shape
    assert nb * dseq == ns == tm
    pidx = lambda i: (jnp.minimum(i, npt - 1), 0)
    sidx = lambda i: (jnp.maximum(i - npt, 0), 0)
    return pl.pallas_call(
        functools.partial(_final_kernel, n_prompt_tiles=npt),
        grid=(m // tm,),
        in_specs=[
            pl.BlockSpec((tm, d), lambda i: (i, 0)),
            pl.BlockSpec((tm, pd), pidx),
            pl.BlockSpec((tm, pd), sidx),
            pl.BlockSpec((d, d), lambda i: (0, 0), pipeline_mode=pl.Buffered(1)),
            pl.BlockSpec((pd, d), lambda i: (0, 0)),
            pl.BlockSpec((1, d), lambda i: (0, 0)),
            pl.BlockSpec((1, d), lambda i: (0, 0)),
        ],
        out_specs=[pl.BlockSpec((tm, d), pidx),
                   pl.BlockSpec((nb, dseq, d), lambda i: (0, 0, 0))],
        out_shape=[jax.ShapeDtypeStruct((npr, d), F32), jax.ShapeDtypeStruct((nb, dseq, d), F32)],
        scratch_shapes=[pltpu.VMEM((tm, d), F32)],
        compiler_params=_params(1),
        name="final",
    )(x2, pp, ps, w_pg, w_ple, ln3, ln_f)


def kernel(x_prompt, x_sample, state_hgrn, state_gla, p_prompt, p_sample, hg_lb, ln1, w_in, hg_norm,
           gla_w_gk, gla_b_gk, gla_norm, w_branch, w_out, ln2, w_gu, w_down, ln3, w_ple, w_pg, ln_f):
    batch, seq, d = x_prompt.shape
    nb, dseq, _ = x_sample.shape
    depth = w_in.shape[0]
    assert depth == 1, "single-layer step"
    npr, ns = batch * seq, nb * dseq
    xp = x_prompt.reshape(npr, d)
    row = lambda v: v.reshape(1, -1)

    w_in0 = jnp.swapaxes(w_in[0], 0, 1)
    wgk = jnp.pad(gla_w_gk[0], ((0, LANE - GLA_GATE_RANK), (0, 0)))
    bgk = row(gla_b_gk[0])

    m = npr + ns
    tm_split, tm_quarter, tm_sixteenth = ns, m // 4, m // 16
    assert npr % tm_split == 0 and m % 64 == 0

    h1, glr, xs = _prep(xp, x_sample, row(ln1[0]), w_in0, tm=tm_split)
    z = _in_proj(h1, w_in0, n_cols=MAIN_COLS + 2 * d, tm=tm_quarter, tn=TN_WIDE)

    a_p, s_hp = _hgrn_prompt(z, hg_lb, row(hg_norm[0]), batch, seq, hpb=HG_HEADS)
    b_p, s_gp = _gla_prompt(z, glr, wgk, bgk, row(gla_norm[0]), batch, seq, hpb=GLA_HEADS)
    a_s, b_s, s_hs, s_gs = _sample_scan(z, glr, state_hgrn, state_gla, hg_lb, row(hg_norm[0]),
                                        row(gla_norm[0]), wgk, bgk, row0=npr, nb=nb, seq=dseq, bb=8)

    merged, w_out_b = _merge(z, a_p, a_s, b_p, b_s, w_branch[0], to_cast=[(w_out[0], 64)],
                             tm=tm_split, tn=TN_WIDE)
    x1, h2 = _out_proj(merged, xp, xs, w_out_b, row(ln2[0]), tm=tm_split)
    act, w_down_b, w_pg_b, w_ple_b = _ffn_up(
        h2, w_gu[0], to_cast=[(w_down[0], 128), (w_pg[0], 64), (w_ple[0], 32)],
        tm=tm_quarter, tf=TF)
    x2 = _ffn_down(act, w_down_b, x1, tm=tm_sixteenth, tn=TN_WIDE)
    yp, ys = _final(x2, p_prompt[0].reshape(npr, -1), p_sample[0].reshape(ns, -1),
                    w_pg_b, w_ple_b, row(ln3[0]), row(ln_f), sample_shape=(nb, dseq), tm=tm_split)

    return (yp.reshape(batch, seq, d), ys,
            s_hp[None], s_gp[None], s_hs, s_gs)
```

```python
import functools

import jax
import jax.numpy as jnp
from jax import lax
from jax.experimental import pallas as pl
from jax.experimental.pallas import tpu as pltpu

F32 = jnp.float32
BF16 = jnp.bfloat16

EPS = 1e-6
LANE = 128
HG_HEADS, HG_K, HG_V = 8, 128, 128
GLA_HEADS, GLA_K, GLA_V = 4, 128, 256
GLA_GATE_RANK = 16
GLA_GATE_NORM = 16.0
CHUNK = 32
MAIN_COLS = 7168
VMEM_LIMIT = 56 * 1024 * 1024
VMEM_LIMIT_LARGE = 60 * 1024 * 1024
COL_BLOCK = 256
TN_WIDE = 1024
TF = 512

NT = (((1,), (1,)), ((), ()))
TN = (((0,), (0,)), ((), ()))


def _params(n_axes, vmem=VMEM_LIMIT):
    return pltpu.CompilerParams(dimension_semantics=("arbitrary",) * n_axes,
                                vmem_limit_bytes=vmem)


def _dot(a, b):
    return jnp.dot(a, b, preferred_element_type=F32)


def _rms(x, g):
    return x * lax.rsqrt(jnp.mean(x * x, axis=-1, keepdims=True) + EPS) * g


def _prep_kernel(xp_ref, xs_ref, ln_ref, wglr_ref, h_ref, glr_ref, xs2d_ref, *, n_prompt_tiles):
    i = pl.program_id(0)

    d = wglr_ref.shape[1]
    wglr = jnp.concatenate([wglr_ref[...], jnp.zeros((LANE - GLA_GATE_RANK, d), F32)], axis=0)
    wglr = wglr.astype(BF16)

    def body(x):
        h = _rms(x, ln_ref[...]).astype(BF16)
        h_ref[...] = h
        glr_ref[...] = lax.dot_general(h, wglr, NT, preferred_element_type=F32)

    @pl.when(i < n_prompt_tiles)
    def _():
        body(xp_ref[...])

    @pl.when(i >= n_prompt_tiles)
    def _():
        x = xs_ref[...].reshape(xs2d_ref.shape)
        xs2d_ref[...] = x
        body(x)


def _prep(xp, xs3, ln1, w_in_t, tm):
    npr, d = xp.shape
    nb, dseq, _ = xs3.shape
    ns = nb * dseq
    assert ns == tm
    npt = npr // tm
    m = npr + ns
    return pl.pallas_call(
        functools.partial(_prep_kernel, n_prompt_tiles=npt),
        grid=(npt + 1,),
        in_specs=[
            pl.BlockSpec((tm, d), lambda i: (jnp.minimum(i, npt - 1), 0)),
            pl.BlockSpec((nb, dseq, d), lambda i: (0, 0, 0)),
            pl.BlockSpec((1, d), lambda i: (0, 0)),
            pl.BlockSpec((GLA_GATE_RANK, d), lambda i: (MAIN_COLS // GLA_GATE_RANK, 0)),
        ],
        out_specs=[
            pl.BlockSpec((tm, d), lambda i: (i, 0)),
            pl.BlockSpec((tm, LANE), lambda i: (i, 0)),
            pl.BlockSpec((tm, d), lambda i: (0, 0)),
        ],
        out_shape=[jax.ShapeDtypeStruct((m, d), BF16),
                   jax.ShapeDtypeStruct((m, LANE), F32),
                   jax.ShapeDtypeStruct((ns, d), F32)],
        compiler_params=_params(1),
        name="prep",
    )(xp, xs3, ln1, w_in_t)


TRANSPOSE_ROWS = 256


def _stage_transposed(w_t, dst_ref, n=None):
    for r in range(0, w_t.shape[0], TRANSPOSE_ROWS):
        blk = w_t[r:r + TRANSPOSE_ROWS, :].T.astype(BF16)
        if n is None:
            dst_ref[:, r:r + TRANSPOSE_ROWS] = blk
        else:
            dst_ref[n, :, r:r + TRANSPOSE_ROWS] = blk


def _in_proj_kernel(h_ref, w_ref, wx_ref, z_ref, wbf_ref, *, n_aligned):
    j = pl.program_id(0)
    stage = pl.program_id(1) == 0
    tn = z_ref.shape[1]

    @pl.when(stage & (j < n_aligned))
    def _():
        _stage_transposed(w_ref[...], wbf_ref)

    @pl.when(stage & (j >= n_aligned))
    def _():
        cat = jnp.concatenate([w_ref[...], wx_ref[...]], axis=0)
        _stage_transposed(cat[GLA_GATE_RANK:GLA_GATE_RANK + tn, :], wbf_ref)

    h = h_ref[...]
    for c0 in range(0, tn, COL_BLOCK):
        cs = slice(c0, c0 + COL_BLOCK)
        z_ref[:, cs] = _dot(h, wbf_ref[:, cs]).astype(z_ref.dtype)


def _in_proj(h1, w_in_t, n_cols, tm, tn):
    m, d = h1.shape
    rk = GLA_GATE_RANK
    return pl.pallas_call(
        functools.partial(_in_proj_kernel, n_aligned=MAIN_COLS // tn),
        grid=(n_cols // tn, m // tm),
        in_specs=[
            pl.BlockSpec((tm, d), lambda j, i: (i, 0)),
            pl.BlockSpec((tn, d), lambda j, i: (j, 0)),
            pl.BlockSpec((rk, d), lambda j, i: ((j + 1) * (tn // rk), 0)),
        ],
        out_specs=pl.BlockSpec((tm, tn), lambda j, i: (i, j)),
        out_shape=jax.ShapeDtypeStruct((m, n_cols), BF16),
        scratch_shapes=[pltpu.VMEM((d, tn), BF16)],
        compiler_params=_params(2, VMEM_LIMIT_LARGE),
        name="in_proj",
    )(h1, w_in_t, w_in_t)


def _cumsum_rows(x, length):
    row = lax.broadcasted_iota(jnp.int32, x.shape, 0) % length
    s = 1
    while s < length:
        x = x + jnp.where(row >= s, pltpu.roll(x, s, axis=0), 0.0)
        s *= 2
    return x


def _forget_lower_bound(lb_rows):
    mx = jnp.max(lb_rows, axis=0, keepdims=True)
    e = jnp.exp(lb_rows - mx)
    return e[0:1, :] / jnp.sum(e, axis=0, keepdims=True)


def _col_broadcast(row):
    return jnp.broadcast_to(row, (LANE, LANE)).T


def _gated_out(o, g, gain):
    return (_rms(o, gain) * jax.nn.silu(g)).astype(BF16)


SLAB = 256
CHUNKS_PER_SLAB = SLAB // CHUNK
SUBLANES = 8
DEC_ROWS = -(-CHUNKS_PER_SLAB // SUBLANES) * SUBLANES


def _slab_rows(g):
    return pl.ds(pl.multiple_of(g * SLAB, SLAB), SLAB)


def _per_chunk_rows(rows_1w):
    w = rows_1w[0].shape[1]
    return jnp.concatenate([jnp.broadcast_to(r, (CHUNK, w)) for r in rows_1w], axis=0)


def _intra_slab(slot, q, k, logf, kdim, vdim, v_slab, qin_scr, kout_scr, dec_scr, o_scr):
    heads = q.shape[1] // kdim
    cum = _cumsum_rows(logf, CHUNK)
    ref_rows = [cum[c * CHUNK + CHUNK // 2:c * CHUNK + CHUNK // 2 + 1, :]
                for c in range(CHUNKS_PER_SLAB)]
    last_rows = [cum[(c + 1) * CHUNK - 1:(c + 1) * CHUNK, :] for c in range(CHUNKS_PER_SLAB)]
    ref = _per_chunk_rows(ref_rows)
    qe = q * jnp.exp(cum - ref)
    ke = k * jnp.exp(ref - cum)
    qin_scr[slot] = (q * jnp.exp(cum)).astype(BF16)
    kout_scr[slot] = (k * jnp.exp(_per_chunk_rows(last_rows) - cum)).astype(BF16)
    for c in range(CHUNKS_PER_SLAB):
        dec_scr[slot, c:c + 1, :] = jnp.exp(last_rows[c])
    qe = qe.astype(BF16)
    ke = ke.astype(BF16)
    r_i = lax.broadcasted_iota(jnp.int32, (SLAB, SLAB), 0)
    c_i = lax.broadcasted_iota(jnp.int32, (SLAB, SLAB), 1)
    causal = (r_i // CHUNK == c_i // CHUNK) & (c_i <= r_i)
    for h in range(heads):
        ks = slice(h * kdim, (h + 1) * kdim)
        vs = slice(h * vdim, (h + 1) * vdim)
        att = lax.dot_general(qe[:, ks], ke[:, ks], NT, preferred_element_type=F32)
        o_scr[slot, :, vs] = _dot(jnp.where(causal, att, 0.0).astype(BF16), v_slab[:, vs])


def _state_chunk(g, slot, c, heads, kdim, vdim, v_ref, g_ref, gain, qin_scr, kout_scr, dec_scr,
                 o_scr, st_scr, o_ref):
    cr = slice(c * CHUNK, (c + 1) * CHUNK)
    rows = pl.ds(pl.multiple_of(g * SLAB + c * CHUNK, CHUNK), CHUNK)
    drow = dec_scr[slot, c:c + 1, :]
    for h in range(heads):
        ks = slice(h * kdim, (h + 1) * kdim)
        vs = slice(h * vdim, (h + 1) * vdim)
        st = st_scr[h]
        o = o_scr[slot, cr, vs] + _dot(qin_scr[slot, cr, ks], st.astype(BF16))
        kv = lax.dot_general(kout_scr[slot, cr, ks], v_ref[rows, vs], TN,
                             preferred_element_type=F32)
        dcol = _col_broadcast(drow[:, ks])
        if vdim != LANE:
            dcol = jnp.concatenate([dcol] * (vdim // LANE), axis=1)
        st_scr[h] = st * dcol + kv
        o_ref[rows, vs] = _gated_out(o, g_ref[rows, vs].astype(F32), gain)


def _run_pipelined(n_slabs, intra, state, st_scr, s_ref):
    assert n_slabs % 2 == 0 and n_slabs >= 4

    def overlapped(g_intra, slot_intra, g_state, slot_state):
        intra(g_intra, slot_intra)
        for c in range(CHUNKS_PER_SLAB):
            state(g_state, slot_state, c)

    st_scr[...] = jnp.zeros_like(st_scr)
    intra(0, 0)

    def pair(i, carry):
        overlapped(2 * i + 1, 1, 2 * i, 0)
        overlapped(2 * i + 2, 0, 2 * i + 1, 1)
        return carry

    lax.fori_loop(0, n_slabs // 2 - 1, pair, 0)
    overlapped(n_slabs - 1, 1, n_slabs - 2, 0)
    for c in range(CHUNKS_PER_SLAB):
        state(n_slabs - 1, 1, c)
    for h in range(st_scr.shape[0]):
        s_ref[0, h] = st_scr[h]


def _hgrn_prompt_kernel(lb_ref, gain_ref, q_ref, f_ref, i_ref, g_ref, o_ref, s_ref,
                        qin_scr, kout_scr, dec_scr, o_scr, st_scr, *, hpb):
    lb = _forget_lower_bound(lb_ref[...])
    gain = gain_ref[...]

    def intra(g, slot):
        rows = _slab_rows(g)
        q = jax.nn.silu(q_ref[rows, :].astype(F32)) * (HG_K ** -0.5)
        sig = jax.nn.sigmoid(f_ref[rows, :].astype(F32))
        logf = jnp.log(lb + (1.0 - lb) * sig)
        k = (1.0 - lb) * (1.0 - sig)
        _intra_slab(slot, q, k, logf, HG_K, HG_V, i_ref[rows, :], qin_scr, kout_scr, dec_scr,
                    o_scr)

    def state(g, slot, c):
        _state_chunk(g, slot, c, hpb, HG_K, HG_V, i_ref, g_ref, gain, qin_scr, kout_scr, dec_scr,
                     o_scr, st_scr, o_ref)

    _run_pipelined(q_ref.shape[0] // SLAB, intra, state, st_scr, s_ref)


def _hgrn_prompt(z, hg_lb, gain, batch, seq, hpb):
    w = hpb * HG_K
    sec = (HG_HEADS * HG_K) // w
    zspec = lambda s: pl.BlockSpec((seq, w), lambda b, g, s=s: (b, s * sec + g))
    return pl.pallas_call(
        functools.partial(_hgrn_prompt_kernel, hpb=hpb),
        grid=(batch, HG_HEADS // hpb),
        in_specs=[
            pl.BlockSpec((hg_lb.shape[0], w), lambda b, g: (0, g)),
            pl.BlockSpec((1, HG_V), lambda b, g: (0, 0)),
            zspec(0), zspec(1), zspec(2), zspec(3),
        ],
        out_specs=[
            pl.BlockSpec((seq, w), lambda b, g: (b, g)),
            pl.BlockSpec((1, hpb, HG_K, HG_V), lambda b, g: (b, g, 0, 0)),
        ],
        out_shape=[jax.ShapeDtypeStruct((batch * seq, HG_HEADS * HG_V), BF16),
                   jax.ShapeDtypeStruct((batch, HG_HEADS, HG_K, HG_V), F32)],
        scratch_shapes=[pltpu.VMEM((2, SLAB, w), BF16), pltpu.VMEM((2, SLAB, w), BF16),
                        pltpu.VMEM((2, DEC_ROWS, w), F32), pltpu.VMEM((2, SLAB, w), F32),
                        pltpu.VMEM((hpb, HG_K, HG_V), F32)],
        compiler_params=_params(2),
        name="hgrn_prompt",
    )(hg_lb, gain, z, z, z, z)


def _gla_prompt_kernel(gain_ref, wgk_ref, bgk_ref, glr_ref, q_ref, k_ref, v_ref, g_ref,
                       o_ref, s_ref, qin_scr, kout_scr, dec_scr, o_scr, st_scr, *, hpb):
    gain = gain_ref[...]
    wgk = wgk_ref[...].astype(BF16)
    bgk = bgk_ref[...]

    def intra(g, slot):
        rows = _slab_rows(g)
        gk = _dot(glr_ref[rows, :].astype(BF16), wgk) + bgk
        logf = jax.nn.log_sigmoid(gk) / GLA_GATE_NORM
        q = q_ref[rows, :].astype(F32) * (GLA_K ** -0.5)
        _intra_slab(slot, q, k_ref[rows, :].astype(F32), logf, GLA_K, GLA_V, v_ref[rows, :],
                    qin_scr, kout_scr, dec_scr, o_scr)

    def state(g, slot, c):
        _state_chunk(g, slot, c, hpb, GLA_K, GLA_V, v_ref, g_ref, gain, qin_scr, kout_scr, dec_scr,
                     o_scr, st_scr, o_ref)

    _run_pipelined(q_ref.shape[0] // SLAB, intra, state, st_scr, s_ref)


def _gla_prompt(z, glr, wgk, bgk, gain, batch, seq, hpb):
    kw, vw = hpb * GLA_K, hpb * GLA_V
    q0 = (4 * HG_HEADS * HG_K) // kw
    k0 = q0 + (GLA_HEADS * GLA_K) // kw
    v0 = (4 * HG_HEADS * HG_K + 2 * GLA_HEADS * GLA_K) // vw
    g0 = v0 + (GLA_HEADS * GLA_V) // vw
    return pl.pallas_call(
        functools.partial(_gla_prompt_kernel, hpb=hpb),
        grid=(batch, GLA_HEADS // hpb),
        in_specs=[
            pl.BlockSpec((1, GLA_V), lambda b, g: (0, 0)),
            pl.BlockSpec((LANE, kw), lambda b, g: (0, g)),
            pl.BlockSpec((1, kw), lambda b, g: (0, g)),
            pl.BlockSpec((seq, LANE), lambda b, g: (b, 0)),
            pl.BlockSpec((seq, kw), lambda b, g: (b, q0 + g)),
            pl.BlockSpec((seq, kw), lambda b, g: (b, k0 + g)),
            pl.BlockSpec((seq, vw), lambda b, g: (b, v0 + g)),
            pl.BlockSpec((seq, vw), lambda b, g: (b, g0 + g)),
        ],
        out_specs=[
            pl.BlockSpec((seq, vw), lambda b, g: (b, g)),
            pl.BlockSpec((1, hpb, GLA_K, GLA_V), lambda b, g: (b, g, 0, 0)),
        ],
        out_shape=[jax.ShapeDtypeStruct((batch * seq, GLA_HEADS * GLA_V), BF16),
                   jax.ShapeDtypeStruct((batch, GLA_HEADS, GLA_K, GLA_V), F32)],
        scratch_shapes=[pltpu.VMEM((2, SLAB, kw), BF16), pltpu.VMEM((2, SLAB, kw), BF16),
                        pltpu.VMEM((2, DEC_ROWS, kw), F32), pltpu.VMEM((2, SLAB, vw), F32),
                        pltpu.VMEM((hpb, GLA_K, GLA_V), F32)],
        compiler_params=_params(2),
        name="gla_prompt",
    )(gain, wgk, bgk, glr, z, z, z, z)


STATE_BUFS = 3

def _seg_pick(x, seg, idx):
    n = x.shape[0]
    rowmod = lax.broadcasted_iota(jnp.int32, x.shape, 0) % seg
    out = x
    for m in range(seg):
        if m != idx:
            out = jnp.where(rowmod == m, pltpu.roll(x, (m - idx) % n, axis=0), out)
    return out


def _sample_branch(q, k, v_b, logf, g, gain, s_ref, ns_ref, o_ref, o_scr, *, heads, kdim, vdim,
                   seq, pairs):
    tile = 2 * seq
    cum = _cumsum_rows(logf, seq)
    ref = _seg_pick(cum, seq, seq // 2)
    last = _seg_pick(cum, seq, seq - 1)
    qe = (q * jnp.exp(cum - ref)).astype(BF16)
    ke = (k * jnp.exp(ref - cum)).astype(BF16)
    q_in = (q * jnp.exp(cum)).astype(BF16)
    k_out = k * jnp.exp(last - cum)
    dec = jnp.exp(last)
    r_i = lax.broadcasted_iota(jnp.int32, (tile, tile), 0)
    c_i = lax.broadcasted_iota(jnp.int32, (tile, tile), 1)
    amask = (r_i // seq == c_i // seq) & (c_i <= r_i)
    half = lax.broadcasted_iota(jnp.int32, (tile, 1), 0) // seq
    for p in range(pairs):
        rs = slice(p * tile, (p + 1) * tile)
        for h in range(heads):
            ks = slice(h * kdim, (h + 1) * kdim)
            vs = slice(h * vdim, (h + 1) * vdim)
            v_t = v_b[rs, vs]
            att = lax.dot_general(qe[rs, ks], ke[rs, ks], NT, preferred_element_type=F32)
            o = _dot(jnp.where(amask, att, 0.0).astype(BF16), v_t)
            for bi in range(2):
                b = 2 * p + bi
                s0 = s_ref[b, h]
                o = o + jnp.where(half == bi, _dot(q_in[rs, ks], s0.astype(BF16)), 0.0)
                km = jnp.where(half == bi, k_out[rs, ks], 0.0).astype(BF16)
                kv = lax.dot_general(km, v_t, TN, preferred_element_type=F32)
                r0 = p * tile + bi * seq
                dcol = _col_broadcast(dec[r0:r0 + 1, ks])
                if vdim != LANE:
                    dcol = jnp.concatenate([dcol] * (vdim // LANE), axis=1)
                ns_ref[0, b, h] = s0 * dcol + kv
            o_scr[rs, vs] = o
    for h in range(heads):
        vs = slice(h * vdim, (h + 1) * vdim)
        o_ref[:, vs] = _gated_out(o_scr[:, vs], g[:, vs].astype(F32), gain)


def _sample_kernel(lb_ref, hgain_ref, ggain_ref, wgk_ref, bgk_ref,
                   hq_ref, hf_ref, hi_ref, hg_ref, gq_ref, gk_ref, gv_ref, gg_ref, glr_ref,
                   sh_hbm, sg_hbm,
                   a_ref, b_ref, nsh_ref, nsg_ref, oh_scr, og_scr, sh_buf, sg_buf, sems, *,
                   seq, pairs, n_steps):
    i = pl.program_id(0)
    bb = 2 * pairs

    def copies(step, slot):
        blk = pl.ds(pl.multiple_of(step * bb, bb), bb)
        return (pltpu.make_async_copy(sh_hbm.at[0, blk], sh_buf.at[slot], sems.at[0, slot]),
                pltpu.make_async_copy(sg_hbm.at[0, blk], sg_buf.at[slot], sems.at[1, slot]))

    @pl.when(i == 0)
    def _():
        for s in range(min(STATE_BUFS, n_steps)):
            for c in copies(s, s):
                c.start()

    ahead = i + (STATE_BUFS - 1)

    @pl.when((i > 0) & (ahead < n_steps))
    def _():
        for c in copies(ahead, lax.rem(ahead, STATE_BUFS)):
            c.start()

    slot = lax.rem(i, STATE_BUFS)
    for c in copies(i, slot):
        c.wait()
    sh_ref = sh_buf.at[slot]
    sg_ref = sg_buf.at[slot]
    lb = _forget_lower_bound(lb_ref[...])
    sig = jax.nn.sigmoid(hf_ref[...].astype(F32))
    _sample_branch(
        jax.nn.silu(hq_ref[...].astype(F32)) * (HG_K ** -0.5),
        (1.0 - lb) * (1.0 - sig),
        hi_ref[...],
        jnp.log(lb + (1.0 - lb) * sig),
        hg_ref[...], hgain_ref[...], sh_ref, nsh_ref, a_ref, oh_scr,
        heads=HG_HEADS, kdim=HG_K, vdim=HG_V, seq=seq, pairs=pairs)
    gk = _dot(glr_ref[...].astype(BF16), wgk_ref[...].astype(BF16)) + bgk_ref[...]
    _sample_branch(
        gq_ref[...].astype(F32) * (GLA_K ** -0.5),
        gk_ref[...].astype(F32),
        gv_ref[...],
        jax.nn.log_sigmoid(gk) / GLA_GATE_NORM,
        gg_ref[...], ggain_ref[...], sg_ref, nsg_ref, b_ref, og_scr,
        heads=GLA_HEADS, kdim=GLA_K, vdim=GLA_V, seq=seq, pairs=pairs)


def _sample_scan(z, glr, state_hgrn, state_gla, hg_lb, hgain, ggain, wgk, bgk, row0, nb, seq, bb):
    rows = bb * seq
    rb0 = row0 // rows
    hw, kw, vw = HG_HEADS * HG_K, GLA_HEADS * GLA_K, GLA_HEADS * GLA_V
    gq0 = (4 * hw) // kw
    gv0 = (4 * hw + 2 * kw) // vw
    zs = lambda w, c: pl.BlockSpec((rows, w), lambda i, c=c: (rb0 + i, c))
    const = lambda shape: pl.BlockSpec(shape, lambda i: (0,) * len(shape))
    st = lambda h, k, v: pl.BlockSpec((1, bb, h, k, v), lambda i: (0, i, 0, 0, 0))
    return pl.pallas_call(
        functools.partial(_sample_kernel, seq=seq, pairs=bb // 2, n_steps=nb // bb),
        grid=(nb // bb,),
        in_specs=[
            const(hg_lb.shape), const((1, HG_V)), const((1, GLA_V)), const((LANE, kw)),
            const((1, kw)),
            zs(hw, 0), zs(hw, 1), zs(hw, 2), zs(hw, 3),
            zs(kw, gq0), zs(kw, gq0 + 1), zs(vw, gv0), zs(vw, gv0 + 1),
            pl.BlockSpec((rows, LANE), lambda i: (rb0 + i, 0)),
            pl.BlockSpec(memory_space=pl.ANY), pl.BlockSpec(memory_space=pl.ANY),
        ],
        out_specs=[
            pl.BlockSpec((rows, hw), lambda i: (i, 0)),
            pl.BlockSpec((rows, vw), lambda i: (i, 0)),
            st(HG_HEADS, HG_K, HG_V), st(GLA_HEADS, GLA_K, GLA_V),
        ],
        out_shape=[jax.ShapeDtypeStruct((nb * seq, hw), BF16),
                   jax.ShapeDtypeStruct((nb * seq, vw), BF16),
                   jax.ShapeDtypeStruct(state_hgrn.shape, state_hgrn.dtype),
                   jax.ShapeDtypeStruct(state_gla.shape, state_gla.dtype)],
        scratch_shapes=[pltpu.VMEM((rows, hw), F32), pltpu.VMEM((rows, vw), F32),
                        pltpu.VMEM((STATE_BUFS, bb, HG_HEADS, HG_K, HG_V), F32),
                        pltpu.VMEM((STATE_BUFS, bb, GLA_HEADS, GLA_K, GLA_V), F32),
                        pltpu.SemaphoreType.DMA((2, STATE_BUFS))],
        compiler_params=_params(1),
        name="sample_scan",
    )(hg_lb, hgain, ggain, wgk, bgk, z, z, z, z, z, z, z, z, glr, state_hgrn, state_gla)


def _cast_specs(to_cast, n_j, n_i):
    specs = []
    for w, slab in to_cast:
        n_slabs = w.shape[0] // slab
        assert w.shape[0] % slab == 0 and n_slabs <= n_j * n_i
        specs.append(pl.BlockSpec(
            (slab, w.shape[1]),
            lambda j, i, n_slabs=n_slabs: (jnp.minimum(j * n_i + i, n_slabs - 1), 0)))
    return specs


def _merge_kernel(g0_ref, g1_ref, ap_ref, as_ref, bp_ref, bs_ref, wb0_ref, wb1_ref, *rest,
                  n_prompt_tiles, n_casts):
    cast_in, (o_ref, *cast_out), (wb_scr,) = (
        rest[:n_casts], rest[n_casts:2 * n_casts + 1], rest[2 * n_casts + 1:])
    tn = o_ref.shape[1]
    is_prompt = pl.program_id(1) < n_prompt_tiles

    @pl.when(pl.program_id(1) == 0)
    def _():
        wb_scr[0] = wb0_ref[0].astype(BF16)
        wb_scr[1] = wb1_ref[0].astype(BF16)

    a = jnp.where(is_prompt, ap_ref[...], as_ref[...])
    b = jnp.where(is_prompt, bp_ref[...], bs_ref[...])
    for c0 in range(0, tn, COL_BLOCK):
        cs = slice(c0, c0 + COL_BLOCK)
        m = jax.nn.sigmoid(g0_ref[:, cs].astype(F32)) * _dot(a, wb_scr[0, :, cs])
        m = m + jax.nn.sigmoid(g1_ref[:, cs].astype(F32)) * _dot(b, wb_scr[1, :, cs])
        o_ref[:, cs] = m.astype(o_ref.dtype)
    for src, dst in zip(cast_in, cast_out):
        dst[...] = src[...].astype(BF16)


def _merge(z, a_p, a_s, b_p, b_s, w_branch, to_cast, tm, tn):
    m = z.shape[0]
    d = w_branch.shape[2]
    bw = a_p.shape[1]
    npt = a_p.shape[0] // tm
    n_i = m // tm
    cast_specs = _cast_specs(to_cast, d // tn, n_i)
    pspec =pl.BlockSpec((tm, bw), lambda j, i: (jnp.minimum(i, npt - 1), 0))
    sspec = pl.BlockSpec((tm, bw), lambda j, i: (jnp.maximum(i - npt, 0), 0))
    gate = lambda n: pl.BlockSpec((tm, tn), lambda j, i, n=n: (i, (MAIN_COLS + n * d) // tn + j))
    return pl.pallas_call(
        functools.partial(_merge_kernel, n_prompt_tiles=npt, n_casts=len(to_cast)),
        grid=(d // tn, n_i),
        in_specs=[
            gate(0), gate(1),
            pspec, sspec, pspec, sspec,
            pl.BlockSpec((1, bw, tn), lambda j, i: (0, 0, j)),
            pl.BlockSpec((1, bw, tn), lambda j, i: (1, 0, j)),
        ] + cast_specs,
        out_specs=[pl.BlockSpec((tm, tn), lambda j, i: (i, j))] + cast_specs,
        out_shape=[jax.ShapeDtypeStruct((m, d), BF16)]
        + [jax.ShapeDtypeStruct(w.shape, BF16) for w, _ in to_cast],
        scratch_shapes=[pltpu.VMEM((2, bw, tn), BF16)],
        compiler_params=_params(2),
        name="merge",
    )(z, z, a_p, a_s, b_p, b_s, w_branch, w_branch, *[w for w, _ in to_cast])


def _out_proj_kernel(m_ref, xp_ref, xs_ref, w_ref, ln_ref, x1_ref, h2_ref, *, n_prompt_tiles):
    is_prompt = pl.program_id(0) < n_prompt_tiles
    m = m_ref[...]
    tm, d = x1_ref.shape
    ss = jnp.zeros((tm, 1), F32)
    for c0 in range(0, d, COL_BLOCK):
        cs = slice(c0, c0 + COL_BLOCK)
        x1 = jnp.where(is_prompt, xp_ref[:, cs], xs_ref[:, cs]) + _dot(m, w_ref[:, cs])
        x1_ref[:, cs] = x1
        ss = ss + jnp.sum(x1 * x1, axis=-1, keepdims=True)
    inv = lax.rsqrt(ss * (1.0 / d) + EPS)
    for c0 in range(0, d, COL_BLOCK):
        cs = slice(c0, c0 + COL_BLOCK)
        h2_ref[:, cs] = (x1_ref[:, cs] * inv * ln_ref[:, cs]).astype(BF16)


def _out_proj(merged, xp, xs, w_out, ln2, tm):
    m, d = merged.shape
    npt = xp.shape[0] // tm
    return pl.pallas_call(
        functools.partial(_out_proj_kernel, n_prompt_tiles=npt),
        grid=(m // tm,),
        in_specs=[
            pl.BlockSpec((tm, d), lambda i: (i, 0)),
            pl.BlockSpec((tm, d), lambda i: (jnp.minimum(i, npt - 1), 0)),
            pl.BlockSpec((tm, d), lambda i: (jnp.maximum(i - npt, 0), 0)),
            pl.BlockSpec((d, d), lambda i: (0, 0), pipeline_mode=pl.Buffered(1)),
            pl.BlockSpec((1, d), lambda i: (0, 0)),
        ],
        out_specs=[pl.BlockSpec((tm, d), lambda i: (i, 0)),
                   pl.BlockSpec((tm, d), lambda i: (i, 0))],
        out_shape=[jax.ShapeDtypeStruct((m, d), F32), jax.ShapeDtypeStruct((m, d), BF16)],
        compiler_params=_params(1),
        name="out_proj",
    )(merged, xp, xs, w_out, ln2)


def _ffn_up_kernel(h_ref, wg_ref, wu_ref, *rest, n_casts):
    cast_in, (o_ref, *cast_out), (w_scr,) = (
        rest[:n_casts], rest[n_casts:2 * n_casts + 1], rest[2 * n_casts + 1:])

    @pl.when(pl.program_id(1) == 0)
    def _():
        w_scr[0] = wg_ref[...].astype(BF16)
        w_scr[1] = wu_ref[...].astype(BF16)

    h = h_ref[...]
    for c0 in range(0, o_ref.shape[1], COL_BLOCK):
        cs = slice(c0, c0 + COL_BLOCK)
        gate = _dot(h, w_scr[0, :, cs])
        o_ref[:, cs] = (jax.nn.silu(gate) * _dot(h, w_scr[1, :, cs])).astype(o_ref.dtype)
    for src, dst in zip(cast_in, cast_out):
        dst[...] = src[...].astype(BF16)


def _ffn_up(h2, w_gu, to_cast, tm, tf):
    m, d = h2.shape
    dff = w_gu.shape[1] // 2
    nj = dff // tf
    cast_specs = _cast_specs(to_cast, nj, m // tm)
    return pl.pallas_call(
        functools.partial(_ffn_up_kernel, n_casts=len(to_cast)),
        grid=(nj, m // tm),
        in_specs=[
            pl.BlockSpec((tm, d), lambda j, i: (i, 0)),
            pl.BlockSpec((d, tf), lambda j, i: (0, j)),
            pl.BlockSpec((d, tf), lambda j, i: (0, nj + j)),
        ] + cast_specs,
        out_specs=[pl.BlockSpec((tm, tf), lambda j, i: (i, j))] + cast_specs,
        out_shape=[jax.ShapeDtypeStruct((m, dff), BF16)]
        + [jax.ShapeDtypeStruct(w.shape, BF16) for w, _ in to_cast],
        scratch_shapes=[pltpu.VMEM((2, d, tf), BF16)],
        compiler_params=_params(2, VMEM_LIMIT_LARGE),
        name="ffn_up",
    )(h2, w_gu, w_gu, *[w for w, _ in to_cast])


def _ffn_down_kernel(a_ref, w_ref, x_ref, o_ref):
    a = a_ref[...]
    for c0 in range(0, o_ref.shape[1], COL_BLOCK):
        cs = slice(c0, c0 + COL_BLOCK)
        o_ref[:, cs] = x_ref[:, cs] + _dot(a, w_ref[:, cs])


def _ffn_down(act, w_down, x1, tm, tn):
    m, dff = act.shape
    d = w_down.shape[1]
    return pl.pallas_call(
        _ffn_down_kernel,
        grid=(d // tn, m // tm),
        in_specs=[
            pl.BlockSpec((tm, dff), lambda j, i: (i, 0)),
            pl.BlockSpec((dff, tn), lambda j, i: (0, j)),
            pl.BlockSpec((tm, tn), lambda j, i: (i, j)),
        ],
        out_specs=pl.BlockSpec((tm, tn), lambda j, i: (i, j)),
        out_shape=jax.ShapeDtypeStruct((m, d), F32),
        compiler_params=_params(2),
        name="ffn_down",
    )(act, w_down, x1)


def _final_kernel(x_ref, pp_ref, ps_ref, wpg_ref, wple_ref, ln3_ref, lnf_ref, yp_ref, ys_ref,
                  x3_scr, *, n_prompt_tiles):
    i = pl.program_id(0)
    is_prompt = i < n_prompt_tiles
    tm, d = x_ref.shape
    x2 = x_ref[...]
    h3 = (x2 * ln3_ref[...]).astype(BF16)
    inv3 = lax.rsqrt(jnp.mean(x2 * x2, axis=-1, keepdims=True) + EPS)
    p = jnp.where(is_prompt, pp_ref[...], ps_ref[...]).astype(BF16)
    ss = jnp.zeros((tm, 1), F32)
    for c0 in range(0, d, COL_BLOCK):
        cs = slice(c0, c0 + COL_BLOCK)
        gate = jax.nn.sigmoid(inv3 * _dot(h3, wpg_ref[:, cs]))
        x3 = x_ref[:, cs] + gate * _dot(p, wple_ref[:, cs])
        x3_scr[:, cs] = x3
        ss = ss + jnp.sum(x3 * x3, axis=-1, keepdims=True)
    inv = lax.rsqrt(ss * (1.0 / d) + EPS)

    def write(y_ref):
        for c0 in range(0, d, COL_BLOCK):
            cs = slice(c0, c0 + COL_BLOCK)
            y = x3_scr[:, cs] * inv * lnf_ref[:, cs]
            if len(y_ref.shape) == 3:
                y_ref[:, :, cs] = y.reshape(y_ref.shape[0], y_ref.shape[1], COL_BLOCK)
            else:
                y_ref[:, cs] = y

    @pl.when(is_prompt)
    def _():
        write(yp_ref)

    @pl.when(jnp.logical_not(is_prompt))
    def _():
        write(ys_ref)


def _final(x2, pp, ps, w_pg, w_ple, ln3, ln_f, sample_shape, tm):
    m, d = x2.shape
    npr, ns = pp.shape[0], ps.shape[0]
    pd = pp.shape[1]
    npt = npr // tm
    nb, dseq = sample_shape
    assert nb * dseq == ns == tm
    pidx = lambda i: (jnp.minimum(i, npt - 1), 0)
    sidx = lambda i: (jnp.maximum(i - npt, 0), 0)
    return pl.pallas_call(
        functools.partial(_final_kernel, n_prompt_tiles=npt),
        grid=(m // tm,),
        in_specs=[
            pl.BlockSpec((tm, d), lambda i: (i, 0)),
            pl.BlockSpec((tm, pd), pidx),
            pl.BlockSpec((tm, pd), sidx),
            pl.BlockSpec((d, d), lambda i: (0, 0), pipeline_mode=pl.Buffered(1)),
            pl.BlockSpec((pd, d), lambda i: (0, 0)),
            pl.BlockSpec((1, d), lambda i: (0, 0)),
            pl.BlockSpec((1, d), lambda i: (0, 0)),
        ],
        out_specs=[pl.BlockSpec((tm, d), pidx),
                   pl.BlockSpec((nb, dseq, d), lambda i: (0, 0, 0))],
        out_shape=[jax.ShapeDtypeStruct((npr, d), F32), jax.ShapeDtypeStruct((nb, dseq, d), F32)],
        scratch_shapes=[pltpu.VMEM((tm, d), F32)],
        compiler_params=_params(1),
        name="final",
    )(x2, pp, ps, w_pg, w_ple, ln3, ln_f)


def kernel(x_prompt, x_sample, state_hgrn, state_gla, p_prompt, p_sample, hg_lb, ln1, w_in, hg_norm,
           gla_w_gk, gla_b_gk, gla_norm, w_branch, w_out, ln2, w_gu, w_down, ln3, w_ple, w_pg, ln_f):
    batch, seq, d = x_prompt.shape
    nb, dseq, _ = x_sample.shape
    depth = w_in.shape[0]
    assert depth == 1, "single-layer step"
    npr, ns = batch * seq, nb * dseq
    xp = x_prompt.reshape(npr, d)
    row = lambda v: v.reshape(1, -1)

    w_in0 = jnp.swapaxes(w_in[0], 0, 1)
    wgk = jnp.pad(gla_w_gk[0], ((0, LANE - GLA_GATE_RANK), (0, 0)))
    bgk = row(gla_b_gk[0])

    m = npr + ns
    tm_split, tm_quarter, tm_sixteenth = ns, m // 4, m // 16
    assert npr % tm_split == 0 and m % 64 == 0

    h1, glr, xs = _prep(xp, x_sample, row(ln1[0]), w_in0, tm=tm_split)
    z = _in_proj(h1, w_in0, n_cols=MAIN_COLS + 2 * d, tm=tm_quarter, tn=TN_WIDE)

    a_p, s_hp = _hgrn_prompt(z, hg_lb, row(hg_norm[0]), batch, seq, hpb=HG_HEADS)
    b_p, s_gp = _gla_prompt(z, glr, wgk, bgk, row(gla_norm[0]), batch, seq, hpb=GLA_HEADS)
    a_s, b_s, s_hs, s_gs = _sample_scan(z, glr, state_hgrn, state_gla, hg_lb, row(hg_norm[0]),
                                        row(gla_norm[0]), wgk, bgk, row0=npr, nb=nb, seq=dseq, bb=8)

    merged, w_out_b = _merge(z, a_p, a_s, b_p, b_s, w_branch[0], to_cast=[(w_out[0], 64)],
                             tm=tm_split, tn=TN_WIDE)
    x1, h2 = _out_proj(merged, xp, xs, w_out_b, row(ln2[0]), tm=tm_split)
    act, w_down_b, w_pg_b, w_ple_b = _ffn_up(
        h2, w_gu[0], to_cast=[(w_down[0], 128), (w_pg[0], 64), (w_ple[0], 32)],
        tm=tm_quarter, tf=TF)
    x2 = _ffn_down(act, w_down_b, x1, tm=tm_sixteenth, tn=TN_WIDE)
    yp, ys = _final(x2, p_prompt[0].reshape(npr, -1), p_sample[0].reshape(ns, -1),
                    w_pg_b, w_ple_b, row(ln3[0]), row(ln_f), sample_shape=(nb, dseq), tm=tm_split)

    return (yp.reshape(batch, seq, d), ys,
            s_hp[None], s_gp[None], s_hs, s_gs)
```
